```python
import math
import jax, jax.numpy as jnp
from jax import lax
import numpy as np

D_MODEL = 1024
BATCH = 4
SEQ = 8192
DEPTH = 2
DEC_BATCH = 32
DEC_SEQ = 4
PAST_LEN = 16384
PAGE_SIZE = 128

N_EVEN = (DEPTH + 1) // 2
N_ODD = DEPTH // 2

S5_WIDTH = D_MODEL // 2
S5_GROUP = 16
S5_GROUPS = S5_WIDTH // S5_GROUP
S5_STATE = 64
B_HEADS = 8
B_DH = 64
B_KV_HEADS = 2
B_ROT = B_DH // 4
IDX_HEADS = 4
IDX_DH = 64
IDX_ROT = IDX_DH // 4
DSA_TOPK = 256
C_HEADS = 8
C_DH = 128
C_KV = 2
C_REP = C_HEADS // C_KV
C_ROT = C_DH // 4
CMP_BLOCK = 64
SEL_BLOCK = 64
N_SEL = 16
WINDOW = 512
N_EGROUPS = 4
E_PER_GROUP = 8
N_EXPERTS = N_EGROUPS * E_PER_GROUP
TOP_E = 2
D_FF_E = 512

ROPE_THETA = 500000.0
ALPHA = (2 * DEPTH) ** 0.25
BETA = (8 * DEPTH) ** -0.25
Q_BLOCK = 128
LN_EPS = 1e-5
NEG = -1e30
FORCE = 1e4

AB_SIZES = (S5_WIDTH, B_HEADS * B_DH, B_KV_HEADS * 2 * B_DH, IDX_HEADS * IDX_DH, IDX_DH, IDX_HEADS)
AB_COLS = sum(AB_SIZES)
C_SIZES = (C_HEADS * C_DH, C_KV * 2 * C_DH, C_KV * 2 * C_DH, C_KV * 2 * C_DH, C_HEADS * 3)
C_COLS = sum(C_SIZES)

kernel_name = 'hybrid_s5_dsa_nsa_hmoe_step'


def split_cols(h, sizes):
    out, start = [], 0
    for n in sizes:
        out.append(h[..., start:start + n])
        start += n
    return out


def layer_norm(x, g, b):
    xf = x.astype(jnp.float32)
    mu = xf.mean(-1, keepdims=True)
    var = jnp.square(xf - mu).mean(-1, keepdims=True)
    return ((xf - mu) * lax.rsqrt(var + LN_EPS) * g + b).astype(x.dtype)


def rope_partial(x, pos, rot_dim):
    half = rot_dim // 2
    inv = ROPE_THETA ** (-jnp.arange(half, dtype=jnp.float32) / half)
    ang = pos.astype(jnp.float32)[:, None] * inv
    cos, sin = jnp.cos(ang)[:, None, :], jnp.sin(ang)[:, None, :]
    x1 = x[..., :half].astype(jnp.float32)
    x2 = x[..., half:rot_dim].astype(jnp.float32)
    rot = jnp.concatenate([x1 * cos - x2 * sin, x2 * cos + x1 * sin], -1).astype(x.dtype)
    return jnp.concatenate([rot, x[..., rot_dim:]], -1)


def rope_kv(kv, pos, rot_dim):
    k = rope_partial(kv[..., 0, :], pos, rot_dim)
    return jnp.stack([k, kv[..., 1, :]], axis=-2)


def to_blocks(a):
    bn, s = a.shape[:2]
    return jnp.swapaxes(a.reshape((bn, s // Q_BLOCK, Q_BLOCK) + a.shape[2:]), 0, 1)


def from_blocks(a):
    nb, bn = a.shape[:2]
    return jnp.swapaxes(a, 0, 1).reshape((bn, nb * Q_BLOCK) + a.shape[3:])


def gather_past(pool, page_table):
    rows = pool[page_table]
    return rows.reshape((page_table.shape[0], -1) + pool.shape[2:])


def paged_rows(pool, page_table, new, pos, group=None):
    past_len = page_table.shape[1] * PAGE_SIZE
    b = jnp.arange(pos.shape[0]).reshape((-1,) + (1,) * (pos.ndim - 1))
    pp = jnp.clip(pos, 0, past_len - 1)
    pn = jnp.clip(pos - past_len, 0, new.shape[1] - 1)
    phys = page_table[b, pp // PAGE_SIZE]
    if group is None:
        old, fresh = pool[phys, pp % PAGE_SIZE], new[b, pn]
    else:
        old, fresh = pool[phys, pp % PAGE_SIZE, group], new[b, pn, group]
    keep = (pos < past_len).reshape(pos.shape + (1,) * (old.ndim - pos.ndim))
    return jnp.where(keep, old, fresh)


def softmax_attend(q, k, v, mask, kspec):
    s = jnp.einsum(f'bqgrd,{kspec}->bqgrk', q, k).astype(jnp.float32) * (q.shape[-1] ** -0.5)
    s = jnp.where(mask, s, NEG)
    p = jax.nn.softmax(s, axis=-1) * jnp.any(mask, axis=-1, keepdims=True)
    return jnp.einsum(f'bqgrk,{kspec}->bqgrd', p.astype(v.dtype), v), p


def window_mask(qpos, kpos):
    m = (kpos[None, :] >= 0) & (kpos[None, :] <= qpos[:, None]) & (kpos[None, :] > qpos[:, None] - WINDOW)
    return m[None, :, None, None, :]


def _linear_combine(e1, e2):
    a1, b1 = e1
    a2, b2 = e2
    return a2 * a1, a2 * b1 + b2


def s5_mixer(u, h0, a_re, a_im, log_dt, b_re, b_im, c_re, c_im, d, w_glu):
    bn, s, _ = u.shape
    lam = lax.complex(a_re.astype(jnp.float32), a_im.astype(jnp.float32))
    dt = jnp.exp(log_dt.astype(jnp.float32))[:, None]
    lam_bar = jnp.exp(lam * dt)
    b_bar = ((lam_bar - 1.0) / lam)[:, :, None] * lax.complex(b_re.astype(jnp.float32), b_im.astype(jnp.float32))
    ug = u.astype(jnp.float32).reshape(bn, s, S5_GROUPS, S5_GROUP).astype(jnp.complex64)
    bu = jnp.einsum('gpc,bsgc->bsgp', b_bar, ug)
    bu = bu.at[:, 0].add(lam_bar * h0)
    a = jnp.broadcast_to(lam_bar, bu.shape)
    _, h = lax.associative_scan(_linear_combine, (a, bu), axis=1)
    c = lax.complex(c_re.astype(jnp.float32), c_im.astype(jnp.float32))
    y = jnp.einsum('gcp,bsgp->bsgc', c, h).real.reshape(bn, s, S5_WIDTH)
    y = jax.nn.gelu(y + d.astype(jnp.float32) * u.astype(jnp.float32)).astype(u.dtype)
    y = y * jax.nn.sigmoid(y @ w_glu)
    return y, h[:, -1]


def indexer_topk(qi, wi, ki, qpos, kpos, topk):
    s = jnp.einsum('bqhd,bkd->bqhk', qi, ki).astype(jnp.float32) * (IDX_DH ** -0.5)
    score = jnp.einsum('bqh,bqhk->bqk', wi.astype(jnp.float32), jax.nn.relu(s))
    score = jnp.where((kpos[None, :] <= qpos[:, None])[None], score, NEG)
    _, idx = lax.top_k(score, topk)
    return idx


def dsa_prompt(q, kv, qi, ki, wi):
    bn, s = q.shape[:2]
    topk = min(DSA_TOPK, s // 4)
    kpos = jnp.arange(s)
    bidx = jnp.arange(bn)[:, None, None]

    def block(args):
        qb, qib, wib, t0 = args
        qpos = t0 + jnp.arange(Q_BLOCK)
        idx = indexer_topk(qib, wib, ki, qpos, kpos, topk)
        rows = kv[bidx, idx]
        mask = (idx <= qpos[None, :, None])[:, :, None, None, :]
        o, _ = softmax_attend(qb, rows[..., 0, :], rows[..., 1, :], mask, 'bqkgd')
        return o

    o = lax.map(block, (to_blocks(q), to_blocks(qi), to_blocks(wi), jnp.arange(s // Q_BLOCK) * Q_BLOCK))
    return from_blocks(o)


def dsa_sample(q, kv, qi, ki, wi, pos, cache_kv, cache_idx, page_table):
    past_len = page_table.shape[1] * PAGE_SIZE
    L = past_len + q.shape[1]
    ki_all = jnp.concatenate([gather_past(cache_idx, page_table), ki], axis=1)
    idx = indexer_topk(qi, wi, ki_all, pos, jnp.arange(L), min(DSA_TOPK, L // 4))
    rows = paged_rows(cache_kv, page_table, kv, idx)
    mask = (idx <= pos[None, :, None])[:, :, None, None, :]
    o, _ = softmax_attend(q, rows[..., 0, :], rows[..., 1, :], mask, 'bqkgd')
    return o


def mixer_ab(x, pos, w_in, a_re, a_im, log_dt, b_re, b_im, c_re, c_im, d, w_glu, w_out,
             state=None, cache_kv=None, cache_idx=None, page_table=None):
    bn, s, _ = x.shape
    u, q, kv, qi, ki, wi = split_cols(x @ w_in, AB_SIZES)
    q = rope_partial(q.reshape(bn, s, B_HEADS, B_DH), pos, B_ROT).reshape(bn, s, B_KV_HEADS, B_HEADS // B_KV_HEADS, B_DH)
    kv = rope_kv(kv.reshape(bn, s, B_KV_HEADS, 2, B_DH), pos, B_ROT)
    qi = rope_partial(qi.reshape(bn, s, IDX_HEADS, IDX_DH), pos, IDX_ROT)
    ki = rope_partial(ki.reshape(bn, s, 1, IDX_DH), pos, IDX_ROT)[:, :, 0]
    if state is None:
        h0 = jnp.zeros((bn, S5_GROUPS, S5_STATE), jnp.complex64)
        o_b = dsa_prompt(q, kv, qi, ki, wi)
    else:
        h0 = lax.complex(state[:, 0].astype(jnp.float32), state[:, 1].astype(jnp.float32))
        o_b = dsa_sample(q, kv, qi, ki, wi, pos, cache_kv, cache_idx, page_table)
    y_s5, h_last = s5_mixer(u, h0, a_re, a_im, log_dt, b_re, b_im, c_re, c_im, d, w_glu)
    out = jnp.concatenate([y_s5, o_b.reshape(bn, s, B_HEADS * B_DH).astype(y_s5.dtype)], -1) @ w_out
    new_a = jnp.stack([h_last.real, h_last.imag], axis=1).astype(x.dtype)
    return out, new_a, kv, ki


def compress_blocks(kv, cmp_pos, cmp_w1, cmp_w2):
    bn, L = kv.shape[:2]
    nb = L // CMP_BLOCK
    blk = kv[:, :nb * CMP_BLOCK].reshape(bn, nb, CMP_BLOCK, C_KV, 2, C_DH) + cmp_pos[:, None]
    blk = blk.transpose(0, 1, 3, 4, 2, 5).reshape(bn, nb, C_KV, 2, CMP_BLOCK * C_DH)
    hid = jax.nn.gelu(jnp.einsum('bngcf,cfd->bngcd', blk, cmp_w1))
    return jnp.einsum('bngcd,cde->bngce', hid, cmp_w2)


def nsa_compressed(q, kv_all, pos, cmp_pos, cmp_w1, cmp_w2):
    kc = compress_blocks(kv_all, cmp_pos, cmp_w1, cmp_w2)
    ends = (jnp.arange(kc.shape[1]) + 1) * CMP_BLOCK - 1
    mask = (ends[None, :] <= pos[:, None])[None, :, None, None, :]
    o, p = softmax_attend(q, kc[..., 0, :], kc[..., 1, :], mask, 'bkgd')
    return o, p.sum(axis=3)


def selected_positions(imp, qpos, L):
    nbs = -(-L // SEL_BLOCK)
    imp = jnp.pad(imp.astype(jnp.float32), ((0, 0), (0, 0), (0, 0), (0, nbs - imp.shape[-1])))
    j = jnp.arange(nbs)
    cur = (qpos // SEL_BLOCK)[:, None]
    forced = ((j == 0) | (j == cur) | (j == cur - 1))[None, :, None, :]
    valid = (j * SEL_BLOCK <= qpos[:, None])[None, :, None, :]
    score = jnp.where(valid, jnp.where(forced, FORCE, imp), NEG)
    _, blk = lax.top_k(score, min(N_SEL, nbs))
    return (blk[..., None] * SEL_BLOCK + jnp.arange(SEL_BLOCK)).reshape(blk.shape[:-1] + (-1,))


def mixer_c(x, pos, w_in, cmp_pos, cmp_w1, cmp_w2, w_out,
            cache_cmp=None, cache_slc=None, win=None, page_table=None):
    bn, s, _ = x.shape
    q, kv_c, kv_s, kv_w, gates = split_cols(x @ w_in, C_SIZES)
    q = q.reshape(bn, s, C_HEADS, C_DH)
    q_raw = q.reshape(bn, s, C_KV, C_REP, C_DH)
    q_rot = rope_partial(q, pos, C_ROT).reshape(bn, s, C_KV, C_REP, C_DH)
    kv_c = kv_c.reshape(bn, s, C_KV, 2, C_DH)
    kv_s = rope_kv(kv_s.reshape(bn, s, C_KV, 2, C_DH), pos, C_ROT)
    kv_w = rope_kv(kv_w.reshape(bn, s, C_KV, 2, C_DH), pos, C_ROT)
    gidx = jnp.arange(C_KV).reshape(1, 1, C_KV, 1)
    if cache_cmp is None:
        o_cmp, imp = nsa_compressed(q_raw, kv_c, pos, cmp_pos, cmp_w1, cmp_w2)
        kv_w_pad = jnp.pad(kv_w, ((0, 0), (WINDOW, 0), (0, 0), (0, 0), (0, 0)))
        bidx = jnp.arange(bn).reshape(bn, 1, 1, 1)

        def block(args):
            qb, impb, t0 = args
            qpos = t0 + jnp.arange(Q_BLOCK)
            kpos = selected_positions(impb, qpos, s)
            rows = kv_s[bidx, kpos, gidx]
            smask = (kpos <= qpos[None, :, None, None])[:, :, :, None, :]
            o_s, _ = softmax_attend(qb, rows[..., 0, :], rows[..., 1, :], smask, 'bqgkd')
            kw = lax.dynamic_slice_in_dim(kv_w_pad, t0, WINDOW + Q_BLOCK, axis=1)
            wpos = t0 - WINDOW + jnp.arange(WINDOW + Q_BLOCK)
            o_w, _ = softmax_attend(qb, kw[..., 0, :], kw[..., 1, :], window_mask(qpos, wpos), 'bkgd')
            return o_s, o_w

        o_slc, o_win = lax.map(block, (to_blocks(q_rot), to_blocks(imp), jnp.arange(s // Q_BLOCK) * Q_BLOCK))
        o_slc, o_win = from_blocks(o_slc), from_blocks(o_win)
        new_win = kv_w[:, -min(WINDOW, s):]
    else:
        past_len = page_table.shape[1] * PAGE_SIZE
        L = past_len + s
        kv_c_all = jnp.concatenate([gather_past(cache_cmp, page_table), kv_c], axis=1)
        o_cmp, imp = nsa_compressed(q_raw, kv_c_all, pos, cmp_pos, cmp_w1, cmp_w2)
        kpos = selected_positions(imp, pos, L)
        rows = paged_rows(cache_slc, page_table, kv_s, kpos, gidx)
        smask = (kpos <= pos[None, :, None, None])[:, :, :, None, :]
        o_slc, _ = softmax_attend(q_rot, rows[..., 0, :], rows[..., 1, :], smask, 'bqgkd')
        kw = jnp.concatenate([win, kv_w], axis=1)
        wpos = past_len - win.shape[1] + jnp.arange(kw.shape[1])
        o_win, _ = softmax_attend(q_rot, kw[..., 0, :], kw[..., 1, :], window_mask(pos, wpos), 'bkgd')
        new_win = kw[:, -win.shape[1]:]
    g = jax.nn.sigmoid(gates.reshape(bn, s, C_KV, C_REP, 3))
    o = g[..., 0:1] * o_cmp + g[..., 1:2] * o_slc + g[..., 2:3] * o_win
    return o.reshape(bn, s, C_HEADS * C_DH) @ w_out, kv_c, kv_s, new_win


def hier_moe(x, w_group, w_expert, w_gate, w_up, w_down):
    shape = x.shape
    xt = x.reshape(-1, shape[-1])
    g_logits = (xt @ w_group).astype(jnp.float32)
    g_top = jnp.argmax(g_logits, axis=-1)
    g_w = jnp.take_along_axis(jax.nn.softmax(g_logits, axis=-1), g_top[:, None], axis=-1)
    e_logits = (xt @ w_expert).astype(jnp.float32).reshape(-1, N_EGROUPS, E_PER_GROUP)
    e_logits = jnp.take_along_axis(e_logits, g_top[:, None, None], axis=1)[:, 0]
    e_val, e_idx = lax.top_k(e_logits, TOP_E)
    e_w = jax.nn.softmax(e_val, axis=-1) * g_w
    onehot = jax.nn.one_hot(g_top[:, None] * E_PER_GROUP + e_idx, N_EXPERTS, dtype=jnp.float32)
    combine = jnp.einsum('tk,tke->te', e_w, onehot).astype(x.dtype)
    y = jnp.zeros_like(xt)
    for e in range(N_EXPERTS):
        h = jax.nn.silu(xt @ w_gate[e]) * (xt @ w_up[e])
        y = y + combine[:, e:e + 1] * (h @ w_down[e])
    return y.reshape(shape)


def setup_inputs(seed: int = 0) -> dict:
    key = jax.random.key(seed)
    ks = iter(jax.random.split(key, 48))
    f32 = jnp.float32

    def nrm(shape, scale):
        return jax.random.normal(next(ks), shape, f32) * scale

    n_pages = PAST_LEN // PAGE_SIZE
    n_used = DEC_BATCH * n_pages
    n_pool = n_used + max(1, n_used // 4)
    win_buf = min(WINDOW, PAST_LEN)
    page_table = jax.random.permutation(next(ks), n_pool)[:n_used].reshape(DEC_BATCH, n_pages).astype(jnp.int32)
    mix_w = B_HEADS * B_DH + S5_WIDTH
    return {
        'x_prompt': nrm((BATCH, SEQ, D_MODEL), 1.0),
        'x_sample': nrm((DEC_BATCH, DEC_SEQ, D_MODEL), 1.0),
        'state_a': nrm((N_EVEN, DEC_BATCH, 2, S5_GROUPS, S5_STATE), 0.1),
        'cache_b_kv': nrm((N_EVEN, n_pool, PAGE_SIZE, B_KV_HEADS, 2, B_DH), 1.0),
        'cache_b_idx': nrm((N_EVEN, n_pool, PAGE_SIZE, IDX_DH), 1.0),
        'cache_c_cmp': nrm((N_ODD, n_pool, PAGE_SIZE, C_KV, 2, C_DH), 1.0),
        'cache_c_slc': nrm((N_ODD, n_pool, PAGE_SIZE, C_KV, 2, C_DH), 1.0),
        'state_c_win': nrm((N_ODD, DEC_BATCH, win_buf, C_KV, 2, C_DH), 1.0),
        'page_table': page_table,
        'w_in_ab': nrm((N_EVEN, D_MODEL, AB_COLS), D_MODEL ** -0.5),
        's5_a_re': -0.5 + nrm((N_EVEN, S5_GROUPS, S5_STATE), 0.01),
        's5_a_im': jnp.pi * jnp.arange(S5_STATE, dtype=f32) + nrm((N_EVEN, S5_GROUPS, S5_STATE), 0.01),
        's5_log_dt': jax.random.uniform(next(ks), (N_EVEN, S5_GROUPS), f32, math.log(1e-3), math.log(1e-1)),
        's5_b_re': nrm((N_EVEN, S5_GROUPS, S5_STATE, S5_GROUP), (2 * S5_GROUP) ** -0.5),
        's5_b_im': nrm((N_EVEN, S5_GROUPS, S5_STATE, S5_GROUP), (2 * S5_GROUP) ** -0.5),
        's5_c_re': nrm((N_EVEN, S5_GROUPS, S5_GROUP, S5_STATE), (2 * S5_STATE) ** -0.5),
        's5_c_im': nrm((N_EVEN, S5_GROUPS, S5_GROUP, S5_STATE), (2 * S5_STATE) ** -0.5),
        's5_d': nrm((N_EVEN, S5_WIDTH), 0.5),
        's5_w_glu': nrm((N_EVEN, S5_WIDTH, S5_WIDTH), S5_WIDTH ** -0.5),
        'w_out_ab': nrm((N_EVEN, mix_w, D_MODEL), BETA * mix_w ** -0.5),
        'w_in_c': nrm((N_ODD, D_MODEL, C_COLS), D_MODEL ** -0.5),
        'cmp_pos': nrm((N_ODD, CMP_BLOCK, 2, C_DH), 0.02),
        'cmp_w1': nrm((N_ODD, 2, CMP_BLOCK * C_DH, C_DH), (CMP_BLOCK * C_DH) ** -0.5),
        'cmp_w2': nrm((N_ODD, 2, C_DH, C_DH), C_DH ** -0.5),
        'w_out_c': nrm((N_ODD, C_HEADS * C_DH, D_MODEL), BETA * (C_HEADS * C_DH) ** -0.5),
        'ln1_g': 1.0 + nrm((DEPTH, D_MODEL), 0.01),
        'ln1_b': nrm((DEPTH, D_MODEL), 0.01),
        'ln2_g': 1.0 + nrm((DEPTH, D_MODEL), 0.01),
        'ln2_b': nrm((DEPTH, D_MODEL), 0.01),
        'moe_w_group': nrm((DEPTH, D_MODEL, N_EGROUPS), D_MODEL ** -0.5),
        'moe_w_expert': nrm((DEPTH, D_MODEL, N_EXPERTS), D_MODEL ** -0.5),
        'moe_w_gate': nrm((DEPTH, N_EXPERTS, D_MODEL, D_FF_E), D_MODEL ** -0.5),
        'moe_w_up': nrm((DEPTH, N_EXPERTS, D_MODEL, D_FF_E), D_MODEL ** -0.5),
        'moe_w_down': nrm((DEPTH, N_EXPERTS, D_FF_E, D_MODEL), BETA * D_FF_E ** -0.5),
    }


def reference(x_prompt, x_sample, state_a, cache_b_kv, cache_b_idx, cache_c_cmp, cache_c_slc, state_c_win, page_table,
              w_in_ab, s5_a_re, s5_a_im, s5_log_dt, s5_b_re, s5_b_im, s5_c_re, s5_c_im, s5_d, s5_w_glu, w_out_ab,
              w_in_c, cmp_pos, cmp_w1, cmp_w2, w_out_c,
              ln1_g, ln1_b, ln2_g, ln2_b, moe_w_group, moe_w_expert, moe_w_gate, moe_w_up, moe_w_down):
    past_len = page_table.shape[1] * PAGE_SIZE
    pos_p = jnp.arange(x_prompt.shape[1])
    pos_s = past_len + jnp.arange(x_sample.shape[1])
    xp, xs = x_prompt, x_sample
    a_p, a_s, bkv_p, bkv_s, bidx_p, bidx_s = [], [], [], [], [], []
    ccmp_p, ccmp_s, cslc_p, cslc_s, cwin_p, cwin_s = [], [], [], [], [], []
    for i in range(DEPTH):
        j = i // 2
        if i % 2 == 0:
            ab = (w_in_ab[j], s5_a_re[j], s5_a_im[j], s5_log_dt[j], s5_b_re[j], s5_b_im[j],
                  s5_c_re[j], s5_c_im[j], s5_d[j], s5_w_glu[j], w_out_ab[j])
            mp, st_p, kv_p, ki_p = mixer_ab(xp, pos_p, *ab)
            ms, st_s, kv_n, ki_n = mixer_ab(xs, pos_s, *ab, state=state_a[j], cache_kv=cache_b_kv[j],
                                            cache_idx=cache_b_idx[j], page_table=page_table)
            a_p.append(st_p); a_s.append(st_s)
            bkv_p.append(kv_p); bkv_s.append(kv_n)
            bidx_p.append(ki_p); bidx_s.append(ki_n)
        else:
            cp = (w_in_c[j], cmp_pos[j], cmp_w1[j], cmp_w2[j], w_out_c[j])
            mp, kc_p, ksl_p, win_p = mixer_c(xp, pos_p, *cp)
            ms, kc_n, ksl_n, win_n = mixer_c(xs, pos_s, *cp, cache_cmp=cache_c_cmp[j], cache_slc=cache_c_slc[j],
                                             win=state_c_win[j], page_table=page_table)
            ccmp_p.append(kc_p); ccmp_s.append(kc_n)
            cslc_p.append(ksl_p); cslc_s.append(ksl_n)
            cwin_p.append(win_p); cwin_s.append(win_n)
        xp = layer_norm(ALPHA * xp + mp, ln1_g[i], ln1_b[i])
        xs = layer_norm(ALPHA * xs + ms, ln1_g[i], ln1_b[i])
        moe = (moe_w_group[i], moe_w_expert[i], moe_w_gate[i], moe_w_up[i], moe_w_down[i])
        xp = layer_norm(ALPHA * xp + hier_moe(xp, *moe), ln2_g[i], ln2_b[i])
        xs = layer_norm(ALPHA * xs + hier_moe(xs, *moe), ln2_g[i], ln2_b[i])
    return (xp, xs, jnp.stack(a_p), jnp.stack(a_s), jnp.stack(bkv_p), jnp.stack(bkv_s),
            jnp.stack(bidx_p), jnp.stack(bidx_s), jnp.stack(ccmp_p), jnp.stack(ccmp_s),
            jnp.stack(cslc_p), jnp.stack(cslc_s), jnp.stack(cwin_p), jnp.stack(cwin_s))
```

```python
import functools
import math

import jax
import jax.numpy as jnp
import numpy as np
from jax import lax
from jax.experimental import pallas as pl
from jax.experimental.pallas import tpu as pltpu

D_MODEL = 1024
DEPTH = 2
PAGE_SIZE = 128

S5_WIDTH = D_MODEL // 2
S5_GROUP = 16
S5_GROUPS = S5_WIDTH // S5_GROUP
S5_STATE = 64
B_HEADS = 8
B_DH = 64
B_KV_HEADS = 2
B_ROT = B_DH // 4
IDX_HEADS = 4
IDX_DH = 64
IDX_ROT = IDX_DH // 4
DSA_TOPK = 256
C_HEADS = 8
C_DH = 128
C_KV = 2
C_REP = C_HEADS // C_KV
C_ROT = C_DH // 4
CMP_BLOCK = 64
SEL_BLOCK = 64
N_SEL = 16
WINDOW = 512
N_EGROUPS = 4
E_PER_GROUP = 8
N_EXPERTS = N_EGROUPS * E_PER_GROUP
TOP_E = 2
D_FF_E = 512

ROPE_THETA = 500000.0
ALPHA = (2 * DEPTH) ** 0.25
Q_BLOCK = 128
LN_EPS = 1e-5
NEG = -1e30
FORCE = 1e4

AB_SIZES = (S5_WIDTH, B_HEADS * B_DH, B_KV_HEADS * 2 * B_DH, IDX_HEADS * IDX_DH, IDX_DH, IDX_HEADS)
C_SIZES = (C_HEADS * C_DH, C_KV * 2 * C_DH, C_KV * 2 * C_DH, C_KV * 2 * C_DH, C_HEADS * 3)

V7X_LANES = 128
V7X_VMEM_BYTES = 64 * 1024 * 1024
VMEM_LIMIT_BYTES = 48 * 1024 * 1024

MOE_ROUTER_TILE = 512
MOE_EXPERT_TILE = 256


def _moe_router_kernel(x_ref, w_ref, o_ref):
    logits = jnp.dot(x_ref[...], w_ref[...], precision=lax.Precision.HIGHEST,
                     preferred_element_type=jnp.float32)
    col = lax.broadcasted_iota(jnp.int32, logits.shape, 1)
    big = jnp.int32(1 << 20)
    ninf = jnp.float32(-jnp.inf)
    gl = jnp.where(col < N_EGROUPS, logits, ninf)
    gmax = jnp.max(gl, axis=-1, keepdims=True)
    g_top = jnp.min(jnp.where(gl == gmax, col, big), axis=-1, keepdims=True)
    g_w = 1.0 / jnp.sum(jnp.exp(gl - gmax), axis=-1, keepdims=True)
    lo = N_EGROUPS + E_PER_GROUP * g_top
    el = jnp.where((col >= lo) & (col < lo + E_PER_GROUP), logits, ninf)
    v1 = jnp.max(el, axis=-1, keepdims=True)
    i1 = jnp.min(jnp.where(el == v1, col, big), axis=-1, keepdims=True)
    el2 = jnp.where(col == i1, ninf, el)
    v2 = jnp.max(el2, axis=-1, keepdims=True)
    i2 = jnp.min(jnp.where(el2 == v2, col, big), axis=-1, keepdims=True)
    e2 = jnp.exp(v2 - v1)
    den = 1.0 + e2
    w1 = g_w / den
    w2 = g_w * e2 / den
    out = jnp.where(col == 0, (i1 - N_EGROUPS).astype(jnp.float32),
                    jnp.where(col == 1, (i2 - N_EGROUPS).astype(jnp.float32),
                              jnp.where(col == 2, w1, jnp.where(col == 3, w2, 0.0))))
    o_ref[...] = out


def _moe_route(x, w_router):
    t = x.shape[0]
    tm = min(MOE_ROUTER_TILE, t)
    assert t % tm == 0
    return pl.pallas_call(
        _moe_router_kernel,
        grid=(t // tm,),
        in_specs=[pl.BlockSpec((tm, D_MODEL), lambda i: (i, 0)),
                  pl.BlockSpec((D_MODEL, V7X_LANES), lambda i: (0, 0))],
        out_specs=pl.BlockSpec((tm, V7X_LANES), lambda i: (i, 0)),
        out_shape=jax.ShapeDtypeStruct((t, V7X_LANES), jnp.float32),
        name="moe_router",
    )(x, w_router)


def _moe_expert_kernel(tile_e_ref, nvalid_ref, xs_ref, wg_ref, wu_ref, wd_ref, sw_ref, o_ref,
                       wg_s, wu_s, wd_s):
    i = pl.program_id(0)
    valid = i < nvalid_ref[0]
    prev_e = tile_e_ref[jnp.maximum(i - 1, 0)]
    new_expert = jnp.logical_or(i == 0, tile_e_ref[i] != prev_e)

    @pl.when(jnp.logical_and(valid, new_expert))
    def _():
        wg_s[...] = wg_ref[0].astype(jnp.bfloat16)
        wu_s[...] = wu_ref[0].astype(jnp.bfloat16)
        wd_s[...] = wd_ref[0].astype(jnp.bfloat16)

    @pl.when(valid)
    def _():
        x = xs_ref[...]
        g = jnp.dot(x, wg_s[...], preferred_element_type=jnp.float32)
        u = jnp.dot(x, wu_s[...], preferred_element_type=jnp.float32)
        h = (g * jax.nn.sigmoid(g)) * u
        y = jnp.dot(h.astype(jnp.bfloat16), wd_s[...], preferred_element_type=jnp.float32)
        o_ref[...] = y * sw_ref[...]

    @pl.when(jnp.logical_not(valid))
    def _():
        o_ref[...] = jnp.zeros_like(o_ref)


def _moe_experts(tile_e, nvalid, xs, slot_w, w_gate, w_up, w_down):
    np_rows = xs.shape[0]
    tm = MOE_EXPERT_TILE
    n_tiles = np_rows // tm
    grid_spec = pltpu.PrefetchScalarGridSpec(
        num_scalar_prefetch=2,
        grid=(n_tiles,),
        in_specs=[
            pl.BlockSpec((tm, D_MODEL), lambda i, te, nv: (i, 0)),
            pl.BlockSpec((1, D_MODEL, D_FF_E), lambda i, te, nv: (te[i], 0, 0)),
            pl.BlockSpec((1, D_MODEL, D_FF_E), lambda i, te, nv: (te[i], 0, 0)),
            pl.BlockSpec((1, D_FF_E, D_MODEL), lambda i, te, nv: (te[i], 0, 0)),
            pl.BlockSpec((tm, 1), lambda i, te, nv: (i, 0)),
        ],
        out_specs=pl.BlockSpec((tm, D_MODEL), lambda i, te, nv: (i, 0)),
        scratch_shapes=[pltpu.VMEM((D_MODEL, D_FF_E), jnp.bfloat16),
                        pltpu.VMEM((D_MODEL, D_FF_E), jnp.bfloat16),
                        pltpu.VMEM((D_FF_E, D_MODEL), jnp.bfloat16)],
    )
    return pl.pallas_call(
        _moe_expert_kernel,
        grid_spec=grid_spec,
        out_shape=jax.ShapeDtypeStruct((np_rows, D_MODEL), jnp.float32),
        compiler_params=pltpu.CompilerParams(dimension_semantics=("arbitrary",),
                                             vmem_limit_bytes=VMEM_LIMIT_BYTES),
        name="moe_experts",
    )(tile_e, nvalid, xs, w_gate, w_up, w_down, slot_w)


def _moe_padded_rows(t):
    tm = MOE_EXPERT_TILE
    slots = t * TOP_E
    return ((slots + N_EXPERTS * (tm - 1)) // tm + 1) * tm


def hier_moe(x, w_group, w_expert, w_gate, w_up, w_down):
    t = x.shape[0]
    tm = MOE_EXPERT_TILE
    w_router = jnp.zeros((D_MODEL, V7X_LANES), jnp.float32)
    w_router = w_router.at[:, :N_EGROUPS].set(w_group).at[:, N_EGROUPS:N_EGROUPS + N_EXPERTS].set(w_expert)
    routed = _moe_route(x, w_router)
    eid = routed[:, 0:2].astype(jnp.int32).reshape(-1)
    ew = routed[:, 2:4].reshape(-1)
    n_slots = t * TOP_E
    np_rows = _moe_padded_rows(t)

    order = jnp.argsort(eid, stable=True).astype(jnp.int32)
    e_sorted = eid[order]
    counts = jnp.zeros((N_EXPERTS,), jnp.int32).at[eid].add(1)
    padded = ((counts + tm - 1) // tm) * tm
    pad_end = jnp.cumsum(padded)
    pad_off = pad_end - padded
    off = jnp.cumsum(counts) - counts
    dest = pad_off[e_sorted] + (jnp.arange(n_slots, dtype=jnp.int32) - off[e_sorted])
    src_tok = jnp.zeros((np_rows,), jnp.int32).at[dest].set(order // TOP_E)
    slot_w = jnp.zeros((np_rows,), jnp.float32).at[dest].set(ew[order])
    pos = jnp.zeros((n_slots,), jnp.int32).at[order].set(dest)
    tile_start = jnp.arange(np_rows // tm, dtype=jnp.int32) * tm
    tile_e = jnp.minimum(jnp.searchsorted(pad_end, tile_start, side='right'), N_EXPERTS - 1).astype(jnp.int32)
    nvalid = (pad_end[-1] // tm).astype(jnp.int32).reshape(1)

    xs = x.astype(jnp.bfloat16)[src_tok]
    ys = _moe_experts(tile_e, nvalid, xs, slot_w[:, None], w_gate, w_up, w_down)
    pos2 = pos.reshape(t, TOP_E)
    return ys[pos2[:, 0]] + ys[pos2[:, 1]]


def split_cols(h, sizes):
    out, start = [], 0
    for n in sizes:
        out.append(h[..., start:start + n])
        start += n
    return out


def layer_norm(x, g, b):
    mu = x.mean(-1, keepdims=True)
    var = jnp.square(x - mu).mean(-1, keepdims=True)
    return (x - mu) * lax.rsqrt(var + LN_EPS) * g + b


def rope_partial(x, pos, rot_dim):
    half = rot_dim // 2
    inv = ROPE_THETA ** (-jnp.arange(half, dtype=jnp.float32) / half)
    ang = pos.astype(jnp.float32)[:, None] * inv
    cos, sin = jnp.cos(ang)[:, None, :], jnp.sin(ang)[:, None, :]
    x1 = x[..., :half]
    x2 = x[..., half:rot_dim]
    rot = jnp.concatenate([x1 * cos - x2 * sin, x2 * cos + x1 * sin], -1)
    return jnp.concatenate([rot, x[..., rot_dim:]], -1)


def rope_kv(kv, pos, rot_dim):
    k = rope_partial(kv[..., 0, :], pos, rot_dim)
    return jnp.stack([k, kv[..., 1, :]], axis=-2)


def to_blocks(a):
    bn, s = a.shape[:2]
    return jnp.swapaxes(a.reshape((bn, s // Q_BLOCK, Q_BLOCK) + a.shape[2:]), 0, 1)


def from_blocks(a):
    nb, bn = a.shape[:2]
    return jnp.swapaxes(a, 0, 1).reshape((bn, nb * Q_BLOCK) + a.shape[3:])


def gather_past(pool, page_table):
    rows = pool[page_table]
    return rows.reshape((page_table.shape[0], -1) + pool.shape[2:])


def paged_rows(pool, page_table, new, pos, group=None):
    past_len = page_table.shape[1] * PAGE_SIZE
    b = jnp.arange(pos.shape[0]).reshape((-1,) + (1,) * (pos.ndim - 1))
    pp = jnp.clip(pos, 0, past_len - 1)
    pn = jnp.clip(pos - past_len, 0, new.shape[1] - 1)
    phys = page_table[b, pp // PAGE_SIZE]
    if group is None:
        old, fresh = pool[phys, pp % PAGE_SIZE], new[b, pn]
    else:
        old, fresh = pool[phys, pp % PAGE_SIZE, group], new[b, pn, group]
    keep = (pos < past_len).reshape(pos.shape + (1,) * (old.ndim - pos.ndim))
    return jnp.where(keep, old, fresh)


def softmax_attend(q, k, v, mask, kspec):
    s = jnp.einsum(f'bqgrd,{kspec}->bqgrk', q, k).astype(jnp.float32) * (q.shape[-1] ** -0.5)
    s = jnp.where(mask, s, NEG)
    p = jax.nn.softmax(s, axis=-1) * jnp.any(mask, axis=-1, keepdims=True)
    return jnp.einsum(f'bqgrk,{kspec}->bqgrd', p.astype(v.dtype), v), p


def window_mask(qpos, kpos):
    m = (kpos[None, :] >= 0) & (kpos[None, :] <= qpos[:, None]) & (kpos[None, :] > qpos[:, None] - WINDOW)
    return m[None, :, None, None, :]


def _linear_combine(e1, e2):
    a1, b1 = e1
    a2, b2 = e2
    return a2 * a1, a2 * b1 + b2


def s5_mixer(u, h0, a_re, a_im, log_dt, b_re, b_im, c_re, c_im, d, w_glu):
    bn, s, _ = u.shape
    lam = lax.complex(a_re, a_im)
    dt = jnp.exp(log_dt)[:, None]
    lam_bar = jnp.exp(lam * dt)
    b_bar = ((lam_bar - 1.0) / lam)[:, :, None] * lax.complex(b_re, b_im)
    ug = u.reshape(bn, s, S5_GROUPS, S5_GROUP).astype(jnp.complex64)
    bu = jnp.einsum('gpc,bsgc->bsgp', b_bar, ug)
    bu = bu.at[:, 0].add(lam_bar * h0)
    a = jnp.broadcast_to(lam_bar, bu.shape)
    _, h = lax.associative_scan(_linear_combine, (a, bu), axis=1)
    c = lax.complex(c_re, c_im)
    y = jnp.einsum('gcp,bsgp->bsgc', c, h).real.reshape(bn, s, S5_WIDTH)
    y = jax.nn.gelu(y + d * u)
    y = y * jax.nn.sigmoid(y @ w_glu)
    return y, h[:, -1]


def indexer_topk(qi, wi, ki, qpos, kpos, topk):
    s = jnp.einsum('bqhd,bkd->bqhk', qi, ki) * (IDX_DH ** -0.5)
    score = jnp.einsum('bqh,bqhk->bqk', wi, jax.nn.relu(s))
    score = jnp.where((kpos[None, :] <= qpos[:, None])[None], score, NEG)
    _, idx = lax.top_k(score, topk)
    return idx


def dsa_prompt(q, kv, qi, ki, wi):
    bn, s = q.shape[:2]
    topk = min(DSA_TOPK, s // 4)
    kpos = jnp.arange(s)
    bidx = jnp.arange(bn)[:, None, None]

    def block(args):
        qb, qib, wib, t0 = args
        qpos = t0 + jnp.arange(Q_BLOCK)
        idx = indexer_topk(qib, wib, ki, qpos, kpos, topk)
        rows = kv[bidx, idx]
        mask = (idx <= qpos[None, :, None])[:, :, None, None, :]
        o, _ = softmax_attend(qb, rows[..., 0, :], rows[..., 1, :], mask, 'bqkgd')
        return o

    o = lax.map(block, (to_blocks(q), to_blocks(qi), to_blocks(wi), jnp.arange(s // Q_BLOCK) * Q_BLOCK))
    return from_blocks(o)


def dsa_sample(q, kv, qi, ki, wi, pos, cache_kv, cache_idx, page_table):
    past_len = page_table.shape[1] * PAGE_SIZE
    L = past_len + q.shape[1]
    ki_all = jnp.concatenate([gather_past(cache_idx, page_table), ki], axis=1)
    idx = indexer_topk(qi, wi, ki_all, pos, jnp.arange(L), min(DSA_TOPK, L // 4))
    rows = paged_rows(cache_kv, page_table, kv, idx)
    mask = (idx <= pos[None, :, None])[:, :, None, None, :]
    o, _ = softmax_attend(q, rows[..., 0, :], rows[..., 1, :], mask, 'bqkgd')
    return o


def mixer_ab(x, pos, w_in, a_re, a_im, log_dt, b_re, b_im, c_re, c_im, d, w_glu, w_out,
             state=None, cache_kv=None, cache_idx=None, page_table=None):
    bn, s, _ = x.shape
    u, q, kv, qi, ki, wi = split_cols(x @ w_in, AB_SIZES)
    q = rope_partial(q.reshape(bn, s, B_HEADS, B_DH), pos, B_ROT).reshape(bn, s, B_KV_HEADS, B_HEADS // B_KV_HEADS, B_DH)
    kv = rope_kv(kv.reshape(bn, s, B_KV_HEADS, 2, B_DH), pos, B_ROT)
    qi = rope_partial(qi.reshape(bn, s, IDX_HEADS, IDX_DH), pos, IDX_ROT)
    ki = rope_partial(ki.reshape(bn, s, 1, IDX_DH), pos, IDX_ROT)[:, :, 0]
    if state is None:
        h0 = jnp.zeros((bn, S5_GROUPS, S5_STATE), jnp.complex64)
        o_b = dsa_prompt(q, kv, qi, ki, wi)
    else:
        h0 = lax.complex(state[:, 0], state[:, 1])
        o_b = dsa_sample(q, kv, qi, ki, wi, pos, cache_kv, cache_idx, page_table)
    y_s5, h_last = s5_mixer(u, h0, a_re, a_im, log_dt, b_re, b_im, c_re, c_im, d, w_glu)
    out = jnp.concatenate([y_s5, o_b.reshape(bn, s, B_HEADS * B_DH)], -1) @ w_out
    new_a = jnp.stack([h_last.real, h_last.imag], axis=1)
    return out, new_a, kv, ki


def compress_blocks(kv, cmp_pos, cmp_w1, cmp_w2):
    bn, L = kv.shape[:2]
    nb = L // CMP_BLOCK
    blk = kv[:, :nb * CMP_BLOCK].reshape(bn, nb, CMP_BLOCK, C_KV, 2, C_DH) + cmp_pos[:, None]
    blk = blk.transpose(0, 1, 3, 4, 2, 5).reshape(bn, nb, C_KV, 2, CMP_BLOCK * C_DH)
    hid = jax.nn.gelu(jnp.einsum('bngcf,cfd->bngcd', blk, cmp_w1))
    return jnp.einsum('bngcd,cde->bngce', hid, cmp_w2)


def nsa_compressed(q, kv_all, pos, cmp_pos, cmp_w1, cmp_w2):
    kc = compress_blocks(kv_all, cmp_pos, cmp_w1, cmp_w2)
    ends = (jnp.arange(kc.shape[1]) + 1) * CMP_BLOCK - 1
    mask = (ends[None, :] <= pos[:, None])[None, :, None, None, :]
    o, p = softmax_attend(q, kc[..., 0, :], kc[..., 1, :], mask, 'bkgd')
    return o, p.sum(axis=3)


def selected_positions(imp, qpos, L):
    nbs = -(-L // SEL_BLOCK)
    imp = jnp.pad(imp, ((0, 0), (0, 0), (0, 0), (0, nbs - imp.shape[-1])))
    j = jnp.arange(nbs)
    cur = (qpos // SEL_BLOCK)[:, None]
    forced = ((j == 0) | (j == cur) | (j == cur - 1))[None, :, None, :]
    valid = (j * SEL_BLOCK <= qpos[:, None])[None, :, None, :]
    score = jnp.where(valid, jnp.where(forced, FORCE, imp), NEG)
    _, blk = lax.top_k(score, min(N_SEL, nbs))
    return (blk[..., None] * SEL_BLOCK + jnp.arange(SEL_BLOCK)).reshape(blk.shape[:-1] + (-1,))


def mixer_c(x, pos, w_in, cmp_pos, cmp_w1, cmp_w2, w_out,
            cache_cmp=None, cache_slc=None, win=None, page_table=None):
    bn, s, _ = x.shape
    q, kv_c, kv_s, kv_w, gates = split_cols(x @ w_in, C_SIZES)
    q = q.reshape(bn, s, C_HEADS, C_DH)
    q_raw = q.reshape(bn, s, C_KV, C_REP, C_DH)
    q_rot = rope_partial(q, pos, C_ROT).reshape(bn, s, C_KV, C_REP, C_DH)
    kv_c = kv_c.reshape(bn, s, C_KV, 2, C_DH)
    kv_s = rope_kv(kv_s.reshape(bn, s, C_KV, 2, C_DH), pos, C_ROT)
    kv_w = rope_kv(kv_w.reshape(bn, s, C_KV, 2, C_DH), pos, C_ROT)
    gidx = jnp.arange(C_KV).reshape(1, 1, C_KV, 1)
    if cache_cmp is None:
        o_cmp, imp = nsa_compressed(q_raw, kv_c, pos, cmp_pos, cmp_w1, cmp_w2)
        kv_w_pad = jnp.pad(kv_w, ((0, 0), (WINDOW, 0), (0, 0), (0, 0), (0, 0)))
        bidx = jnp.arange(bn).reshape(bn, 1, 1, 1)

        def block(args):
            qb, impb, t0 = args
            qpos = t0 + jnp.arange(Q_BLOCK)
            kpos = selected_positions(impb, qpos, s)
            rows = kv_s[bidx, kpos, gidx]
            smask = (kpos <= qpos[None, :, None, None])[:, :, :, None, :]
            o_s, _ = softmax_attend(qb, rows[..., 0, :], rows[..., 1, :], smask, 'bqgkd')
            kw = lax.dynamic_slice_in_dim(kv_w_pad, t0, WINDOW + Q_BLOCK, axis=1)
            wpos = t0 - WINDOW + jnp.arange(WINDOW + Q_BLOCK)
            o_w, _ = softmax_attend(qb, kw[..., 0, :], kw[..., 1, :], window_mask(qpos, wpos), 'bkgd')
            return o_s, o_w

        o_slc, o_win = lax.map(block, (to_blocks(q_rot), to_blocks(imp), jnp.arange(s // Q_BLOCK) * Q_BLOCK))
        o_slc, o_win = from_blocks(o_slc), from_blocks(o_win)
        new_win = kv_w[:, -min(WINDOW, s):]
    else:
        past_len = page_table.shape[1] * PAGE_SIZE
        L = past_len + s
        kv_c_all = jnp.concatenate([gather_past(cache_cmp, page_table), kv_c], axis=1)
        o_cmp, imp = nsa_compressed(q_raw, kv_c_all, pos, cmp_pos, cmp_w1, cmp_w2)
        kpos = selected_positions(imp, pos, L)
        rows = paged_rows(cache_slc, page_table, kv_s, kpos, gidx)
        smask = (kpos <= pos[None, :, None, None])[:, :, :, None, :]
        o_slc, _ = softmax_attend(q_rot, rows[..., 0, :], rows[..., 1, :], smask, 'bqgkd')
        kw = jnp.concatenate([win, kv_w], axis=1)
        wpos = past_len - win.shape[1] + jnp.arange(kw.shape[1])
        o_win, _ = softmax_attend(q_rot, kw[..., 0, :], kw[..., 1, :], window_mask(pos, wpos), 'bkgd')
        new_win = kw[:, -win.shape[1]:]
    g = jax.nn.sigmoid(gates.reshape(bn, s, C_KV, C_REP, 3))
    o = g[..., 0:1] * o_cmp + g[..., 1:2] * o_slc + g[..., 2:3] * o_win
    return o.reshape(bn, s, C_HEADS * C_DH) @ w_out, kv_c, kv_s, new_win


def kernel(x_prompt, x_sample, state_a, cache_b_kv, cache_b_idx, cache_c_cmp, cache_c_slc, state_c_win, page_table,
           w_in_ab, s5_a_re, s5_a_im, s5_log_dt, s5_b_re, s5_b_im, s5_c_re, s5_c_im, s5_d, s5_w_glu, w_out_ab,
           w_in_c, cmp_pos, cmp_w1, cmp_w2, w_out_c,
           ln1_g, ln1_b, ln2_g, ln2_b, moe_w_group, moe_w_expert, moe_w_gate, moe_w_up, moe_w_down):
    past_len = page_table.shape[1] * PAGE_SIZE
    pos_p = jnp.arange(x_prompt.shape[1])
    pos_s = past_len + jnp.arange(x_sample.shape[1])
    xp, xs = x_prompt, x_sample
    pshape, sshape = xp.shape, xs.shape
    n_p = pshape[0] * pshape[1]
    outs = {k: [] for k in ('a_p', 'a_s', 'bkv_p', 'bkv_s', 'bidx_p', 'bidx_s',
                            'ccmp_p', 'ccmp_s', 'cslc_p', 'cslc_s', 'cwin_p', 'cwin_s')}
    for i in range(DEPTH):
        j = i // 2
        if i % 2 == 0:
            ab = (w_in_ab[j], s5_a_re[j], s5_a_im[j], s5_log_dt[j], s5_b_re[j], s5_b_im[j],
                  s5_c_re[j], s5_c_im[j], s5_d[j], s5_w_glu[j], w_out_ab[j])
            mp, st_p, kv_p, ki_p = mixer_ab(xp, pos_p, *ab)
            ms, st_s, kv_n, ki_n = mixer_ab(xs, pos_s, *ab, state=state_a[j], cache_kv=cache_b_kv[j],
                                            cache_idx=cache_b_idx[j], page_table=page_table)
            outs['a_p'].append(st_p); outs['a_s'].append(st_s)
            outs['bkv_p'].append(kv_p); outs['bkv_s'].append(kv_n)
            outs['bidx_p'].append(ki_p); outs['bidx_s'].append(ki_n)
        else:
            cp = (w_in_c[j], cmp_pos[j], cmp_w1[j], cmp_w2[j], w_out_c[j])
            mp, kc_p, ksl_p, win_p = mixer_c(xp, pos_p, *cp)
            ms, kc_n, ksl_n, win_n = mixer_c(xs, pos_s, *cp, cache_cmp=cache_c_cmp[j], cache_slc=cache_c_slc[j],
                                             win=state_c_win[j], page_table=page_table)
            outs['ccmp_p'].append(kc_p); outs['ccmp_s'].append(kc_n)
            outs['cslc_p'].append(ksl_p); outs['cslc_s'].append(ksl_n)
            outs['cwin_p'].append(win_p); outs['cwin_s'].append(win_n)
        xp = layer_norm(ALPHA * xp + mp, ln1_g[i], ln1_b[i])
        xs = layer_norm(ALPHA * xs + ms, ln1_g[i], ln1_b[i])
        xt = jnp.concatenate([xp.reshape(n_p, D_MODEL), xs.reshape(-1, D_MODEL)], axis=0)
        t = xt.shape[0]
        t_pad = -(-t // MOE_ROUTER_TILE) * MOE_ROUTER_TILE
        xt = jnp.pad(xt, ((0, t_pad - t), (0, 0)))
        yt = hier_moe(xt, moe_w_group[i], moe_w_expert[i], moe_w_gate[i], moe_w_up[i], moe_w_down[i])
        xt = layer_norm(ALPHA * xt + yt, ln2_g[i], ln2_b[i])
        xp = xt[:n_p].reshape(pshape)
        xs = xt[n_p:t].reshape(sshape)
    st = lambda k: jnp.stack(outs[k])
    return (xp, xs, st('a_p'), st('a_s'), st('bkv_p'), st('bkv_s'), st('bidx_p'), st('bidx_s'),
            st('ccmp_p'), st('ccmp_s'), st('cslc_p'), st('cslc_s'), st('cwin_p'), st('cwin_s'))
```

```python
import functools
import math

import jax
import jax.numpy as jnp
import numpy as np
from jax import lax
from jax.experimental import pallas as pl
from jax.experimental.pallas import tpu as pltpu

D_MODEL = 1024
DEPTH = 2
PAGE_SIZE = 128

S5_WIDTH = D_MODEL // 2
S5_GROUP = 16
S5_GROUPS = S5_WIDTH // S5_GROUP
S5_STATE = 64
B_HEADS = 8
B_DH = 64
B_KV_HEADS = 2
B_ROT = B_DH // 4
IDX_HEADS = 4
IDX_DH = 64
IDX_ROT = IDX_DH // 4
DSA_TOPK = 256
C_HEADS = 8
C_DH = 128
C_KV = 2
C_REP = C_HEADS // C_KV
C_ROT = C_DH // 4
CMP_BLOCK = 64
SEL_BLOCK = 64
N_SEL = 16
WINDOW = 512
N_EGROUPS = 4
E_PER_GROUP = 8
N_EXPERTS = N_EGROUPS * E_PER_GROUP
TOP_E = 2
D_FF_E = 512

ROPE_THETA = 500000.0
ALPHA = (2 * DEPTH) ** 0.25
Q_BLOCK = 128
LN_EPS = 1e-5
NEG = -1e30
FORCE = 1e4

AB_SIZES = (S5_WIDTH, B_HEADS * B_DH, B_KV_HEADS * 2 * B_DH, IDX_HEADS * IDX_DH, IDX_DH, IDX_HEADS)
C_SIZES = (C_HEADS * C_DH, C_KV * 2 * C_DH, C_KV * 2 * C_DH, C_KV * 2 * C_DH, C_HEADS * 3)

V7X_LANES = 128
V7X_SUBLANES = 8
V7X_VMEM_BYTES = 64 * 1024 * 1024
VMEM_LIMIT_BYTES = 48 * 1024 * 1024

MOE_ROUTER_TILE = 512
MOE_EXPERT_TILE = 256


def _moe_router_kernel(x_ref, w_ref, o_ref):
    logits = jnp.dot(x_ref[...], w_ref[...], precision=lax.Precision.HIGHEST,
                     preferred_element_type=jnp.float32)
    col = lax.broadcasted_iota(jnp.int32, logits.shape, 1)
    big = jnp.int32(1 << 20)
    ninf = jnp.float32(-jnp.inf)
    gl = jnp.where(col < N_EGROUPS, logits, ninf)
    gmax = jnp.max(gl, axis=-1, keepdims=True)
    g_top = jnp.min(jnp.where(gl == gmax, col, big), axis=-1, keepdims=True)
    g_w = 1.0 / jnp.sum(jnp.exp(gl - gmax), axis=-1, keepdims=True)
    lo = N_EGROUPS + E_PER_GROUP * g_top
    el = jnp.where((col >= lo) & (col < lo + E_PER_GROUP), logits, ninf)
    v1 = jnp.max(el, axis=-1, keepdims=True)
    i1 = jnp.min(jnp.where(el == v1, col, big), axis=-1, keepdims=True)
    el2 = jnp.where(col == i1, ninf, el)
    v2 = jnp.max(el2, axis=-1, keepdims=True)
    i2 = jnp.min(jnp.where(el2 == v2, col, big), axis=-1, keepdims=True)
    e2 = jnp.exp(v2 - v1)
    den = 1.0 + e2
    w1 = g_w / den
    w2 = g_w * e2 / den
    out = jnp.where(col == 0, (i1 - N_EGROUPS).astype(jnp.float32),
                    jnp.where(col == 1, (i2 - N_EGROUPS).astype(jnp.float32),
                              jnp.where(col == 2, w1, jnp.where(col == 3, w2, 0.0))))
    o_ref[...] = out


def _moe_route(x, w_router):
    t = x.shape[0]
    tm = min(MOE_ROUTER_TILE, t)
    assert t % tm == 0
    return pl.pallas_call(
        _moe_router_kernel,
        grid=(t // tm,),
        in_specs=[pl.BlockSpec((tm, D_MODEL), lambda i: (i, 0)),
                  pl.BlockSpec((D_MODEL, V7X_LANES), lambda i: (0, 0))],
        out_specs=pl.BlockSpec((tm, V7X_LANES), lambda i: (i, 0)),
        out_shape=jax.ShapeDtypeStruct((t, V7X_LANES), jnp.float32),
        name="moe_router",
    )(x, w_router)


def _moe_expert_kernel(tile_e_ref, nvalid_ref, xs_ref, wg_ref, wu_ref, wd_ref, sw_ref, o_ref,
                       wg_s, wu_s, wd_s):
    i = pl.program_id(0)
    valid = i < nvalid_ref[0]
    prev_e = tile_e_ref[jnp.maximum(i - 1, 0)]
    new_expert = jnp.logical_or(i == 0, tile_e_ref[i] != prev_e)

    @pl.when(jnp.logical_and(valid, new_expert))
    def _():
        wg_s[...] = wg_ref[0].astype(jnp.bfloat16)
        wu_s[...] = wu_ref[0].astype(jnp.bfloat16)
        wd_s[...] = wd_ref[0].astype(jnp.bfloat16)

    @pl.when(valid)
    def _():
        x = xs_ref[...]
        g = jnp.dot(x, wg_s[...], preferred_element_type=jnp.float32)
        u = jnp.dot(x, wu_s[...], preferred_element_type=jnp.float32)
        h = (g * jax.nn.sigmoid(g)) * u
        y = jnp.dot(h.astype(jnp.bfloat16), wd_s[...], preferred_element_type=jnp.float32)
        o_ref[...] = y * sw_ref[...]

    @pl.when(jnp.logical_not(valid))
    def _():
        o_ref[...] = jnp.zeros_like(o_ref)


def _moe_experts(tile_e, nvalid, xs, slot_w, w_gate, w_up, w_down):
    np_rows = xs.shape[0]
    tm = MOE_EXPERT_TILE
    n_tiles = np_rows // tm
    grid_spec = pltpu.PrefetchScalarGridSpec(
        num_scalar_prefetch=2,
        grid=(n_tiles,),
        in_specs=[
            pl.BlockSpec((tm, D_MODEL), lambda i, te, nv: (i, 0)),
            pl.BlockSpec((1, D_MODEL, D_FF_E), lambda i, te, nv: (te[i], 0, 0)),
            pl.BlockSpec((1, D_MODEL, D_FF_E), lambda i, te, nv: (te[i], 0, 0)),
            pl.BlockSpec((1, D_FF_E, D_MODEL), lambda i, te, nv: (te[i], 0, 0)),
            pl.BlockSpec((tm, 1), lambda i, te, nv: (i, 0)),
        ],
        out_specs=pl.BlockSpec((tm, D_MODEL), lambda i, te, nv: (i, 0)),
        scratch_shapes=[pltpu.VMEM((D_MODEL, D_FF_E), jnp.bfloat16),
                        pltpu.VMEM((D_MODEL, D_FF_E), jnp.bfloat16),
                        pltpu.VMEM((D_FF_E, D_MODEL), jnp.bfloat16)],
    )
    return pl.pallas_call(
        _moe_expert_kernel,
        grid_spec=grid_spec,
        out_shape=jax.ShapeDtypeStruct((np_rows, D_MODEL), jnp.float32),
        compiler_params=pltpu.CompilerParams(dimension_semantics=("arbitrary",),
                                             vmem_limit_bytes=VMEM_LIMIT_BYTES),
        name="moe_experts",
    )(tile_e, nvalid, xs, w_gate, w_up, w_down, slot_w)


def _moe_padded_rows(t):
    tm = MOE_EXPERT_TILE
    slots = t * TOP_E
    return ((slots + N_EXPERTS * (tm - 1)) // tm + 1) * tm


def hier_moe(xs_list, w_group, w_expert, w_gate, w_up, w_down):
    tm = MOE_EXPERT_TILE
    w_router = jnp.zeros((D_MODEL, V7X_LANES), jnp.float32)
    w_router = w_router.at[:, :N_EGROUPS].set(w_group).at[:, N_EGROUPS:N_EGROUPS + N_EXPERTS].set(w_expert)
    routed = jnp.concatenate([_moe_route(x, w_router) for x in xs_list], axis=0)
    t = routed.shape[0]
    eid = routed[:, 0:2].astype(jnp.int32).reshape(-1)
    ew = routed[:, 2:4].reshape(-1)
    n_slots = t * TOP_E
    np_rows = _moe_padded_rows(t)

    order = jnp.argsort(eid, stable=True).astype(jnp.int32)
    e_sorted = eid[order]
    counts = jnp.zeros((N_EXPERTS,), jnp.int32).at[eid].add(1)
    padded = ((counts + tm - 1) // tm) * tm
    pad_end = jnp.cumsum(padded)
    pad_off = pad_end - padded
    off = jnp.cumsum(counts) - counts
    dest = pad_off[e_sorted] + (jnp.arange(n_slots, dtype=jnp.int32) - off[e_sorted])
    src_tok = jnp.zeros((np_rows,), jnp.int32).at[dest].set(order // TOP_E)
    slot_w = jnp.zeros((np_rows,), jnp.float32).at[dest].set(ew[order])
    pos = jnp.zeros((n_slots,), jnp.int32).at[order].set(dest)
    tile_start = jnp.arange(np_rows // tm, dtype=jnp.int32) * tm
    tile_e = jnp.minimum(jnp.searchsorted(pad_end, tile_start, side='right'), N_EXPERTS - 1).astype(jnp.int32)
    nvalid = (pad_end[-1] // tm).astype(jnp.int32).reshape(1)

    x_all = jnp.concatenate([x.astype(jnp.bfloat16) for x in xs_list], axis=0)
    ys = _moe_experts(tile_e, nvalid, x_all[src_tok], slot_w[:, None], w_gate, w_up, w_down)
    pos2 = pos.reshape(t, TOP_E)
    y = ys[pos2[:, 0]] + ys[pos2[:, 1]]
    out, start = [], 0
    for x in xs_list:
        out.append(y[start:start + x.shape[0]])
        start += x.shape[0]
    return out


INT_MIN = -(2 ** 31)
INT_MAX = 2 ** 31 - 1


def _f32_order_key(x):
    b = lax.bitcast_convert_type(x + 0.0, jnp.int32)
    return jnp.where(b >= 0, b, b ^ jnp.int32(INT_MAX))


def _count_rows(key_ref, n_chunks, pred_fns):
    _, rows, w = key_ref.shape

    def body(c, accs):
        blk = key_ref[c]
        idx = c * w + lax.broadcasted_iota(jnp.int32, (rows, w), 1)
        out = []
        for fn, acc in zip(pred_fns, accs):
            m = jnp.where(fn(blk, idx), 1, 0)
            part = m[:, 0:V7X_LANES]
            for s in range(1, w // V7X_LANES):
                part = part + m[:, s * V7X_LANES:(s + 1) * V7X_LANES]
            out.append(acc + part)
        return tuple(out)

    init = tuple(jnp.zeros((rows, V7X_LANES), jnp.int32) for _ in pred_fns)
    accs = lax.fori_loop(0, n_chunks, body, init)
    return [jnp.sum(a, axis=-1, keepdims=True) for a in accs]


def _any_row(mask):
    return jnp.max(jnp.where(mask, 1.0, 0.0)) > 0.5


def _topk_threshold(key_ref, n_chunks, k, index_bits):
    _, rows, w = key_ref.shape
    col = lambda v: jnp.full((rows, 1), v, jnp.int32)

    def v_body(st):
        lo, hi, exact, _ = st
        mid = (lo & hi) + ((lo ^ hi) >> 1)
        cnt, = _count_rows(key_ref, n_chunks, [lambda blk, idx: blk > mid])
        active = lo < hi
        hit = active & (cnt == k)
        less = active & (cnt < k)
        more = active & (cnt > k)
        hi = jnp.where(hit | less, mid, hi)
        lo = jnp.where(hit, mid, jnp.where(more, mid + 1, lo))
        exact = jnp.where(hit, 1, exact)
        return lo, hi, exact, _any_row(lo < hi)

    thr, _, exact, _ = lax.while_loop(lambda st: st[3], v_body,
                                      (col(INT_MIN), col(INT_MAX), col(0), jnp.bool_(True)))
    c_gt, c_eq = _count_rows(key_ref, n_chunks, [lambda blk, idx: blk > thr, lambda blk, idx: blk == thr])
    r = k - c_gt
    take_all_ties = (exact == 0) & (thr != INT_MIN)
    need = take_all_ties & (c_eq > r)
    cut_default = jnp.where(take_all_ties, INT_MAX, -1)

    def tie_phase():
        def body(_, st):
            lo_i, hi_i = st
            mid = (lo_i + hi_i) >> 1
            g, = _count_rows(key_ref, n_chunks, [lambda blk, idx: (blk == thr) & (idx <= mid)])
            ok = g >= r
            return jnp.where(ok, lo_i, mid + 1), jnp.where(ok, mid, hi_i)
        lo_i, _ = lax.fori_loop(0, index_bits, body, (col(0), col(0) + (n_chunks * w - 1)))
        return jnp.where(need, lo_i, cut_default)

    cut = lax.cond(_any_row(need), tie_phase, lambda: cut_default)
    return thr, cut


def _selected(key, idx, thr, cut):
    return (key > thr) | ((key == thr) & (idx <= cut))


DSA_CHUNK = 1024


def _split3_lhs(x):
    hi = x.astype(jnp.bfloat16)
    lo = (x - hi.astype(jnp.float32)).astype(jnp.bfloat16)
    return jnp.concatenate([hi, hi, lo], axis=-1)


def _split3_rhs(x):
    hi = x.astype(jnp.bfloat16)
    lo = (x - hi.astype(jnp.float32)).astype(jnp.bfloat16)
    return jnp.concatenate([hi, lo, hi], axis=-1)


_NT = (((1,), (1,)), ((), ()))


def _dsa_prompt_kernel(qi_ref, wi_ref, kcat_ref, q_ref, k_ref, v_ref, o_ref, key_s, *, topk):
    cw = DSA_CHUNK
    rep = B_HEADS // B_KV_HEADS
    j = pl.program_id(1)
    t0 = j * Q_BLOCK
    n_chunks = (t0 + Q_BLOCK + cw - 1) // cw
    qpos = t0 + lax.broadcasted_iota(jnp.int32, (Q_BLOCK, 1), 0)

    qi = qi_ref[0]
    wi = wi_ref[0]
    qcat = jnp.concatenate([_split3_lhs(qi[:, h * IDX_DH:(h + 1) * IDX_DH]) for h in range(IDX_HEADS)], axis=0)

    def score_chunk(c, carry):
        start = pl.multiple_of(c * cw, cw)
        kc = kcat_ref[0, pl.ds(start, cw), :]
        s = lax.dot_general(qcat, kc, _NT, preferred_element_type=jnp.float32) * (IDX_DH ** -0.5)
        acc = None
        for h in range(IDX_HEADS):
            term = wi[:, h:h + 1] * jnp.maximum(s[h * Q_BLOCK:(h + 1) * Q_BLOCK], 0.0)
            acc = term if acc is None else acc + term
        kpos = start + lax.broadcasted_iota(jnp.int32, (Q_BLOCK, cw), 1)
        key_s[c] = jnp.where(kpos <= qpos, _f32_order_key(acc), INT_MIN)
        return carry

    lax.fori_loop(0, n_chunks, score_chunk, 0)
    thr, cut = _topk_threshold(key_s, n_chunks, topk, index_bits=int(math.log2(key_s.shape[0] * cw)))

    q = q_ref[0]
    qg = [jnp.concatenate([q[:, (g * rep + r) * B_DH:(g * rep + r + 1) * B_DH] for r in range(rep)],
                          axis=0).astype(jnp.bfloat16) for g in range(B_KV_HEADS)]

    def att_chunk(c, carry):
        start = pl.multiple_of(c * cw, cw)
        kidx = start + lax.broadcasted_iota(jnp.int32, (Q_BLOCK, cw), 1)
        sel = _selected(key_s[c], kidx, thr, cut)[None]
        out = []
        for g in range(B_KV_HEADS):
            m, l, acc = carry[g]
            kg = k_ref[0, g, pl.ds(start, cw), :]
            vg = v_ref[0, g, pl.ds(start, cw), :]
            s = lax.dot_general(qg[g], kg, _NT, preferred_element_type=jnp.float32) * (B_DH ** -0.5)
            s = jnp.where(sel, s.reshape(rep, Q_BLOCK, cw), NEG)
            m_new = jnp.maximum(m, jnp.max(s, axis=-1, keepdims=True))
            p = jnp.where(sel, jnp.exp(s - m_new), 0.0)
            alpha = jnp.exp(m - m_new)
            l = alpha * l + jnp.sum(p, axis=-1, keepdims=True)
            pv = jnp.dot(p.reshape(rep * Q_BLOCK, cw).astype(jnp.bfloat16), vg,
                         preferred_element_type=jnp.float32)
            acc = alpha * acc + pv.reshape(rep, Q_BLOCK, B_DH)
            out.append((m_new, l, acc))
        return tuple(out)

    init = tuple((jnp.full((rep, Q_BLOCK, 1), NEG, jnp.float32),
                  jnp.zeros((rep, Q_BLOCK, 1), jnp.float32),
                  jnp.zeros((rep, Q_BLOCK, B_DH), jnp.float32)) for _ in range(B_KV_HEADS))
    res = lax.fori_loop(0, n_chunks, att_chunk, init)
    heads = []
    for g in range(B_KV_HEADS):
        _, l, acc = res[g]
        o = jnp.where(l > 0.0, acc / jnp.where(l > 0.0, l, 1.0), 0.0)
        heads += [o[r] for r in range(rep)]
    o_ref[0] = jnp.concatenate(heads, axis=-1)


def dsa_prompt(q, kv, qi, ki, wi):
    bn, s, _ = q.shape
    topk = min(DSA_TOPK, s // 4)
    cw = DSA_CHUNK
    assert s % cw == 0 and s % Q_BLOCK == 0
    kcat = _split3_rhs(ki)
    k = jnp.transpose(kv[:, :, :, 0, :], (0, 2, 1, 3)).astype(jnp.bfloat16)
    v = jnp.transpose(kv[:, :, :, 1, :], (0, 2, 1, 3)).astype(jnp.bfloat16)
    qw = B_HEADS * B_DH
    return pl.pallas_call(
        functools.partial(_dsa_prompt_kernel, topk=topk),
        grid=(bn, s // Q_BLOCK),
        in_specs=[
            pl.BlockSpec((1, Q_BLOCK, IDX_HEADS * IDX_DH), lambda b, j: (b, j, 0)),
            pl.BlockSpec((1, Q_BLOCK, IDX_HEADS), lambda b, j: (b, j, 0)),
            pl.BlockSpec((1, s, 3 * IDX_DH), lambda b, j: (b, 0, 0)),
            pl.BlockSpec((1, Q_BLOCK, qw), lambda b, j: (b, j, 0)),
            pl.BlockSpec((1, B_KV_HEADS, s, B_DH), lambda b, j: (b, 0, 0, 0)),
            pl.BlockSpec((1, B_KV_HEADS, s, B_DH), lambda b, j: (b, 0, 0, 0)),
        ],
        out_specs=pl.BlockSpec((1, Q_BLOCK, qw), lambda b, j: (b, j, 0)),
        out_shape=jax.ShapeDtypeStruct((bn, s, qw), jnp.float32),
        scratch_shapes=[pltpu.VMEM((s // cw, Q_BLOCK, cw), jnp.int32)],
        compiler_params=pltpu.CompilerParams(dimension_semantics=("arbitrary", "arbitrary"),
                                             vmem_limit_bytes=VMEM_LIMIT_BYTES),
        name="dsa_prompt",
    )(qi, wi, kcat, q, k, v)


NSA_CHUNK = 1024
NSA_WIN_BLOCKS = WINDOW // Q_BLOCK + 1


def _nsa_prompt_kernel(qraw_ref, qrot_ref, gates_ref, kck_ref, kcv_ref, e_ref, ks_ref, vs_ref, *rest, n_sel):
    kw_refs = rest[0:NSA_WIN_BLOCKS]
    vw_refs = rest[NSA_WIN_BLOCKS:2 * NSA_WIN_BLOCKS]
    o_ref, key_s = rest[2 * NSA_WIN_BLOCKS:]
    cw = NSA_CHUNK
    rep = C_REP
    scale = C_DH ** -0.5
    j = pl.program_id(1)
    t0 = j * Q_BLOCK
    n_chunks = (t0 + Q_BLOCK + cw - 1) // cw
    qpos = t0 + lax.broadcasted_iota(jnp.int32, (Q_BLOCK, 1), 0)
    nbp = kck_ref.shape[2]
    blk = lax.broadcasted_iota(jnp.int32, (Q_BLOCK, nbp), 1)
    cmask = ((blk + 1) * CMP_BLOCK - 1) <= qpos
    cur = qpos // SEL_BLOCK
    forced = (blk == 0) | (blk == cur) | (blk == cur - 1)
    valid = blk * SEL_BLOCK <= qpos
    wlen = NSA_WIN_BLOCKS * Q_BLOCK
    wpos = t0 - WINDOW + lax.broadcasted_iota(jnp.int32, (Q_BLOCK, wlen), 1)
    wmask = ((wpos >= 0) & (wpos <= qpos) & (wpos > qpos - WINDOW))[None]

    qraw = qraw_ref[0]
    qrot = qrot_ref[0]
    sig = jax.nn.sigmoid(gates_ref[0])

    def stack(x, g):
        return jnp.concatenate([x[:, (g * rep + r) * C_DH:(g * rep + r + 1) * C_DH] for r in range(rep)], axis=0)

    for g in range(C_KV):
        s = lax.dot_general(stack(qraw, g), kck_ref[0, g], _NT, precision=lax.Precision.HIGHEST,
                            preferred_element_type=jnp.float32) * scale
        s = jnp.where(cmask[None], s.reshape(rep, Q_BLOCK, nbp), NEG)
        e = jnp.exp(s - jnp.max(s, axis=-1, keepdims=True))
        p = e / jnp.sum(e, axis=-1, keepdims=True)
        p = p * jnp.where(jnp.max(jnp.where(cmask, 1.0, 0.0), axis=-1, keepdims=True) > 0.5, 1.0, 0.0)[None]
        o_cmp = jnp.dot(p.reshape(rep * Q_BLOCK, nbp), kcv_ref[0, g], precision=lax.Precision.HIGHEST,
                        preferred_element_type=jnp.float32).reshape(rep, Q_BLOCK, C_DH)
        imp = p[0]
        for r in range(1, rep):
            imp = imp + p[r]
        score = jnp.where(forced, FORCE, imp)
        key_s[0] = jnp.where(valid, _f32_order_key(score), INT_MIN)
        thr, cut = _topk_threshold(key_s, 1, n_sel, index_bits=int(math.log2(nbp)))
        selblk = jnp.where(_selected(key_s[0], blk, thr, cut), 1.0, 0.0).astype(jnp.bfloat16)

        qg = stack(qrot, g).astype(jnp.bfloat16)

        def att_chunk(c, carry):
            m, l, acc = carry
            start = pl.multiple_of(c * cw, cw)
            kpos = start + lax.broadcasted_iota(jnp.int32, (Q_BLOCK, cw), 1)
            hit = jnp.dot(selblk, e_ref[c], preferred_element_type=jnp.float32)
            sel = ((hit > 0.5) & (kpos <= qpos))[None]
            kg = ks_ref[0, g, pl.ds(start, cw), :]
            vg = vs_ref[0, g, pl.ds(start, cw), :]
            sc = lax.dot_general(qg, kg, _NT, preferred_element_type=jnp.float32) * scale
            sc = jnp.where(sel, sc.reshape(rep, Q_BLOCK, cw), NEG)
            m_new = jnp.maximum(m, jnp.max(sc, axis=-1, keepdims=True))
            pp = jnp.where(sel, jnp.exp(sc - m_new), 0.0)
            alpha = jnp.exp(m - m_new)
            l = alpha * l + jnp.sum(pp, axis=-1, keepdims=True)
            pv = jnp.dot(pp.reshape(rep * Q_BLOCK, cw).astype(jnp.bfloat16), vg, preferred_element_type=jnp.float32)
            return m_new, l, alpha * acc + pv.reshape(rep, Q_BLOCK, C_DH)

        init = (jnp.full((rep, Q_BLOCK, 1), NEG, jnp.float32), jnp.zeros((rep, Q_BLOCK, 1), jnp.float32),
                jnp.zeros((rep, Q_BLOCK, C_DH), jnp.float32))
        _, l, acc = lax.fori_loop(0, n_chunks, att_chunk, init)
        o_slc = jnp.where(l > 0.0, acc / jnp.where(l > 0.0, l, 1.0), 0.0)

        kwin = jnp.concatenate([r_[0, g] for r_ in kw_refs], axis=0)
        vwin = jnp.concatenate([r_[0, g] for r_ in vw_refs], axis=0)
        sw = lax.dot_general(qg, kwin, _NT, preferred_element_type=jnp.float32) * scale
        sw = jnp.where(wmask, sw.reshape(rep, Q_BLOCK, wlen), NEG)
        ew = jnp.where(wmask, jnp.exp(sw - jnp.max(sw, axis=-1, keepdims=True)), 0.0)
        lw = jnp.sum(ew, axis=-1, keepdims=True)
        pw = ew / jnp.where(lw > 0.0, lw, 1.0)
        o_win = jnp.dot(pw.reshape(rep * Q_BLOCK, wlen).astype(jnp.bfloat16), vwin,
                        preferred_element_type=jnp.float32).reshape(rep, Q_BLOCK, C_DH)

        for r in range(rep):
            hh = g * rep + r
            o = (sig[:, 3 * hh:3 * hh + 1] * o_cmp[r] + sig[:, 3 * hh + 1:3 * hh + 2] * o_slc[r]
                 + sig[:, 3 * hh + 2:3 * hh + 3] * o_win[r])
            o_ref[0, :, hh * C_DH:(hh + 1) * C_DH] = o


def nsa_prompt(q_raw, q_rot, gates, kc, kv_s, kv_w):
    bn, s, _ = q_raw.shape
    cw = NSA_CHUNK
    assert s % cw == 0
    nb = kc.shape[1]
    nbs = -(-s // SEL_BLOCK)
    assert nb == nbs
    nbp = -(-nb // V7X_LANES) * V7X_LANES
    n_sel = min(N_SEL, nbs)
    kcp = jnp.pad(kc, ((0, 0), (0, nbp - nb), (0, 0), (0, 0), (0, 0)))
    kck = jnp.transpose(kcp[:, :, :, 0, :], (0, 2, 1, 3))
    kcv = jnp.transpose(kcp[:, :, :, 1, :], (0, 2, 1, 3))
    split = lambda kv, c: jnp.transpose(kv[:, :, :, c, :], (0, 2, 1, 3)).astype(jnp.bfloat16)
    ks, vs, kw, vw = split(kv_s, 0), split(kv_s, 1), split(kv_w, 0), split(kv_w, 1)
    expand = (jnp.arange(s, dtype=jnp.int32)[None, :] // SEL_BLOCK == jnp.arange(nbp, dtype=jnp.int32)[:, None])
    expand = jnp.transpose(expand.astype(jnp.bfloat16).reshape(nbp, s // cw, cw), (1, 0, 2))
    qd = C_HEADS * C_DH
    nq = s // Q_BLOCK
    wb = NSA_WIN_BLOCKS

    def win_spec(slot):
        return pl.BlockSpec((1, C_KV, Q_BLOCK, C_DH),
                            lambda b, j: (b, 0, jnp.maximum(j - (wb - 1) + slot, 0), 0))

    return pl.pallas_call(
        functools.partial(_nsa_prompt_kernel, n_sel=n_sel),
        grid=(bn, nq),
        in_specs=[
            pl.BlockSpec((1, Q_BLOCK, qd), lambda b, j: (b, j, 0)),
            pl.BlockSpec((1, Q_BLOCK, qd), lambda b, j: (b, j, 0)),
            pl.BlockSpec((1, Q_BLOCK, C_HEADS * 3), lambda b, j: (b, j, 0)),
            pl.BlockSpec((1, C_KV, nbp, C_DH), lambda b, j: (b, 0, 0, 0)),
            pl.BlockSpec((1, C_KV, nbp, C_DH), lambda b, j: (b, 0, 0, 0)),
            pl.BlockSpec((s // cw, nbp, cw), lambda b, j: (0, 0, 0)),
            pl.BlockSpec((1, C_KV, s, C_DH), lambda b, j: (b, 0, 0, 0)),
            pl.BlockSpec((1, C_KV, s, C_DH), lambda b, j: (b, 0, 0, 0)),
        ] + [win_spec(i) for i in range(wb)] + [win_spec(i) for i in range(wb)],
        out_specs=pl.BlockSpec((1, Q_BLOCK, qd), lambda b, j: (b, j, 0)),
        out_shape=jax.ShapeDtypeStruct((bn, s, qd), jnp.float32),
        scratch_shapes=[pltpu.VMEM((1, Q_BLOCK, nbp), jnp.int32)],
        compiler_params=pltpu.CompilerParams(dimension_semantics=("arbitrary", "arbitrary"),
                                             vmem_limit_bytes=VMEM_LIMIT_BYTES),
        name="nsa_prompt",
    )(q_raw, q_rot, gates, kck, kcv, expand, ks, vs, *([kw] * wb), *([vw] * wb))


S5_N = S5_GROUPS * S5_STATE
S5_TILES = 4
S5_TILE_IN = S5_WIDTH // S5_TILES
S5_TILE_N = S5_N // S5_TILES
S5_CHUNK = 256
S5_SUB = S5_CHUNK // V7X_SUBLANES


def _s5_input_map(u, bcat_ref, store):
    for i in range(S5_TILES):
        ucat = _split3_lhs(u[:, i * S5_TILE_IN:(i + 1) * S5_TILE_IN])
        r = jnp.dot(ucat, bcat_ref[i], preferred_element_type=jnp.float32)
        store(i, r[:, :S5_TILE_N], r[:, S5_TILE_N:])


def _s5_output_map(h_tile, u, ccat_ref, d_ref, wglu_ref):
    ys = []
    for i in range(S5_TILES):
        re, im = h_tile(i)
        hcat = jnp.concatenate([re, im], axis=-1).astype(jnp.bfloat16)
        ys.append(jnp.dot(hcat, ccat_ref[i], preferred_element_type=jnp.float32))
    y = jax.nn.gelu(jnp.concatenate(ys, axis=-1) + d_ref[...] * u)
    z = jnp.dot(y.astype(jnp.bfloat16), wglu_ref[...], preferred_element_type=jnp.float32)
    return y * jax.nn.sigmoid(z)


def _cmul_add(a_re, a_im, x_re, x_im, b_re, b_im):
    return a_re * x_re - a_im * x_im + b_re, a_re * x_im + a_im * x_re + b_im


def _s5_prompt_kernel(u_ref, h0_ref, bcat_ref, lam8_ref, lamm_ref, ccat_ref, d_ref, wglu_ref,
                      y_ref, hlast_ref, hs, ein_s, carry_s):
    n = S5_N
    m = S5_SUB
    c = pl.program_id(1)

    @pl.when(c == 0)
    def _():
        carry_s[...] = h0_ref[0]

    u = u_ref[0]

    def store_bu(i, re, im):
        hs[:, i * S5_TILE_N:(i + 1) * S5_TILE_N] = re
        hs[:, n + i * S5_TILE_N:n + (i + 1) * S5_TILE_N] = im

    _s5_input_map(u, bcat_ref, store_bu)

    def scan_body(k, h):
        row = pl.multiple_of(k * V7X_SUBLANES, V7X_SUBLANES)
        rows = pl.ds(row, V7X_SUBLANES)
        n_re, n_im = _cmul_add(lam8_ref[:, :n], lam8_ref[:, n:], h[0], h[1], hs[rows, :n], hs[rows, n:])
        hs[rows, :n] = n_re
        hs[rows, n:] = n_im
        return n_re, n_im

    zero = jnp.zeros((V7X_SUBLANES, n), jnp.float32)
    ends = lax.fori_loop(0, m, scan_body, (zero, zero))

    lm_re, lm_im = lamm_ref[:, :n], lamm_ref[:, n:]
    e_re, e_im = carry_s[:, :n], carry_s[:, n:]
    for s in range(V7X_SUBLANES):
        ein_s[s:s + 1, :n] = e_re
        ein_s[s:s + 1, n:] = e_im
        e_re, e_im = _cmul_add(lm_re, lm_im, e_re, e_im, ends[0][s:s + 1], ends[1][s:s + 1])
    carry_s[:, :n] = e_re
    carry_s[:, n:] = e_im
    hlast_ref[0] = carry_s[...]

    def fix_body(k, corr):
        row = pl.multiple_of(k * V7X_SUBLANES, V7X_SUBLANES)
        rows = pl.ds(row, V7X_SUBLANES)
        c_re, c_im = _cmul_add(lam8_ref[:, :n], lam8_ref[:, n:], corr[0], corr[1], 0.0, 0.0)
        hs[rows, :n] = hs[rows, :n] + c_re
        hs[rows, n:] = hs[rows, n:] + c_im
        return c_re, c_im

    lax.fori_loop(0, m, fix_body, (ein_s[:, :n], ein_s[:, n:]))

    def h_tile(i):
        return hs[:, i * S5_TILE_N:(i + 1) * S5_TILE_N], hs[:, n + i * S5_TILE_N:n + (i + 1) * S5_TILE_N]

    y_ref[0] = _s5_output_map(h_tile, u, ccat_ref, d_ref, wglu_ref)


def _s5_sample_kernel(u_ref, h0_ref, bcat_ref, lam_ref, ccat_ref, d_ref, wglu_ref, y_ref, hlast_ref, hs):
    n = S5_N
    steps, bd, _ = u_ref.shape
    h_re, h_im = h0_ref[:, :n], h0_ref[:, n:]
    lam_re, lam_im = lam_ref[:, :n], lam_ref[:, n:]
    for t in range(steps):
        u = u_ref[t]

        def store_bu(i, re, im):
            hs[:, i * S5_TILE_N:(i + 1) * S5_TILE_N] = re
            hs[:, n + i * S5_TILE_N:n + (i + 1) * S5_TILE_N] = im

        _s5_input_map(u, bcat_ref, store_bu)
        h_re, h_im = _cmul_add(lam_re, lam_im, h_re, h_im, hs[:, :n], hs[:, n:])
        hs[:, :n] = h_re
        hs[:, n:] = h_im

        def h_tile(i):
            return hs[:, i * S5_TILE_N:(i + 1) * S5_TILE_N], hs[:, n + i * S5_TILE_N:n + (i + 1) * S5_TILE_N]

        y_ref[t] = _s5_output_map(h_tile, u, ccat_ref, d_ref, wglu_ref)
    hlast_ref[:, :n] = h_re
    hlast_ref[:, n:] = h_im


def _s5_params(a_re, a_im, log_dt, b_re, b_im, c_re, c_im):
    lam = lax.complex(a_re, a_im)
    dt = jnp.exp(log_dt)[:, None]
    lam_bar = jnp.exp(lam * dt)
    b_bar = ((lam_bar - 1.0) / lam)[:, :, None] * lax.complex(b_re, b_im)
    gpt = S5_GROUPS // S5_TILES
    eye = jnp.eye(gpt, dtype=jnp.float32)

    def in_blockdiag(w):
        w = w.reshape(S5_TILES, gpt, S5_STATE, S5_GROUP)
        return jnp.einsum('tgpc,gh->tgchp', w, eye).reshape(S5_TILES, S5_TILE_IN, S5_TILE_N)

    def out_blockdiag(w):
        w = w.reshape(S5_TILES, gpt, S5_GROUP, S5_STATE)
        return jnp.einsum('tgcp,gh->tgphc', w, eye).reshape(S5_TILES, S5_TILE_N, S5_TILE_IN)

    b_full = jnp.concatenate([in_blockdiag(b_bar.real), in_blockdiag(b_bar.imag)], axis=-1)
    b_hi = b_full.astype(jnp.bfloat16)
    b_lo = (b_full - b_hi.astype(jnp.float32)).astype(jnp.bfloat16)
    bcat = jnp.concatenate([b_hi, b_lo, b_hi], axis=1)
    ccat = jnp.concatenate([out_blockdiag(c_re), out_blockdiag(-c_im)], axis=1).astype(jnp.bfloat16)
    flat = lambda z: jnp.concatenate([z.real.reshape(1, S5_N), z.imag.reshape(1, S5_N)], axis=-1)
    lam_row = flat(lam_bar)
    lamm_row = flat(jnp.exp(lam * dt * S5_SUB))
    return bcat, ccat, lam_row, lamm_row


def s5_prompt(u, a_re, a_im, log_dt, b_re, b_im, c_re, c_im, d, w_glu):
    bn, s, _ = u.shape
    tc, m = S5_CHUNK, S5_SUB
    assert s % tc == 0
    nc = s // tc
    bcat, ccat, lam_row, lamm_row = _s5_params(a_re, a_im, log_dt, b_re, b_im, c_re, c_im)
    lam8 = jnp.broadcast_to(lam_row, (V7X_SUBLANES, 2 * S5_N))
    to_ks = lambda a: a.reshape(bn, nc, V7X_SUBLANES, m, S5_WIDTH).swapaxes(2, 3).reshape(bn, s, S5_WIDTH)
    from_ks = lambda a: a.reshape(bn, nc, m, V7X_SUBLANES, S5_WIDTH).swapaxes(2, 3).reshape(bn, s, S5_WIDTH)
    h0 = jnp.zeros((bn, 1, 2 * S5_N), jnp.float32)
    const = lambda shape: pl.BlockSpec(shape, lambda b, c: (0,) * len(shape))
    y, hlast = pl.pallas_call(
        _s5_prompt_kernel,
        grid=(bn, nc),
        in_specs=[
            pl.BlockSpec((1, tc, S5_WIDTH), lambda b, c: (b, c, 0)),
            pl.BlockSpec((1, 1, 2 * S5_N), lambda b, c: (b, 0, 0)),
            const(bcat.shape), const(lam8.shape), const(lamm_row.shape), const(ccat.shape),
            const((1, S5_WIDTH)), const((S5_WIDTH, S5_WIDTH)),
        ],
        out_specs=[pl.BlockSpec((1, tc, S5_WIDTH), lambda b, c: (b, c, 0)),
                   pl.BlockSpec((1, 1, 2 * S5_N), lambda b, c: (b, 0, 0))],
        out_shape=[jax.ShapeDtypeStruct((bn, s, S5_WIDTH), jnp.float32),
                   jax.ShapeDtypeStruct((bn, 1, 2 * S5_N), jnp.float32)],
        scratch_shapes=[pltpu.VMEM((tc, 2 * S5_N), jnp.float32),
                        pltpu.VMEM((V7X_SUBLANES, 2 * S5_N), jnp.float32),
                        pltpu.VMEM((1, 2 * S5_N), jnp.float32)],
        compiler_params=pltpu.CompilerParams(dimension_semantics=("arbitrary", "arbitrary"),
                                             vmem_limit_bytes=VMEM_LIMIT_BYTES),
        name="s5_prompt",
    )(to_ks(u), h0, bcat, lam8, lamm_row, ccat, d.reshape(1, S5_WIDTH), w_glu.astype(jnp.bfloat16))
    return from_ks(y), hlast.reshape(bn, 2, S5_GROUPS, S5_STATE)


def s5_sample(u, state, a_re, a_im, log_dt, b_re, b_im, c_re, c_im, d, w_glu):
    bd, steps, _ = u.shape
    bcat, ccat, lam_row, _ = _s5_params(a_re, a_im, log_dt, b_re, b_im, c_re, c_im)
    y, hlast = pl.pallas_call(
        _s5_sample_kernel,
        out_shape=[jax.ShapeDtypeStruct((steps, bd, S5_WIDTH), jnp.float32),
                   jax.ShapeDtypeStruct((bd, 2 * S5_N), jnp.float32)],
        scratch_shapes=[pltpu.VMEM((bd, 2 * S5_N), jnp.float32)],
        compiler_params=pltpu.CompilerParams(vmem_limit_bytes=VMEM_LIMIT_BYTES),
        name="s5_sample",
    )(jnp.swapaxes(u, 0, 1), state.reshape(bd, 2 * S5_N), bcat, lam_row, ccat,
      d.reshape(1, S5_WIDTH), w_glu.astype(jnp.bfloat16))
    return jnp.swapaxes(y, 0, 1), hlast.reshape(bd, 2, S5_GROUPS, S5_STATE)


def split_cols(h, sizes):
    out, start = [], 0
    for n in sizes:
        out.append(h[..., start:start + n])
        start += n
    return out


def layer_norm(x, g, b):
    mu = x.mean(-1, keepdims=True)
    var = jnp.square(x - mu).mean(-1, keepdims=True)
    return (x - mu) * lax.rsqrt(var + LN_EPS) * g + b


def rope_partial(x, pos, rot_dim):
    half = rot_dim // 2
    inv = ROPE_THETA ** (-jnp.arange(half, dtype=jnp.float32) / half)
    ang = pos.astype(jnp.float32)[:, None] * inv
    cos, sin = jnp.cos(ang)[:, None, :], jnp.sin(ang)[:, None, :]
    x1 = x[..., :half]
    x2 = x[..., half:rot_dim]
    rot = jnp.concatenate([x1 * cos - x2 * sin, x2 * cos + x1 * sin], -1)
    return jnp.concatenate([rot, x[..., rot_dim:]], -1)


def rope_kv(kv, pos, rot_dim):
    k = rope_partial(kv[..., 0, :], pos, rot_dim)
    return jnp.stack([k, kv[..., 1, :]], axis=-2)


def gather_past(pool, page_table):
    rows = pool[page_table]
    return rows.reshape((page_table.shape[0], -1) + pool.shape[2:])


def paged_rows(pool, page_table, new, pos, group=None):
    past_len = page_table.shape[1] * PAGE_SIZE
    b = jnp.arange(pos.shape[0]).reshape((-1,) + (1,) * (pos.ndim - 1))
    pp = jnp.clip(pos, 0, past_len - 1)
    pn = jnp.clip(pos - past_len, 0, new.shape[1] - 1)
    phys = page_table[b, pp // PAGE_SIZE]
    if group is None:
        old, fresh = pool[phys, pp % PAGE_SIZE], new[b, pn]
    else:
        old, fresh = pool[phys, pp % PAGE_SIZE, group], new[b, pn, group]
    keep = (pos < past_len).reshape(pos.shape + (1,) * (old.ndim - pos.ndim))
    return jnp.where(keep, old, fresh)


def softmax_attend(q, k, v, mask, kspec):
    s = jnp.einsum(f'bqgrd,{kspec}->bqgrk', q, k).astype(jnp.float32) * (q.shape[-1] ** -0.5)
    s = jnp.where(mask, s, NEG)
    p = jax.nn.softmax(s, axis=-1) * jnp.any(mask, axis=-1, keepdims=True)
    return jnp.einsum(f'bqgrk,{kspec}->bqgrd', p.astype(v.dtype), v), p


def window_mask(qpos, kpos):
    m = (kpos[None, :] >= 0) & (kpos[None, :] <= qpos[:, None]) & (kpos[None, :] > qpos[:, None] - WINDOW)
    return m[None, :, None, None, :]


def indexer_topk(qi, wi, ki, qpos, kpos, topk):
    s = jnp.einsum('bqhd,bkd->bqhk', qi, ki) * (IDX_DH ** -0.5)
    score = jnp.einsum('bqh,bqhk->bqk', wi, jax.nn.relu(s))
    score = jnp.where((kpos[None, :] <= qpos[:, None])[None], score, NEG)
    _, idx = lax.top_k(score, topk)
    return idx


def dsa_sample(q, kv, qi, ki, wi, pos, cache_kv, cache_idx, page_table):
    past_len = page_table.shape[1] * PAGE_SIZE
    L = past_len + q.shape[1]
    ki_all = jnp.concatenate([gather_past(cache_idx, page_table), ki], axis=1)
    idx = indexer_topk(qi, wi, ki_all, pos, jnp.arange(L), min(DSA_TOPK, L // 4))
    rows = paged_rows(cache_kv, page_table, kv, idx)
    mask = (idx <= pos[None, :, None])[:, :, None, None, :]
    o, _ = softmax_attend(q, rows[..., 0, :], rows[..., 1, :], mask, 'bqkgd')
    return o


def mixer_ab(x, pos, w_in, a_re, a_im, log_dt, b_re, b_im, c_re, c_im, d, w_glu, w_out,
             state=None, cache_kv=None, cache_idx=None, page_table=None):
    bn, s, _ = x.shape
    s5p = (a_re, a_im, log_dt, b_re, b_im, c_re, c_im, d, w_glu)
    u, q, kv, qi, ki, wi = split_cols(x @ w_in, AB_SIZES)
    q = rope_partial(q.reshape(bn, s, B_HEADS, B_DH), pos, B_ROT)
    kv = rope_kv(kv.reshape(bn, s, B_KV_HEADS, 2, B_DH), pos, B_ROT)
    qi = rope_partial(qi.reshape(bn, s, IDX_HEADS, IDX_DH), pos, IDX_ROT)
    ki = rope_partial(ki.reshape(bn, s, 1, IDX_DH), pos, IDX_ROT)[:, :, 0]
    if state is None:
        o_b = dsa_prompt(q.reshape(bn, s, B_HEADS * B_DH), kv, qi.reshape(bn, s, IDX_HEADS * IDX_DH), ki, wi)
        y_s5, new_a = s5_prompt(u, *s5p)
    else:
        q5 = q.reshape(bn, s, B_KV_HEADS, B_HEADS // B_KV_HEADS, B_DH)
        o_b = dsa_sample(q5, kv, qi, ki, wi, pos, cache_kv, cache_idx, page_table).reshape(bn, s, B_HEADS * B_DH)
        y_s5, new_a = s5_sample(u, state, *s5p)
    out = jnp.concatenate([y_s5, o_b], -1) @ w_out
    return out, new_a, kv, ki


def compress_blocks(kv, cmp_pos, cmp_w1, cmp_w2):
    bn, L = kv.shape[:2]
    nb = L // CMP_BLOCK
    blk = kv[:, :nb * CMP_BLOCK].reshape(bn, nb, CMP_BLOCK, C_KV, 2, C_DH) + cmp_pos[:, None]
    blk = blk.transpose(0, 1, 3, 4, 2, 5).reshape(bn, nb, C_KV, 2, CMP_BLOCK * C_DH)
    hid = jax.nn.gelu(jnp.einsum('bngcf,cfd->bngcd', blk, cmp_w1))
    return jnp.einsum('bngcd,cde->bngce', hid, cmp_w2)


def nsa_compressed(q, kv_all, pos, cmp_pos, cmp_w1, cmp_w2):
    kc = compress_blocks(kv_all, cmp_pos, cmp_w1, cmp_w2)
    ends = (jnp.arange(kc.shape[1]) + 1) * CMP_BLOCK - 1
    mask = (ends[None, :] <= pos[:, None])[None, :, None, None, :]
    o, p = softmax_attend(q, kc[..., 0, :], kc[..., 1, :], mask, 'bkgd')
    return o, p.sum(axis=3)


def selected_positions(imp, qpos, L):
    nbs = -(-L // SEL_BLOCK)
    imp = jnp.pad(imp, ((0, 0), (0, 0), (0, 0), (0, nbs - imp.shape[-1])))
    j = jnp.arange(nbs)
    cur = (qpos // SEL_BLOCK)[:, None]
    forced = ((j == 0) | (j == cur) | (j == cur - 1))[None, :, None, :]
    valid = (j * SEL_BLOCK <= qpos[:, None])[None, :, None, :]
    score = jnp.where(valid, jnp.where(forced, FORCE, imp), NEG)
    _, blk = lax.top_k(score, min(N_SEL, nbs))
    return (blk[..., None] * SEL_BLOCK + jnp.arange(SEL_BLOCK)).reshape(blk.shape[:-1] + (-1,))


def mixer_c(x, pos, w_in, cmp_pos, cmp_w1, cmp_w2, w_out,
            cache_cmp=None, cache_slc=None, win=None, page_table=None):
    bn, s, _ = x.shape
    q, kv_c, kv_s, kv_w, gates = split_cols(x @ w_in, C_SIZES)
    q = q.reshape(bn, s, C_HEADS, C_DH)
    q_raw = q.reshape(bn, s, C_KV, C_REP, C_DH)
    q_rot = rope_partial(q, pos, C_ROT).reshape(bn, s, C_KV, C_REP, C_DH)
    kv_c = kv_c.reshape(bn, s, C_KV, 2, C_DH)
    kv_s = rope_kv(kv_s.reshape(bn, s, C_KV, 2, C_DH), pos, C_ROT)
    kv_w = rope_kv(kv_w.reshape(bn, s, C_KV, 2, C_DH), pos, C_ROT)
    gidx = jnp.arange(C_KV).reshape(1, 1, C_KV, 1)
    if cache_cmp is None:
        kc = compress_blocks(kv_c, cmp_pos, cmp_w1, cmp_w2)
        o = nsa_prompt(q.reshape(bn, s, C_HEADS * C_DH), q_rot.reshape(bn, s, C_HEADS * C_DH), gates, kc, kv_s, kv_w)
        return o @ w_out, kv_c, kv_s, kv_w[:, -min(WINDOW, s):]
    past_len = page_table.shape[1] * PAGE_SIZE
    L = past_len + s
    kv_c_all = jnp.concatenate([gather_past(cache_cmp, page_table), kv_c], axis=1)
    o_cmp, imp = nsa_compressed(q_raw, kv_c_all, pos, cmp_pos, cmp_w1, cmp_w2)
    kpos = selected_positions(imp, pos, L)
    rows = paged_rows(cache_slc, page_table, kv_s, kpos, gidx)
    smask = (kpos <= pos[None, :, None, None])[:, :, :, None, :]
    o_slc, _ = softmax_attend(q_rot, rows[..., 0, :], rows[..., 1, :], smask, 'bqgkd')
    kw = jnp.concatenate([win, kv_w], axis=1)
    wpos = past_len - win.shape[1] + jnp.arange(kw.shape[1])
    o_win, _ = softmax_attend(q_rot, kw[..., 0, :], kw[..., 1, :], window_mask(pos, wpos), 'bkgd')
    new_win = kw[:, -win.shape[1]:]
    g = jax.nn.sigmoid(gates.reshape(bn, s, C_KV, C_REP, 3))
    o = g[..., 0:1] * o_cmp + g[..., 1:2] * o_slc + g[..., 2:3] * o_win
    return o.reshape(bn, s, C_HEADS * C_DH) @ w_out, kv_c, kv_s, new_win


def kernel(x_prompt, x_sample, state_a, cache_b_kv, cache_b_idx, cache_c_cmp, cache_c_slc, state_c_win, page_table,
           w_in_ab, s5_a_re, s5_a_im, s5_log_dt, s5_b_re, s5_b_im, s5_c_re, s5_c_im, s5_d, s5_w_glu, w_out_ab,
           w_in_c, cmp_pos, cmp_w1, cmp_w2, w_out_c,
           ln1_g, ln1_b, ln2_g, ln2_b, moe_w_group, moe_w_expert, moe_w_gate, moe_w_up, moe_w_down):
    past_len = page_table.shape[1] * PAGE_SIZE
    pos_p = jnp.arange(x_prompt.shape[1])
    pos_s = past_len + jnp.arange(x_sample.shape[1])
    xp, xs = x_prompt, x_sample
    pshape, sshape = xp.shape, xs.shape
    n_p = pshape[0] * pshape[1]
    outs = {k: [] for k in ('a_p', 'a_s', 'bkv_p', 'bkv_s', 'bidx_p', 'bidx_s',
                            'ccmp_p', 'ccmp_s', 'cslc_p', 'cslc_s', 'cwin_p', 'cwin_s')}
    for i in range(DEPTH):
        j = i // 2
        if i % 2 == 0:
            ab = (w_in_ab[j], s5_a_re[j], s5_a_im[j], s5_log_dt[j], s5_b_re[j], s5_b_im[j],
                  s5_c_re[j], s5_c_im[j], s5_d[j], s5_w_glu[j], w_out_ab[j])
            mp, st_p, kv_p, ki_p = mixer_ab(xp, pos_p, *ab)
            ms, st_s, kv_n, ki_n = mixer_ab(xs, pos_s, *ab, state=state_a[j], cache_kv=cache_b_kv[j],
                                            cache_idx=cache_b_idx[j], page_table=page_table)
            outs['a_p'].append(st_p); outs['a_s'].append(st_s)
            outs['bkv_p'].append(kv_p); outs['bkv_s'].append(kv_n)
            outs['bidx_p'].append(ki_p); outs['bidx_s'].append(ki_n)
        else:
            cp = (w_in_c[j], cmp_pos[j], cmp_w1[j], cmp_w2[j], w_out_c[j])
            mp, kc_p, ksl_p, win_p = mixer_c(xp, pos_p, *cp)
            ms, kc_n, ksl_n, win_n = mixer_c(xs, pos_s, *cp, cache_cmp=cache_c_cmp[j], cache_slc=cache_c_slc[j],
                                             win=state_c_win[j], page_table=page_table)
            outs['ccmp_p'].append(kc_p); outs['ccmp_s'].append(kc_n)
            outs['cslc_p'].append(ksl_p); outs['cslc_s'].append(ksl_n)
            outs['cwin_p'].append(win_p); outs['cwin_s'].append(win_n)
        xp = layer_norm(ALPHA * xp + mp, ln1_g[i], ln1_b[i])
        xs = layer_norm(ALPHA * xs + ms, ln1_g[i], ln1_b[i])
        moe = (moe_w_group[i], moe_w_expert[i], moe_w_gate[i], moe_w_up[i], moe_w_down[i])
        yp, ys = hier_moe([xp.reshape(n_p, D_MODEL), xs.reshape(-1, D_MODEL)], *moe)
        xp = layer_norm(ALPHA * xp + yp.reshape(pshape), ln2_g[i], ln2_b[i])
        xs = layer_norm(ALPHA * xs + ys.reshape(sshape), ln2_g[i], ln2_b[i])
    st = lambda k: jnp.stack(outs[k])
    return (xp, xs, st('a_p'), st('a_s'), st('bkv_p'), st('bkv_s'), st('bidx_p'), st('bidx_s'),
            st('ccmp_p'), st('ccmp_s'), st('cslc_p'), st('cslc_s'), st('cwin_p'), st('cwin_s'))
```

```python
import functools
import math

import jax
import jax.numpy as jnp
import numpy as np
from jax import lax
from jax.experimental import pallas as pl
from jax.experimental.pallas import tpu as pltpu

D_MODEL = 1024
DEPTH = 2
PAGE_SIZE = 128

S5_WIDTH = D_MODEL // 2
S5_GROUP = 16
S5_GROUPS = S5_WIDTH // S5_GROUP
S5_STATE = 64
B_HEADS = 8
B_DH = 64
B_KV_HEADS = 2
B_ROT = B_DH // 4
IDX_HEADS = 4
IDX_DH = 64
IDX_ROT = IDX_DH // 4
DSA_TOPK = 256
C_HEADS = 8
C_DH = 128
C_KV = 2
C_REP = C_HEADS // C_KV
C_ROT = C_DH // 4
CMP_BLOCK = 64
SEL_BLOCK = 64
N_SEL = 16
WINDOW = 512
N_EGROUPS = 4
E_PER_GROUP = 8
N_EXPERTS = N_EGROUPS * E_PER_GROUP
TOP_E = 2
D_FF_E = 512

ROPE_THETA = 500000.0
ALPHA = (2 * DEPTH) ** 0.25
Q_BLOCK = 128
LN_EPS = 1e-5
NEG = -1e30
FORCE = 1e4
LOG2E = math.log2(math.e)

AB_SIZES = (S5_WIDTH, B_HEADS * B_DH, B_KV_HEADS * 2 * B_DH, IDX_HEADS * IDX_DH, IDX_DH, IDX_HEADS)
C_SIZES = (C_HEADS * C_DH, C_KV * 2 * C_DH, C_KV * 2 * C_DH, C_KV * 2 * C_DH, C_HEADS * 3)

V7X_LANES = 128
V7X_SUBLANES = 8
V7X_VMEM_BYTES = 64 * 1024 * 1024
VMEM_LIMIT_BYTES = 48 * 1024 * 1024

MOE_ROUTER_TILE = 512
MOE_EXPERT_TILE = 256


def _moe_router_kernel(x_ref, w_ref, o_ref):
    logits = jnp.dot(x_ref[...], w_ref[...], precision=lax.Precision.HIGHEST,
                     preferred_element_type=jnp.float32)
    col = lax.broadcasted_iota(jnp.int32, logits.shape, 1)
    big = jnp.int32(1 << 20)
    ninf = jnp.float32(-jnp.inf)
    gl = jnp.where(col < N_EGROUPS, logits, ninf)
    gmax = jnp.max(gl, axis=-1, keepdims=True)
    g_top = jnp.min(jnp.where(gl == gmax, col, big), axis=-1, keepdims=True)
    g_w = 1.0 / jnp.sum(jnp.exp(gl - gmax), axis=-1, keepdims=True)
    lo = N_EGROUPS + E_PER_GROUP * g_top
    el = jnp.where((col >= lo) & (col < lo + E_PER_GROUP), logits, ninf)
    v1 = jnp.max(el, axis=-1, keepdims=True)
    i1 = jnp.min(jnp.where(el == v1, col, big), axis=-1, keepdims=True)
    el2 = jnp.where(col == i1, ninf, el)
    v2 = jnp.max(el2, axis=-1, keepdims=True)
    i2 = jnp.min(jnp.where(el2 == v2, col, big), axis=-1, keepdims=True)
    e2 = jnp.exp(v2 - v1)
    den = 1.0 + e2
    w1 = g_w / den
    w2 = g_w * e2 / den
    out = jnp.where(col == 0, (i1 - N_EGROUPS).astype(jnp.float32),
                    jnp.where(col == 1, (i2 - N_EGROUPS).astype(jnp.float32),
                              jnp.where(col == 2, w1, jnp.where(col == 3, w2, 0.0))))
    o_ref[...] = out


def _moe_route(x, w_router):
    t = x.shape[0]
    tm = min(MOE_ROUTER_TILE, t)
    assert t % tm == 0
    return pl.pallas_call(
        _moe_router_kernel,
        grid=(t // tm,),
        in_specs=[pl.BlockSpec((tm, D_MODEL), lambda i: (i, 0)),
                  pl.BlockSpec((D_MODEL, V7X_LANES), lambda i: (0, 0))],
        out_specs=pl.BlockSpec((tm, V7X_LANES), lambda i: (i, 0)),
        out_shape=jax.ShapeDtypeStruct((t, V7X_LANES), jnp.float32),
        name="moe_router",
    )(x, w_router)


def _moe_expert_kernel(tile_e_ref, nvalid_ref, xs_ref, wg_ref, wu_ref, wd_ref, sw_ref, o_ref,
                       wg_s, wu_s, wd_s):
    i = pl.program_id(0)
    valid = i < nvalid_ref[0]
    prev_e = tile_e_ref[jnp.maximum(i - 1, 0)]
    new_expert = jnp.logical_or(i == 0, tile_e_ref[i] != prev_e)

    @pl.when(jnp.logical_and(valid, new_expert))
    def _():
        wg_s[...] = wg_ref[0].astype(jnp.bfloat16)
        wu_s[...] = wu_ref[0].astype(jnp.bfloat16)
        wd_s[...] = wd_ref[0].astype(jnp.bfloat16)

    @pl.when(valid)
    def _():
        x = xs_ref[...]
        g = jnp.dot(x, wg_s[...], preferred_element_type=jnp.float32)
        u = jnp.dot(x, wu_s[...], preferred_element_type=jnp.float32)
        h = (g * jax.nn.sigmoid(g)) * u
        y = jnp.dot(h.astype(jnp.bfloat16), wd_s[...], preferred_element_type=jnp.float32)
        o_ref[...] = y * sw_ref[...]

    @pl.when(jnp.logical_not(valid))
    def _():
        o_ref[...] = jnp.zeros_like(o_ref)


def _moe_experts(tile_e, nvalid, xs, slot_w, w_gate, w_up, w_down):
    np_rows = xs.shape[0]
    tm = MOE_EXPERT_TILE
    n_tiles = np_rows // tm
    grid_spec = pltpu.PrefetchScalarGridSpec(
        num_scalar_prefetch=2,
        grid=(n_tiles,),
        in_specs=[
            pl.BlockSpec((tm, D_MODEL), lambda i, te, nv: (i, 0)),
            pl.BlockSpec((1, D_MODEL, D_FF_E), lambda i, te, nv: (te[i], 0, 0)),
            pl.BlockSpec((1, D_MODEL, D_FF_E), lambda i, te, nv: (te[i], 0, 0)),
            pl.BlockSpec((1, D_FF_E, D_MODEL), lambda i, te, nv: (te[i], 0, 0)),
            pl.BlockSpec((tm, 1), lambda i, te, nv: (i, 0)),
        ],
        out_specs=pl.BlockSpec((tm, D_MODEL), lambda i, te, nv: (i, 0)),
        scratch_shapes=[pltpu.VMEM((D_MODEL, D_FF_E), jnp.bfloat16),
                        pltpu.VMEM((D_MODEL, D_FF_E), jnp.bfloat16),
                        pltpu.VMEM((D_FF_E, D_MODEL), jnp.bfloat16)],
    )
    return pl.pallas_call(
        _moe_expert_kernel,
        grid_spec=grid_spec,
        out_shape=jax.ShapeDtypeStruct((np_rows, D_MODEL), jnp.float32),
        compiler_params=pltpu.CompilerParams(dimension_semantics=("arbitrary",),
                                             vmem_limit_bytes=VMEM_LIMIT_BYTES),
        name="moe_experts",
    )(tile_e, nvalid, xs, w_gate, w_up, w_down, slot_w)


def _moe_padded_rows(t):
    tm = MOE_EXPERT_TILE
    slots = t * TOP_E
    return ((slots + N_EXPERTS * (tm - 1)) // tm + 1) * tm


def hier_moe(xs_list, w_group, w_expert, w_gate, w_up, w_down):
    tm = MOE_EXPERT_TILE
    w_router = jnp.zeros((D_MODEL, V7X_LANES), jnp.float32)
    w_router = w_router.at[:, :N_EGROUPS].set(w_group).at[:, N_EGROUPS:N_EGROUPS + N_EXPERTS].set(w_expert)
    routed = jnp.concatenate([_moe_route(x, w_router) for x in xs_list], axis=0)
    t = routed.shape[0]
    eid = routed[:, 0:2].astype(jnp.int32).reshape(-1)
    ew = routed[:, 2:4].reshape(-1)
    n_slots = t * TOP_E
    np_rows = _moe_padded_rows(t)

    order = jnp.argsort(eid, stable=True).astype(jnp.int32)
    e_sorted = eid[order]
    counts = jnp.zeros((N_EXPERTS,), jnp.int32).at[eid].add(1)
    padded = ((counts + tm - 1) // tm) * tm
    pad_end = jnp.cumsum(padded)
    pad_off = pad_end - padded
    off = jnp.cumsum(counts) - counts
    dest = pad_off[e_sorted] + (jnp.arange(n_slots, dtype=jnp.int32) - off[e_sorted])
    src_tok = jnp.zeros((np_rows,), jnp.int32).at[dest].set(order // TOP_E)
    slot_w = jnp.zeros((np_rows,), jnp.float32).at[dest].set(ew[order])
    pos = jnp.zeros((n_slots,), jnp.int32).at[order].set(dest)
    tile_start = jnp.arange(np_rows // tm, dtype=jnp.int32) * tm
    tile_e = jnp.minimum(jnp.searchsorted(pad_end, tile_start, side='right'), N_EXPERTS - 1).astype(jnp.int32)
    nvalid = (pad_end[-1] // tm).astype(jnp.int32).reshape(1)

    x_all = jnp.concatenate([x.astype(jnp.bfloat16) for x in xs_list], axis=0)
    ys = _moe_experts(tile_e, nvalid, x_all[src_tok], slot_w[:, None], w_gate, w_up, w_down)
    pos2 = pos.reshape(t, TOP_E)
    y = ys[pos2[:, 0]] + ys[pos2[:, 1]]
    out, start = [], 0
    for x in xs_list:
        out.append(y[start:start + x.shape[0]])
        start += x.shape[0]
    return out


INT_MIN = -(2 ** 31)
INT_MAX = 2 ** 31 - 1


def _f32_order_key(x):
    b = lax.bitcast_convert_type(x + 0.0, jnp.int32)
    return jnp.where(b >= 0, b, b ^ jnp.int32(INT_MAX))


def _count_rows(key_ref, n_chunks, pred_fns):
    _, rows, w = key_ref.shape

    def body(c, accs):
        blk = key_ref[c]
        idx = c * w + lax.broadcasted_iota(jnp.int32, (rows, w), 1)
        out = []
        for fn, acc in zip(pred_fns, accs):
            m = jnp.where(fn(blk, idx), 1, 0)
            part = m[:, 0:V7X_LANES]
            for s in range(1, w // V7X_LANES):
                part = part + m[:, s * V7X_LANES:(s + 1) * V7X_LANES]
            out.append(acc + part)
        return tuple(out)

    init = tuple(jnp.zeros((rows, V7X_LANES), jnp.int32) for _ in pred_fns)
    accs = lax.fori_loop(0, n_chunks, body, init)
    return [jnp.sum(a, axis=-1, keepdims=True) for a in accs]


def _any_row(mask):
    return jnp.max(jnp.where(mask, 1.0, 0.0)) > 0.5


def _topk_threshold(key_ref, n_chunks, k, index_bits):
    _, rows, w = key_ref.shape
    col = lambda v: jnp.full((rows, 1), v, jnp.int32)

    def v_body(st):
        lo, hi, exact, _ = st
        mid = (lo & hi) + ((lo ^ hi) >> 1)
        cnt, = _count_rows(key_ref, n_chunks, [lambda blk, idx: blk > mid])
        active = lo < hi
        hit = active & (cnt == k)
        less = active & (cnt < k)
        more = active & (cnt > k)
        hi = jnp.where(hit | less, mid, hi)
        lo = jnp.where(hit, mid, jnp.where(more, mid + 1, lo))
        exact = jnp.where(hit, 1, exact)
        return lo, hi, exact, _any_row(lo < hi)

    thr, _, exact, _ = lax.while_loop(lambda st: st[3], v_body,
                                      (col(INT_MIN), col(INT_MAX), col(0), jnp.bool_(True)))
    c_gt, c_eq = _count_rows(key_ref, n_chunks, [lambda blk, idx: blk > thr, lambda blk, idx: blk == thr])
    r = k - c_gt
    take_all_ties = (exact == 0) & (thr != INT_MIN)
    need = take_all_ties & (c_eq > r)
    cut_default = jnp.where(take_all_ties, INT_MAX, -1)

    def tie_phase():
        def body(_, st):
            lo_i, hi_i = st
            mid = (lo_i + hi_i) >> 1
            g, = _count_rows(key_ref, n_chunks, [lambda blk, idx: (blk == thr) & (idx <= mid)])
            ok = g >= r
            return jnp.where(ok, lo_i, mid + 1), jnp.where(ok, mid, hi_i)
        lo_i, _ = lax.fori_loop(0, index_bits, body, (col(0), col(0) + (n_chunks * w - 1)))
        return jnp.where(need, lo_i, cut_default)

    cut = lax.cond(_any_row(need), tie_phase, lambda: cut_default)
    return thr, cut


def _selected(key, idx, thr, cut):
    return (key > thr) | ((key == thr) & (idx <= cut))


DSA_CHUNK = 1024


def _split3_lhs(x):
    hi = x.astype(jnp.bfloat16)
    lo = (x - hi.astype(jnp.float32)).astype(jnp.bfloat16)
    return jnp.concatenate([hi, hi, lo], axis=-1)


def _split3_rhs(x):
    hi = x.astype(jnp.bfloat16)
    lo = (x - hi.astype(jnp.float32)).astype(jnp.bfloat16)
    return jnp.concatenate([hi, lo, hi], axis=-1)


_NT = (((1,), (1,)), ((), ()))


def _dsa_prompt_kernel(qi_ref, wi_ref, kcat_ref, q_ref, k_ref, v_ref, o_ref, key_s, *, topk):
    cw = DSA_CHUNK
    rep = B_HEADS // B_KV_HEADS
    j = pl.program_id(1)
    t0 = j * Q_BLOCK
    n_chunks = (t0 + Q_BLOCK + cw - 1) // cw
    qpos = t0 + lax.broadcasted_iota(jnp.int32, (Q_BLOCK, 1), 0)

    qi = qi_ref[0]
    wi = wi_ref[0]
    qcat = jnp.concatenate([_split3_lhs(qi[:, h * IDX_DH:(h + 1) * IDX_DH]) for h in range(IDX_HEADS)], axis=0)

    def score_chunk(c, carry):
        start = pl.multiple_of(c * cw, cw)
        kc = kcat_ref[0, pl.ds(start, cw), :]
        s = lax.dot_general(qcat, kc, _NT, preferred_element_type=jnp.float32) * (IDX_DH ** -0.5)
        acc = None
        for h in range(IDX_HEADS):
            term = wi[:, h:h + 1] * jnp.maximum(s[h * Q_BLOCK:(h + 1) * Q_BLOCK], 0.0)
            acc = term if acc is None else acc + term
        kpos = start + lax.broadcasted_iota(jnp.int32, (Q_BLOCK, cw), 1)
        key_s[c] = jnp.where(kpos <= qpos, _f32_order_key(acc), INT_MIN)
        return carry

    lax.fori_loop(0, n_chunks, score_chunk, 0)
    thr, cut = _topk_threshold(key_s, n_chunks, topk, index_bits=int(math.log2(key_s.shape[0] * cw)))

    q = q_ref[0]
    qg = [jnp.concatenate([q[:, (g * rep + r) * B_DH:(g * rep + r + 1) * B_DH] for r in range(rep)],
                          axis=0).astype(jnp.bfloat16) for g in range(B_KV_HEADS)]

    def att_chunk(c, carry):
        start = pl.multiple_of(c * cw, cw)
        kidx = start + lax.broadcasted_iota(jnp.int32, (Q_BLOCK, cw), 1)
        sel = _selected(key_s[c], kidx, thr, cut)[None]
        out = []
        for g in range(B_KV_HEADS):
            m, l, acc = carry[g]
            kg = k_ref[0, g, pl.ds(start, cw), :]
            vg = v_ref[0, g, pl.ds(start, cw), :]
            s = lax.dot_general(qg[g], kg, _NT, preferred_element_type=jnp.float32) * (B_DH ** -0.5 * LOG2E)
            s = jnp.where(sel, s.reshape(rep, Q_BLOCK, cw), NEG)
            m_new = jnp.maximum(m, jnp.max(s, axis=-1, keepdims=True))
            p = jnp.exp2(s - m_new)
            alpha = jnp.exp2(m - m_new)
            l = alpha * l + jnp.sum(p, axis=-1, keepdims=True)
            pv = jnp.dot(p.reshape(rep * Q_BLOCK, cw).astype(jnp.bfloat16), vg,
                         preferred_element_type=jnp.float32)
            acc = alpha * acc + pv.reshape(rep, Q_BLOCK, B_DH)
            out.append((m_new, l, acc))
        return tuple(out)

    init = tuple((jnp.full((rep, Q_BLOCK, 1), NEG, jnp.float32),
                  jnp.zeros((rep, Q_BLOCK, 1), jnp.float32),
                  jnp.zeros((rep, Q_BLOCK, B_DH), jnp.float32)) for _ in range(B_KV_HEADS))
    res = lax.fori_loop(0, n_chunks, att_chunk, init)
    heads = []
    for g in range(B_KV_HEADS):
        _, l, acc = res[g]
        o = jnp.where(l > 0.0, acc / jnp.where(l > 0.0, l, 1.0), 0.0)
        heads += [o[r] for r in range(rep)]
    o_ref[0] = jnp.concatenate(heads, axis=-1)


def dsa_prompt(q, kv, qi, ki, wi):
    bn, s, _ = q.shape
    topk = min(DSA_TOPK, s // 4)
    cw = DSA_CHUNK
    assert s % cw == 0 and s % Q_BLOCK == 0
    kcat = _split3_rhs(ki)
    k = jnp.transpose(kv[:, :, :, 0, :], (0, 2, 1, 3)).astype(jnp.bfloat16)
    v = jnp.transpose(kv[:, :, :, 1, :], (0, 2, 1, 3)).astype(jnp.bfloat16)
    qw = B_HEADS * B_DH
    return pl.pallas_call(
        functools.partial(_dsa_prompt_kernel, topk=topk),
        grid=(bn, s // Q_BLOCK),
        in_specs=[
            pl.BlockSpec((1, Q_BLOCK, IDX_HEADS * IDX_DH), lambda b, j: (b, j, 0)),
            pl.BlockSpec((1, Q_BLOCK, IDX_HEADS), lambda b, j: (b, j, 0)),
            pl.BlockSpec((1, s, 3 * IDX_DH), lambda b, j: (b, 0, 0)),
            pl.BlockSpec((1, Q_BLOCK, qw), lambda b, j: (b, j, 0)),
            pl.BlockSpec((1, B_KV_HEADS, s, B_DH), lambda b, j: (b, 0, 0, 0)),
            pl.BlockSpec((1, B_KV_HEADS, s, B_DH), lambda b, j: (b, 0, 0, 0)),
        ],
        out_specs=pl.BlockSpec((1, Q_BLOCK, qw), lambda b, j: (b, j, 0)),
        out_shape=jax.ShapeDtypeStruct((bn, s, qw), jnp.float32),
        scratch_shapes=[pltpu.VMEM((s // cw, Q_BLOCK, cw), jnp.int32)],
        compiler_params=pltpu.CompilerParams(dimension_semantics=("arbitrary", "arbitrary"),
                                             vmem_limit_bytes=VMEM_LIMIT_BYTES),
        name="dsa_prompt",
    )(qi, wi, kcat, q, k, v)


NSA_CHUNK = 1024
NSA_WIN_BLOCKS = WINDOW // Q_BLOCK + 1


def _nsa_prompt_kernel(qraw_ref, qrot_ref, gates_ref, kck_ref, kcv_ref, e_ref, ks_ref, vs_ref, *rest, n_sel):
    kw_refs = rest[0:NSA_WIN_BLOCKS]
    vw_refs = rest[NSA_WIN_BLOCKS:2 * NSA_WIN_BLOCKS]
    o_ref, key_s = rest[2 * NSA_WIN_BLOCKS:]
    cw = NSA_CHUNK
    rep = C_REP
    scale = C_DH ** -0.5
    j = pl.program_id(1)
    t0 = j * Q_BLOCK
    n_chunks = (t0 + Q_BLOCK + cw - 1) // cw
    qpos = t0 + lax.broadcasted_iota(jnp.int32, (Q_BLOCK, 1), 0)
    nbp = kck_ref.shape[2]
    blk = lax.broadcasted_iota(jnp.int32, (Q_BLOCK, nbp), 1)
    cmask = ((blk + 1) * CMP_BLOCK - 1) <= qpos
    cur = qpos // SEL_BLOCK
    forced = (blk == 0) | (blk == cur) | (blk == cur - 1)
    valid = blk * SEL_BLOCK <= qpos
    wlen = NSA_WIN_BLOCKS * Q_BLOCK
    wpos = t0 - WINDOW + lax.broadcasted_iota(jnp.int32, (Q_BLOCK, wlen), 1)
    wmask = ((wpos >= 0) & (wpos <= qpos) & (wpos > qpos - WINDOW))[None]

    qraw = qraw_ref[0]
    qrot = qrot_ref[0]
    sig = jax.nn.sigmoid(gates_ref[0])

    def stack(x, g):
        return jnp.concatenate([x[:, (g * rep + r) * C_DH:(g * rep + r + 1) * C_DH] for r in range(rep)], axis=0)

    for g in range(C_KV):
        s = lax.dot_general(stack(qraw, g), kck_ref[0, g], _NT, precision=lax.Precision.HIGHEST,
                            preferred_element_type=jnp.float32) * scale
        s = jnp.where(cmask[None], s.reshape(rep, Q_BLOCK, nbp), NEG)
        e = jnp.exp(s - jnp.max(s, axis=-1, keepdims=True))
        p = e / jnp.sum(e, axis=-1, keepdims=True)
        p = p * jnp.where(jnp.max(jnp.where(cmask, 1.0, 0.0), axis=-1, keepdims=True) > 0.5, 1.0, 0.0)[None]
        o_cmp = jnp.dot(p.reshape(rep * Q_BLOCK, nbp), kcv_ref[0, g], precision=lax.Precision.HIGHEST,
                        preferred_element_type=jnp.float32).reshape(rep, Q_BLOCK, C_DH)
        imp = p[0]
        for r in range(1, rep):
            imp = imp + p[r]
        score = jnp.where(forced, FORCE, imp)
        key_s[0] = jnp.where(valid, _f32_order_key(score), INT_MIN)
        thr, cut = _topk_threshold(key_s, 1, n_sel, index_bits=int(math.log2(nbp)))
        selblk = jnp.where(_selected(key_s[0], blk, thr, cut), 1.0, 0.0).astype(jnp.bfloat16)

        qg = stack(qrot, g).astype(jnp.bfloat16)

        def att_chunk(c, carry):
            m, l, acc = carry
            start = pl.multiple_of(c * cw, cw)
            kpos = start + lax.broadcasted_iota(jnp.int32, (Q_BLOCK, cw), 1)
            hit = jnp.dot(selblk, e_ref[c], preferred_element_type=jnp.float32)
            sel = ((hit > 0.5) & (kpos <= qpos))[None]
            kg = ks_ref[0, g, pl.ds(start, cw), :]
            vg = vs_ref[0, g, pl.ds(start, cw), :]
            sc = lax.dot_general(qg, kg, _NT, preferred_element_type=jnp.float32) * (scale * LOG2E)
            sc = jnp.where(sel, sc.reshape(rep, Q_BLOCK, cw), NEG)
            m_new = jnp.maximum(m, jnp.max(sc, axis=-1, keepdims=True))
            pp = jnp.exp2(sc - m_new)
            alpha = jnp.exp2(m - m_new)
            l = alpha * l + jnp.sum(pp, axis=-1, keepdims=True)
            pv = jnp.dot(pp.reshape(rep * Q_BLOCK, cw).astype(jnp.bfloat16), vg, preferred_element_type=jnp.float32)
            return m_new, l, alpha * acc + pv.reshape(rep, Q_BLOCK, C_DH)

        init = (jnp.full((rep, Q_BLOCK, 1), NEG, jnp.float32), jnp.zeros((rep, Q_BLOCK, 1), jnp.float32),
                jnp.zeros((rep, Q_BLOCK, C_DH), jnp.float32))
        _, l, acc = lax.fori_loop(0, n_chunks, att_chunk, init)
        o_slc = jnp.where(l > 0.0, acc / jnp.where(l > 0.0, l, 1.0), 0.0)

        kwin = jnp.concatenate([r_[0, g] for r_ in kw_refs], axis=0)
        vwin = jnp.concatenate([r_[0, g] for r_ in vw_refs], axis=0)
        sw = lax.dot_general(qg, kwin, _NT, preferred_element_type=jnp.float32) * scale
        sw = jnp.where(wmask, sw.reshape(rep, Q_BLOCK, wlen), NEG)
        ew = jnp.where(wmask, jnp.exp(sw - jnp.max(sw, axis=-1, keepdims=True)), 0.0)
        lw = jnp.sum(ew, axis=-1, keepdims=True)
        pw = ew / jnp.where(lw > 0.0, lw, 1.0)
        o_win = jnp.dot(pw.reshape(rep * Q_BLOCK, wlen).astype(jnp.bfloat16), vwin,
                        preferred_element_type=jnp.float32).reshape(rep, Q_BLOCK, C_DH)

        for r in range(rep):
            hh = g * rep + r
            o = (sig[:, 3 * hh:3 * hh + 1] * o_cmp[r] + sig[:, 3 * hh + 1:3 * hh + 2] * o_slc[r]
                 + sig[:, 3 * hh + 2:3 * hh + 3] * o_win[r])
            o_ref[0, :, hh * C_DH:(hh + 1) * C_DH] = o


def nsa_prompt(q_raw, q_rot, gates, kck, kcv, kv_s, kv_w):
    bn, s, _ = q_raw.shape
    cw = NSA_CHUNK
    assert s % cw == 0
    nb = kck.shape[2]
    nbs = -(-s // SEL_BLOCK)
    assert nb == nbs
    nbp = -(-nb // V7X_LANES) * V7X_LANES
    n_sel = min(N_SEL, nbs)
    kck = jnp.pad(kck, ((0, 0), (0, 0), (0, nbp - nb), (0, 0)))
    kcv = jnp.pad(kcv, ((0, 0), (0, 0), (0, nbp - nb), (0, 0)))
    split = lambda kv, c: jnp.transpose(kv[:, :, :, c, :], (0, 2, 1, 3)).astype(jnp.bfloat16)
    ks, vs, kw, vw = split(kv_s, 0), split(kv_s, 1), split(kv_w, 0), split(kv_w, 1)
    expand = (jnp.arange(s, dtype=jnp.int32)[None, :] // SEL_BLOCK == jnp.arange(nbp, dtype=jnp.int32)[:, None])
    expand = jnp.transpose(expand.astype(jnp.bfloat16).reshape(nbp, s // cw, cw), (1, 0, 2))
    qd = C_HEADS * C_DH
    nq = s // Q_BLOCK
    wb = NSA_WIN_BLOCKS

    def win_spec(slot):
        return pl.BlockSpec((1, C_KV, Q_BLOCK, C_DH),
                            lambda b, j: (b, 0, jnp.maximum(j - (wb - 1) + slot, 0), 0))

    return pl.pallas_call(
        functools.partial(_nsa_prompt_kernel, n_sel=n_sel),
        grid=(bn, nq),
        in_specs=[
            pl.BlockSpec((1, Q_BLOCK, qd), lambda b, j: (b, j, 0)),
            pl.BlockSpec((1, Q_BLOCK, qd), lambda b, j: (b, j, 0)),
            pl.BlockSpec((1, Q_BLOCK, C_HEADS * 3), lambda b, j: (b, j, 0)),
            pl.BlockSpec((1, C_KV, nbp, C_DH), lambda b, j: (b, 0, 0, 0)),
            pl.BlockSpec((1, C_KV, nbp, C_DH), lambda b, j: (b, 0, 0, 0)),
            pl.BlockSpec((s // cw, nbp, cw), lambda b, j: (0, 0, 0)),
            pl.BlockSpec((1, C_KV, s, C_DH), lambda b, j: (b, 0, 0, 0)),
            pl.BlockSpec((1, C_KV, s, C_DH), lambda b, j: (b, 0, 0, 0)),
        ] + [win_spec(i) for i in range(wb)] + [win_spec(i) for i in range(wb)],
        out_specs=pl.BlockSpec((1, Q_BLOCK, qd), lambda b, j: (b, j, 0)),
        out_shape=jax.ShapeDtypeStruct((bn, s, qd), jnp.float32),
        scratch_shapes=[pltpu.VMEM((1, Q_BLOCK, nbp), jnp.int32)],
        compiler_params=pltpu.CompilerParams(dimension_semantics=("arbitrary", "arbitrary"),
                                             vmem_limit_bytes=VMEM_LIMIT_BYTES),
        name="nsa_prompt",
    )(q_raw, q_rot, gates, kck, kcv, expand, ks, vs, *([kw] * wb), *([vw] * wb))


S5_N = S5_GROUPS * S5_STATE
S5_TILES = 4
S5_TILE_IN = S5_WIDTH // S5_TILES
S5_TILE_N = S5_N // S5_TILES
S5_CHUNK = 256
S5_SUB = S5_CHUNK // V7X_SUBLANES


def _s5_input_map(u, bcat_ref, store):
    for i in range(S5_TILES):
        ucat = _split3_lhs(u[:, i * S5_TILE_IN:(i + 1) * S5_TILE_IN])
        r = jnp.dot(ucat, bcat_ref[i], preferred_element_type=jnp.float32)
        store(i, r[:, :S5_TILE_N], r[:, S5_TILE_N:])


def _s5_output_map(h_tile, u, ccat_ref, d_ref, wglu_ref):
    ys = []
    for i in range(S5_TILES):
        re, im = h_tile(i)
        hcat = jnp.concatenate([re, im], axis=-1).astype(jnp.bfloat16)
        ys.append(jnp.dot(hcat, ccat_ref[i], preferred_element_type=jnp.float32))
    y = jax.nn.gelu(jnp.concatenate(ys, axis=-1) + d_ref[...] * u)
    z = jnp.dot(y.astype(jnp.bfloat16), wglu_ref[...], preferred_element_type=jnp.float32)
    return y * jax.nn.sigmoid(z)


def _cmul_add(a_re, a_im, x_re, x_im, b_re, b_im):
    return a_re * x_re - a_im * x_im + b_re, a_re * x_im + a_im * x_re + b_im


def _s5_prompt_kernel(u_ref, h0_ref, bcat_ref, lam8_ref, lamm_ref, ccat_ref, d_ref, wglu_ref,
                      y_ref, hlast_ref, hs, ein_s, carry_s):
    n = S5_N
    m = S5_SUB
    c = pl.program_id(1)

    @pl.when(c == 0)
    def _():
        carry_s[...] = h0_ref[0]

    u = u_ref[0]

    def store_bu(i, re, im):
        hs[:, i * S5_TILE_N:(i + 1) * S5_TILE_N] = re
        hs[:, n + i * S5_TILE_N:n + (i + 1) * S5_TILE_N] = im

    _s5_input_map(u, bcat_ref, store_bu)

    def scan_body(k, h):
        row = pl.multiple_of(k * V7X_SUBLANES, V7X_SUBLANES)
        rows = pl.ds(row, V7X_SUBLANES)
        n_re, n_im = _cmul_add(lam8_ref[:, :n], lam8_ref[:, n:], h[0], h[1], hs[rows, :n], hs[rows, n:])
        hs[rows, :n] = n_re
        hs[rows, n:] = n_im
        return n_re, n_im

    zero = jnp.zeros((V7X_SUBLANES, n), jnp.float32)
    ends = lax.fori_loop(0, m, scan_body, (zero, zero))

    lm_re, lm_im = lamm_ref[:, :n], lamm_ref[:, n:]
    e_re, e_im = carry_s[:, :n], carry_s[:, n:]
    for s in range(V7X_SUBLANES):
        ein_s[s:s + 1, :n] = e_re
        ein_s[s:s + 1, n:] = e_im
        e_re, e_im = _cmul_add(lm_re, lm_im, e_re, e_im, ends[0][s:s + 1], ends[1][s:s + 1])
    carry_s[:, :n] = e_re
    carry_s[:, n:] = e_im
    hlast_ref[0] = carry_s[...]

    def fix_body(k, corr):
        row = pl.multiple_of(k * V7X_SUBLANES, V7X_SUBLANES)
        rows = pl.ds(row, V7X_SUBLANES)
        c_re, c_im = _cmul_add(lam8_ref[:, :n], lam8_ref[:, n:], corr[0], corr[1], 0.0, 0.0)
        hs[rows, :n] = hs[rows, :n] + c_re
        hs[rows, n:] = hs[rows, n:] + c_im
        return c_re, c_im

    lax.fori_loop(0, m, fix_body, (ein_s[:, :n], ein_s[:, n:]))

    def h_tile(i):
        return hs[:, i * S5_TILE_N:(i + 1) * S5_TILE_N], hs[:, n + i * S5_TILE_N:n + (i + 1) * S5_TILE_N]

    y_ref[0] = _s5_output_map(h_tile, u, ccat_ref, d_ref, wglu_ref)


def _s5_sample_kernel(u_ref, h0_ref, bcat_ref, lam_ref, ccat_ref, d_ref, wglu_ref, y_ref, hlast_ref, hs):
    n = S5_N
    steps, bd, _ = u_ref.shape
    h_re, h_im = h0_ref[:, :n], h0_ref[:, n:]
    lam_re, lam_im = lam_ref[:, :n], lam_ref[:, n:]
    for t in range(steps):
        u = u_ref[t]

        def store_bu(i, re, im):
            hs[:, i * S5_TILE_N:(i + 1) * S5_TILE_N] = re
            hs[:, n + i * S5_TILE_N:n + (i + 1) * S5_TILE_N] = im

        _s5_input_map(u, bcat_ref, store_bu)
        h_re, h_im = _cmul_add(lam_re, lam_im, h_re, h_im, hs[:, :n], hs[:, n:])
        hs[:, :n] = h_re
        hs[:, n:] = h_im

        def h_tile(i):
            return hs[:, i * S5_TILE_N:(i + 1) * S5_TILE_N], hs[:, n + i * S5_TILE_N:n + (i + 1) * S5_TILE_N]

        y_ref[t] = _s5_output_map(h_tile, u, ccat_ref, d_ref, wglu_ref)
    hlast_ref[:, :n] = h_re
    hlast_ref[:, n:] = h_im


def _s5_params(a_re, a_im, log_dt, b_re, b_im, c_re, c_im):
    lam = lax.complex(a_re, a_im)
    dt = jnp.exp(log_dt)[:, None]
    lam_bar = jnp.exp(lam * dt)
    b_bar = ((lam_bar - 1.0) / lam)[:, :, None] * lax.complex(b_re, b_im)
    gpt = S5_GROUPS // S5_TILES
    eye = jnp.eye(gpt, dtype=jnp.float32)

    def in_blockdiag(w):
        w = w.reshape(S5_TILES, gpt, S5_STATE, S5_GROUP)
        return jnp.einsum('tgpc,gh->tgchp', w, eye).reshape(S5_TILES, S5_TILE_IN, S5_TILE_N)

    def out_blockdiag(w):
        w = w.reshape(S5_TILES, gpt, S5_GROUP, S5_STATE)
        return jnp.einsum('tgcp,gh->tgphc', w, eye).reshape(S5_TILES, S5_TILE_N, S5_TILE_IN)

    b_full = jnp.concatenate([in_blockdiag(b_bar.real), in_blockdiag(b_bar.imag)], axis=-1)
    b_hi = b_full.astype(jnp.bfloat16)
    b_lo = (b_full - b_hi.astype(jnp.float32)).astype(jnp.bfloat16)
    bcat = jnp.concatenate([b_hi, b_lo, b_hi], axis=1)
    ccat = jnp.concatenate([out_blockdiag(c_re), out_blockdiag(-c_im)], axis=1).astype(jnp.bfloat16)
    flat = lambda z: jnp.concatenate([z.real.reshape(1, S5_N), z.imag.reshape(1, S5_N)], axis=-1)
    lam_row = flat(lam_bar)
    lamm_row = flat(jnp.exp(lam * dt * S5_SUB))
    return bcat, ccat, lam_row, lamm_row


def s5_prompt(u, a_re, a_im, log_dt, b_re, b_im, c_re, c_im, d, w_glu):
    bn, s, _ = u.shape
    tc, m = S5_CHUNK, S5_SUB
    assert s % tc == 0
    nc = s // tc
    bcat, ccat, lam_row, lamm_row = _s5_params(a_re, a_im, log_dt, b_re, b_im, c_re, c_im)
    lam8 = jnp.broadcast_to(lam_row, (V7X_SUBLANES, 2 * S5_N))
    to_ks = lambda a: a.reshape(bn, nc, V7X_SUBLANES, m, S5_WIDTH).swapaxes(2, 3).reshape(bn, s, S5_WIDTH)
    from_ks = lambda a: a.reshape(bn, nc, m, V7X_SUBLANES, S5_WIDTH).swapaxes(2, 3).reshape(bn, s, S5_WIDTH)
    h0 = jnp.zeros((bn, 1, 2 * S5_N), jnp.float32)
    const = lambda shape: pl.BlockSpec(shape, lambda b, c: (0,) * len(shape))
    y, hlast = pl.pallas_call(
        _s5_prompt_kernel,
        grid=(bn, nc),
        in_specs=[
            pl.BlockSpec((1, tc, S5_WIDTH), lambda b, c: (b, c, 0)),
            pl.BlockSpec((1, 1, 2 * S5_N), lambda b, c: (b, 0, 0)),
            const(bcat.shape), const(lam8.shape), const(lamm_row.shape), const(ccat.shape),
            const((1, S5_WIDTH)), const((S5_WIDTH, S5_WIDTH)),
        ],
        out_specs=[pl.BlockSpec((1, tc, S5_WIDTH), lambda b, c: (b, c, 0)),
                   pl.BlockSpec((1, 1, 2 * S5_N), lambda b, c: (b, 0, 0))],
        out_shape=[jax.ShapeDtypeStruct((bn, s, S5_WIDTH), jnp.float32),
                   jax.ShapeDtypeStruct((bn, 1, 2 * S5_N), jnp.float32)],
        scratch_shapes=[pltpu.VMEM((tc, 2 * S5_N), jnp.float32),
                        pltpu.VMEM((V7X_SUBLANES, 2 * S5_N), jnp.float32),
                        pltpu.VMEM((1, 2 * S5_N), jnp.float32)],
        compiler_params=pltpu.CompilerParams(dimension_semantics=("arbitrary", "arbitrary"),
                                             vmem_limit_bytes=VMEM_LIMIT_BYTES),
        name="s5_prompt",
    )(to_ks(u), h0, bcat, lam8, lamm_row, ccat, d.reshape(1, S5_WIDTH), w_glu.astype(jnp.bfloat16))
    return from_ks(y), hlast.reshape(bn, 2, S5_GROUPS, S5_STATE)


def s5_sample(u, state, a_re, a_im, log_dt, b_re, b_im, c_re, c_im, d, w_glu):
    bd, steps, _ = u.shape
    bcat, ccat, lam_row, _ = _s5_params(a_re, a_im, log_dt, b_re, b_im, c_re, c_im)
    y, hlast = pl.pallas_call(
        _s5_sample_kernel,
        out_shape=[jax.ShapeDtypeStruct((steps, bd, S5_WIDTH), jnp.float32),
                   jax.ShapeDtypeStruct((bd, 2 * S5_N), jnp.float32)],
        scratch_shapes=[pltpu.VMEM((bd, 2 * S5_N), jnp.float32)],
        compiler_params=pltpu.CompilerParams(vmem_limit_bytes=VMEM_LIMIT_BYTES),
        name="s5_sample",
    )(jnp.swapaxes(u, 0, 1), state.reshape(bd, 2 * S5_N), bcat, lam_row, ccat,
      d.reshape(1, S5_WIDTH), w_glu.astype(jnp.bfloat16))
    return jnp.swapaxes(y, 0, 1), hlast.reshape(bd, 2, S5_GROUPS, S5_STATE)


SAMPLE_PAGES_PER_STEP = 16


def _dsa_sample_kernel(pt_ref, qi_ref, wi_ref, q_ref, kinew_ref, kvnew_ref, *rest, topk, past_len):
    pg = SAMPLE_PAGES_PER_STEP
    idx_refs = rest[0:pg]
    kv_refs = rest[pg:2 * pg]
    o_ref, key_s, kv_all = rest[2 * pg:]
    cw = pg * PAGE_SIZE
    n_steps = past_len // cw
    steps_q = q_ref.shape[1]
    rows = key_s.shape[1]
    rep = B_HEADS // B_KV_HEADS
    kvw = B_KV_HEADS * 2 * B_DH
    s_id = pl.program_id(1)
    qpos = past_len + lax.broadcasted_iota(jnp.int32, (rows, 1), 0)

    qi = qi_ref[0]
    wi = wi_ref[0]
    qcat = jnp.concatenate([_split3_lhs(qi[:, h * IDX_DH:(h + 1) * IDX_DH]) for h in range(IDX_HEADS)], axis=0)
    row_ok = lax.broadcasted_iota(jnp.int32, (rows, 1), 0) < steps_q

    def scores(kidx_rows, kpos):
        s = lax.dot_general(qcat, _split3_rhs(kidx_rows), _NT, preferred_element_type=jnp.float32) * (IDX_DH ** -0.5)
        acc = None
        for h in range(IDX_HEADS):
            term = wi[:, h:h + 1] * jnp.maximum(s[h * steps_q:(h + 1) * steps_q], 0.0)
            acc = term if acc is None else acc + term
        acc = jnp.concatenate([acc, jnp.zeros((rows - steps_q, acc.shape[1]), jnp.float32)], axis=0)
        return jnp.where((kpos <= qpos) & row_ok, _f32_order_key(acc), INT_MIN)

    kpos = s_id * cw + lax.broadcasted_iota(jnp.int32, (rows, cw), 1)
    key_s[s_id] = scores(jnp.concatenate([r[0] for r in idx_refs], axis=0), kpos)
    for i in range(pg):
        start = pl.multiple_of(s_id * cw + i * PAGE_SIZE, PAGE_SIZE)
        kv_all[pl.ds(start, PAGE_SIZE), :] = kv_refs[i][0].astype(jnp.bfloat16)

    @pl.when(s_id == n_steps - 1)
    def _():
        pad = jnp.zeros((cw - steps_q, IDX_DH), jnp.float32)
        kpos_new = past_len + lax.broadcasted_iota(jnp.int32, (rows, cw), 1)
        new_key = scores(jnp.concatenate([kinew_ref[0], pad], axis=0), kpos_new)
        lane = lax.broadcasted_iota(jnp.int32, (rows, cw), 1)
        key_s[n_steps] = jnp.where(lane < steps_q, new_key, INT_MIN)
        kv_all[pl.ds(past_len, cw), :] = jnp.concatenate(
            [kvnew_ref[0], jnp.zeros((cw - steps_q, kvw), jnp.float32)], axis=0).astype(jnp.bfloat16)
        n_chunks = n_steps + 1
        thr, cut = _topk_threshold(key_s, n_chunks, topk, index_bits=int(math.ceil(math.log2(n_chunks * cw))))

        q = q_ref[0]
        qrows = []
        for h in range(B_HEADS):
            g = h // rep
            qh = q[:, h * B_DH:(h + 1) * B_DH]
            pieces = [jnp.zeros((steps_q, g * 2 * B_DH), jnp.float32)] if g else []
            pieces += [qh, jnp.zeros((steps_q, kvw - g * 2 * B_DH - B_DH), jnp.float32)]
            qrows.append(jnp.concatenate(pieces, axis=-1))
        qx = jnp.concatenate(qrows, axis=0).astype(jnp.bfloat16)
        nr = B_HEADS * steps_q

        def att_chunk(c, carry):
            m, l, acc = carry
            start = pl.multiple_of(c * cw, cw)
            kidx = start + lax.broadcasted_iota(jnp.int32, (rows, cw), 1)
            sel = _selected(key_s[c], kidx, thr, cut)[0:steps_q][None]
            kvc = kv_all[pl.ds(start, cw), :]
            s = lax.dot_general(qx, kvc, _NT, preferred_element_type=jnp.float32) * (B_DH ** -0.5)
            s = jnp.where(sel, s.reshape(B_HEADS, steps_q, cw), NEG)
            m_new = jnp.maximum(m, jnp.max(s, axis=-1, keepdims=True))
            p = jnp.where(sel, jnp.exp(s - m_new), 0.0)
            alpha = jnp.exp(m - m_new)
            l = alpha * l + jnp.sum(p, axis=-1, keepdims=True)
            pv = jnp.dot(p.reshape(nr, cw).astype(jnp.bfloat16), kvc, preferred_element_type=jnp.float32)
            return m_new, l, alpha * acc + pv.reshape(B_HEADS, steps_q, kvw)

        init = (jnp.full((B_HEADS, steps_q, 1), NEG, jnp.float32), jnp.zeros((B_HEADS, steps_q, 1), jnp.float32),
                jnp.zeros((B_HEADS, steps_q, kvw), jnp.float32))
        _, l, acc = lax.fori_loop(0, n_chunks, att_chunk, init)
        o = jnp.where(l > 0.0, acc / jnp.where(l > 0.0, l, 1.0), 0.0)
        heads = []
        for h in range(B_HEADS):
            g = h // rep
            heads.append(o[h][:, g * 2 * B_DH + B_DH:(g + 1) * 2 * B_DH])
        o_ref[0] = jnp.concatenate(heads, axis=-1)


def dsa_sample(q, kv, qi, ki, wi, cache_kv, cache_idx, page_table):
    bd, steps_q, _ = q.shape
    n_pages = page_table.shape[1]
    past_len = n_pages * PAGE_SIZE
    pg = SAMPLE_PAGES_PER_STEP
    assert n_pages % pg == 0
    n_steps = n_pages // pg
    cw = pg * PAGE_SIZE
    topk = min(DSA_TOPK, (past_len + steps_q) // 4)
    kvw = B_KV_HEADS * 2 * B_DH
    n_pool = cache_kv.shape[0]
    rows = V7X_SUBLANES
    assert steps_q <= rows

    def page_spec(i, width):
        return pl.BlockSpec((1, PAGE_SIZE, width), lambda b, s, pt: (pt[b * n_pages + s * pg + i], 0, 0))

    per_b = lambda shape: pl.BlockSpec((1,) + shape, lambda b, s, pt: (b, 0, 0))
    grid_spec = pltpu.PrefetchScalarGridSpec(
        num_scalar_prefetch=1,
        grid=(bd, n_steps),
        in_specs=[per_b((steps_q, IDX_HEADS * IDX_DH)), per_b((steps_q, IDX_HEADS)), per_b((steps_q, B_HEADS * B_DH)),
                  per_b((steps_q, IDX_DH)), per_b((steps_q, kvw))]
        + [page_spec(i, IDX_DH) for i in range(pg)] + [page_spec(i, kvw) for i in range(pg)],
        out_specs=per_b((steps_q, B_HEADS * B_DH)),
        scratch_shapes=[pltpu.VMEM((n_steps + 1, rows, cw), jnp.int32),
                        pltpu.VMEM((past_len + cw, kvw), jnp.bfloat16)],
    )
    return pl.pallas_call(
        functools.partial(_dsa_sample_kernel, topk=topk, past_len=past_len),
        grid_spec=grid_spec,
        out_shape=jax.ShapeDtypeStruct((bd, steps_q, B_HEADS * B_DH), jnp.float32),
        compiler_params=pltpu.CompilerParams(dimension_semantics=("arbitrary", "arbitrary"),
                                             vmem_limit_bytes=VMEM_LIMIT_BYTES),
        name="dsa_sample",
    )(page_table.reshape(-1), qi, wi, q, ki, kv.reshape(bd, steps_q, kvw),
      *([cache_idx] * pg), *([cache_kv.reshape(n_pool, PAGE_SIZE, kvw)] * pg))


def _nsa_compress_kernel(pt_ref, cmp_pos_ref, w1_ref, w2_ref, *rest):
    pg = SAMPLE_PAGES_PER_STEP
    page_refs = rest[:pg]
    kck_ref, kcv_ref, xs = rest[pg:]
    bpp = PAGE_SIZE // CMP_BLOCK
    nblk = pg * bpp
    for i in range(pg):
        for slab in range(C_KV * 2):
            xs[slab, i * PAGE_SIZE:(i + 1) * PAGE_SIZE, :] = page_refs[i][0, :, slab * C_DH:(slab + 1) * C_DH]
    for c, out_ref in enumerate((kck_ref, kcv_ref)):
        acc = jnp.zeros((C_KV * nblk, C_DH), jnp.float32)
        for pp in range(CMP_BLOCK // 2):
            cols = []
            for pos in (2 * pp, 2 * pp + 1):
                bias = cmp_pos_ref[pos:pos + 1, c * C_DH:(c + 1) * C_DH]
                cols.append(jnp.concatenate(
                    [xs.at[g * 2 + c][pl.ds(pos, nblk, stride=CMP_BLOCK), :] + bias
                     for g in range(C_KV)], axis=0))
            lhs = jnp.concatenate(cols, axis=-1).astype(jnp.bfloat16)
            acc = acc + jnp.dot(lhs, w1_ref[c, pp], preferred_element_type=jnp.float32)
        kc = jnp.dot(jax.nn.gelu(acc).astype(jnp.bfloat16), w2_ref[c], preferred_element_type=jnp.float32)
        for g in range(C_KV):
            out_ref[0, g] = kc[g * nblk:(g + 1) * nblk]


def nsa_compress(pages, page_table, cmp_pos, cmp_w1, cmp_w2):
    bn, n_pages = page_table.shape
    pg = SAMPLE_PAGES_PER_STEP
    assert n_pages % pg == 0
    n_pool = pages.shape[0]
    roww = C_KV * 2 * C_DH
    bpp = PAGE_SIZE // CMP_BLOCK
    nb = n_pages * bpp
    w1 = cmp_w1.reshape(2, CMP_BLOCK // 2, 2 * C_DH, C_DH).astype(jnp.bfloat16)
    w2 = cmp_w2.astype(jnp.bfloat16)

    def page_spec(i):
        return pl.BlockSpec((1, PAGE_SIZE, roww), lambda b, s, pt: (pt[b * n_pages + s * pg + i], 0, 0))

    const = lambda shape: pl.BlockSpec(shape, lambda b, s, pt: (0,) * len(shape))
    out_spec = pl.BlockSpec((1, C_KV, pg * bpp, C_DH), lambda b, s, pt: (b, 0, s, 0))
    grid_spec = pltpu.PrefetchScalarGridSpec(
        num_scalar_prefetch=1,
        grid=(bn, n_pages // pg),
        in_specs=[const((CMP_BLOCK, 2 * C_DH)), const(w1.shape), const(w2.shape)] + [page_spec(i) for i in range(pg)],
        out_specs=[out_spec, out_spec],
        scratch_shapes=[pltpu.VMEM((C_KV * 2, pg * PAGE_SIZE, C_DH), jnp.float32)],
    )
    shape = jax.ShapeDtypeStruct((bn, C_KV, nb, C_DH), jnp.float32)
    return pl.pallas_call(
        _nsa_compress_kernel,
        grid_spec=grid_spec,
        out_shape=[shape, shape],
        compiler_params=pltpu.CompilerParams(dimension_semantics=("arbitrary", "arbitrary"),
                                             vmem_limit_bytes=VMEM_LIMIT_BYTES),
        name="nsa_compress",
    )(page_table.reshape(-1), cmp_pos.reshape(CMP_BLOCK, 2 * C_DH), w1, w2,
      *([pages.reshape(n_pool, PAGE_SIZE, roww)] * pg))


def _nsa_sample_attend_kernel(qraw_ref, qrot_ref, kck_ref, kcv_ref, win_ref, kvw_ref, tri_ref,
                              ocmp_ref, owin_ref, sel_ref, key_s, *, past_len, n_sel):
    rep = C_REP
    scale = C_DH ** -0.5
    steps_q = qraw_ref.shape[1]
    rows = key_s.shape[1]
    nbp = key_s.shape[2]
    nb = kck_ref.shape[2]
    wbuf = win_ref.shape[1]
    qpos = past_len + lax.broadcasted_iota(jnp.int32, (steps_q, 1), 0)
    blk_c = lax.broadcasted_iota(jnp.int32, (steps_q, nb), 1)
    cmask = ((blk_c + 1) * CMP_BLOCK - 1) <= qpos
    any_c = jnp.where(jnp.max(jnp.where(cmask, 1.0, 0.0), axis=-1, keepdims=True) > 0.5, 1.0, 0.0)
    blk = lax.broadcasted_iota(jnp.int32, (steps_q, nbp), 1)
    cur = qpos // SEL_BLOCK
    forced = (blk == 0) | (blk == cur) | (blk == cur - 1)
    nbs = -(-(past_len + steps_q) // SEL_BLOCK)
    valid = (blk * SEL_BLOCK <= qpos) & (blk < nbs)
    lane = lax.broadcasted_iota(jnp.int32, (rows, nbp), 1)
    wlen = wbuf + V7X_SUBLANES
    wcol = lax.broadcasted_iota(jnp.int32, (steps_q, wlen), 1)
    wpos = jnp.where(wcol < wbuf, past_len - wbuf + wcol, past_len + wcol - wbuf)
    wmask = ((wpos >= 0) & (wpos <= qpos) & (wpos > qpos - WINDOW) & (wcol < wbuf + steps_q))[None]
    qraw = qraw_ref[0]
    qrot = qrot_ref[0]

    def stack(x, g):
        return jnp.concatenate([x[:, (g * rep + r) * C_DH:(g * rep + r + 1) * C_DH] for r in range(rep)], axis=0)

    pad_rows = lambda x: jnp.concatenate([x, jnp.zeros((rows - steps_q,) + x.shape[1:], x.dtype)], axis=0)

    for g in range(C_KV):
        s = lax.dot_general(stack(qraw, g), kck_ref[0, g], _NT, precision=lax.Precision.HIGHEST,
                            preferred_element_type=jnp.float32) * scale
        s = jnp.where(cmask[None], s.reshape(rep, steps_q, nb), NEG)
        e = jnp.exp(s - jnp.max(s, axis=-1, keepdims=True))
        p = e / jnp.sum(e, axis=-1, keepdims=True) * any_c[None]
        o_cmp = jnp.dot(p.reshape(rep * steps_q, nb), kcv_ref[0, g], precision=lax.Precision.HIGHEST,
                        preferred_element_type=jnp.float32)
        imp = p[0]
        for r in range(1, rep):
            imp = imp + p[r]
        imp = jnp.concatenate([imp, jnp.zeros((steps_q, nbp - nb), jnp.float32)], axis=-1)
        score = jnp.where(forced, FORCE, imp)
        key_s[0] = pad_rows(jnp.where(valid, _f32_order_key(score), INT_MIN))
        thr, cut = _topk_threshold(key_s, 1, n_sel, index_bits=int(math.ceil(math.log2(nbp))))
        mask = _selected(key_s[0], lane, thr, cut)
        prefix = jnp.dot(jnp.where(mask, 1.0, 0.0).astype(jnp.bfloat16), tri_ref[...],
                         preferred_element_type=jnp.float32)
        ids = [jnp.sum(jnp.where(mask & (prefix == float(t + 1)), lane, 0), axis=-1, keepdims=True)
               for t in range(n_sel)]
        ids.append(jnp.zeros((rows, V7X_LANES - n_sel), jnp.int32))
        sel_ref[0, g] = jnp.concatenate(ids, axis=-1)

        qg = stack(qrot, g).astype(jnp.bfloat16)
        newkv = kvw_ref[0]
        padn = jnp.zeros((V7X_SUBLANES - steps_q, C_DH), jnp.float32)
        kwin = jnp.concatenate([win_ref[0, :, (2 * g) * C_DH:(2 * g + 1) * C_DH],
                                newkv[:, (2 * g) * C_DH:(2 * g + 1) * C_DH], padn], axis=0).astype(jnp.bfloat16)
        vwin = jnp.concatenate([win_ref[0, :, (2 * g + 1) * C_DH:(2 * g + 2) * C_DH],
                                newkv[:, (2 * g + 1) * C_DH:(2 * g + 2) * C_DH], padn], axis=0).astype(jnp.bfloat16)
        sw = lax.dot_general(qg, kwin, _NT, preferred_element_type=jnp.float32) * scale
        sw = jnp.where(wmask, sw.reshape(rep, steps_q, wlen), NEG)
        ew = jnp.where(wmask, jnp.exp(sw - jnp.max(sw, axis=-1, keepdims=True)), 0.0)
        lw = jnp.sum(ew, axis=-1, keepdims=True)
        pw = ew / jnp.where(lw > 0.0, lw, 1.0)
        o_win = jnp.dot(pw.reshape(rep * steps_q, wlen).astype(jnp.bfloat16), vwin, preferred_element_type=jnp.float32)
        for r in range(rep):
            hh = g * rep + r
            ocmp_ref[0, :, hh * C_DH:(hh + 1) * C_DH] = o_cmp[r * steps_q:(r + 1) * steps_q]
            owin_ref[0, :, hh * C_DH:(hh + 1) * C_DH] = o_win[r * steps_q:(r + 1) * steps_q]


def nsa_sample_attend(q_raw, q_rot, kck, kcv, win, kv_w):
    bd, steps_q, qd = q_raw.shape
    nb = kck.shape[2]
    past_len = nb * CMP_BLOCK
    nbs = -(-(past_len + steps_q) // SEL_BLOCK)
    assert nbs >= N_SEL and steps_q <= V7X_SUBLANES
    nbp = -(-nbs // V7X_LANES) * V7X_LANES
    wbuf = win.shape[1]
    roww = C_KV * 2 * C_DH
    tri = (jnp.arange(nbp)[:, None] <= jnp.arange(nbp)[None, :]).astype(jnp.bfloat16)
    per_b = lambda shape: pl.BlockSpec((1,) + shape, lambda b: (b,) + (0,) * len(shape))
    return pl.pallas_call(
        functools.partial(_nsa_sample_attend_kernel, past_len=past_len, n_sel=N_SEL),
        grid=(bd,),
        in_specs=[per_b((steps_q, qd)), per_b((steps_q, qd)), per_b((C_KV, nb, C_DH)), per_b((C_KV, nb, C_DH)),
                  per_b((wbuf, roww)), per_b((steps_q, roww)), pl.BlockSpec((nbp, nbp), lambda b: (0, 0))],
        out_specs=[per_b((steps_q, qd)), per_b((steps_q, qd)), per_b((C_KV, V7X_SUBLANES, V7X_LANES))],
        out_shape=[jax.ShapeDtypeStruct((bd, steps_q, qd), jnp.float32),
                   jax.ShapeDtypeStruct((bd, steps_q, qd), jnp.float32),
                   jax.ShapeDtypeStruct((bd, C_KV, V7X_SUBLANES, V7X_LANES), jnp.int32)],
        scratch_shapes=[pltpu.VMEM((1, V7X_SUBLANES, nbp), jnp.int32)],
        compiler_params=pltpu.CompilerParams(dimension_semantics=("arbitrary",), vmem_limit_bytes=VMEM_LIMIT_BYTES),
        name="nsa_sample_attend",
    )(q_raw, q_rot, kck, kcv, win.reshape(bd, wbuf, roww), kv_w.reshape(bd, steps_q, roww), tri)


def _nsa_sample_selected_kernel(phys_ref, blk_ref, q_ref, gates_ref, ocmp_ref, owin_ref, new_ref, *rest,
                                past_len, steps_q):
    blk_refs = rest[:N_SEL]
    o_ref = rest[N_SEL]
    i = pl.program_id(0)
    qidx = (i // C_KV) % steps_q
    qpos = past_len + qidx
    scale = C_DH ** -0.5
    q = q_ref[0].astype(jnp.bfloat16)
    ks, vs, kpos = [], [], []
    row = lax.broadcasted_iota(jnp.int32, (1, SEL_BLOCK), 1)
    nb_past = past_len // SEL_BLOCK
    has_new = jnp.int32(0)
    for t in range(N_SEL):
        b_id = blk_ref[i * N_SEL + t]
        is_new = b_id >= nb_past
        has_new = jnp.maximum(has_new, is_new.astype(jnp.int32))
        ks.append(blk_refs[t][0, :, 0:C_DH])
        vs.append(blk_refs[t][0, :, C_DH:2 * C_DH])
        kpos.append(jnp.where(is_new, jnp.int32(INT_MAX), b_id * SEL_BLOCK + row))
    newkv = new_ref[0, 0]
    padn = jnp.zeros((V7X_SUBLANES - steps_q, C_DH), jnp.float32)
    ks.append(jnp.concatenate([newkv[:, 0:C_DH], padn], axis=0))
    vs.append(jnp.concatenate([newkv[:, C_DH:2 * C_DH], padn], axis=0))
    ncol = lax.broadcasted_iota(jnp.int32, (1, V7X_SUBLANES), 1)
    kpos.append(jnp.where((has_new > 0) & (ncol < steps_q), past_len + ncol, jnp.int32(INT_MAX)))
    k = jnp.concatenate(ks, axis=0).astype(jnp.bfloat16)
    v = jnp.concatenate(vs, axis=0).astype(jnp.bfloat16)
    mask = jnp.concatenate(kpos, axis=-1) <= qpos
    s = lax.dot_general(q, k, _NT, preferred_element_type=jnp.float32) * scale
    s = jnp.where(mask, s, NEG)
    e = jnp.where(mask, jnp.exp(s - jnp.max(s, axis=-1, keepdims=True)), 0.0)
    l = jnp.sum(e, axis=-1, keepdims=True)
    p = e / jnp.where(l > 0.0, l, 1.0)
    o_slc = jnp.dot(p.astype(jnp.bfloat16), v, preferred_element_type=jnp.float32)
    sig = jax.nn.sigmoid(gates_ref[0])
    o_ref[0] = sig[:, 0:1] * ocmp_ref[0] + sig[:, 1:2] * o_slc + sig[:, 2:3] * owin_ref[0]


def nsa_sample_selected(sel_idx, q_rot, gates, o_cmp, o_win, kv_s, cache_slc, page_table):
    bd, steps_q, qd = q_rot.shape
    n_pages = page_table.shape[1]
    past_len = n_pages * PAGE_SIZE
    n_pool = cache_slc.shape[0]
    bpp = PAGE_SIZE // SEL_BLOCK
    roww = C_KV * 2 * C_DH
    blk = jnp.transpose(sel_idx[:, :, :steps_q, :N_SEL], (0, 2, 1, 3))
    page = jnp.minimum(blk // bpp, n_pages - 1)
    phys = jnp.take_along_axis(page_table, page.reshape(bd, -1), axis=1).reshape(blk.shape) * bpp + blk % bpp
    n = bd * steps_q * C_KV
    heads = lambda a: a.reshape(n, C_REP, C_DH)
    new = jnp.transpose(kv_s.reshape(bd, steps_q, C_KV, 2 * C_DH), (0, 2, 1, 3))

    def blk_spec(t):
        return pl.BlockSpec((1, SEL_BLOCK, 2 * C_DH), lambda i, ph, bl: (ph[i * N_SEL + t], 0, i % C_KV))

    row_spec = lambda shape: pl.BlockSpec((1,) + shape, lambda i, ph, bl: (i,) + (0,) * len(shape))
    grid_spec = pltpu.PrefetchScalarGridSpec(
        num_scalar_prefetch=2,
        grid=(n,),
        in_specs=[row_spec((C_REP, C_DH)), row_spec((C_REP, 3)), row_spec((C_REP, C_DH)), row_spec((C_REP, C_DH)),
                  pl.BlockSpec((1, 1, steps_q, 2 * C_DH),
                               lambda i, ph, bl: (i // (steps_q * C_KV), i % C_KV, 0, 0))]
        + [blk_spec(t) for t in range(N_SEL)],
        out_specs=row_spec((C_REP, C_DH)),
    )
    o = pl.pallas_call(
        functools.partial(_nsa_sample_selected_kernel, past_len=past_len, steps_q=steps_q),
        grid_spec=grid_spec,
        out_shape=jax.ShapeDtypeStruct((n, C_REP, C_DH), jnp.float32),
        compiler_params=pltpu.CompilerParams(dimension_semantics=("arbitrary",), vmem_limit_bytes=VMEM_LIMIT_BYTES),
        name="nsa_sample_selected",
    )(phys.reshape(-1).astype(jnp.int32), blk.reshape(-1).astype(jnp.int32),
      heads(q_rot), gates.reshape(n, C_REP, 3), heads(o_cmp), heads(o_win), new,
      *([cache_slc.reshape(n_pool * bpp, SEL_BLOCK, roww)] * N_SEL))
    return o.reshape(bd, steps_q, qd)


def split_cols(h, sizes):
    out, start = [], 0
    for n in sizes:
        out.append(h[..., start:start + n])
        start += n
    return out


def layer_norm(x, g, b):
    mu = x.mean(-1, keepdims=True)
    var = jnp.square(x - mu).mean(-1, keepdims=True)
    return (x - mu) * lax.rsqrt(var + LN_EPS) * g + b


def rope_partial(x, pos, rot_dim):
    half = rot_dim // 2
    inv = ROPE_THETA ** (-jnp.arange(half, dtype=jnp.float32) / half)
    ang = pos.astype(jnp.float32)[:, None] * inv
    cos, sin = jnp.cos(ang)[:, None, :], jnp.sin(ang)[:, None, :]
    x1 = x[..., :half]
    x2 = x[..., half:rot_dim]
    rot = jnp.concatenate([x1 * cos - x2 * sin, x2 * cos + x1 * sin], -1)
    return jnp.concatenate([rot, x[..., rot_dim:]], -1)


def rope_kv(kv, pos, rot_dim):
    k = rope_partial(kv[..., 0, :], pos, rot_dim)
    return jnp.stack([k, kv[..., 1, :]], axis=-2)


def mixer_ab(x, pos, w_in, a_re, a_im, log_dt, b_re, b_im, c_re, c_im, d, w_glu, w_out,
             state=None, cache_kv=None, cache_idx=None, page_table=None):
    bn, s, _ = x.shape
    s5p = (a_re, a_im, log_dt, b_re, b_im, c_re, c_im, d, w_glu)
    u, q, kv, qi, ki, wi = split_cols(x @ w_in, AB_SIZES)
    q = rope_partial(q.reshape(bn, s, B_HEADS, B_DH), pos, B_ROT)
    kv = rope_kv(kv.reshape(bn, s, B_KV_HEADS, 2, B_DH), pos, B_ROT)
    qi = rope_partial(qi.reshape(bn, s, IDX_HEADS, IDX_DH), pos, IDX_ROT)
    ki = rope_partial(ki.reshape(bn, s, 1, IDX_DH), pos, IDX_ROT)[:, :, 0]
    if state is None:
        o_b = dsa_prompt(q.reshape(bn, s, B_HEADS * B_DH), kv, qi.reshape(bn, s, IDX_HEADS * IDX_DH), ki, wi)
        y_s5, new_a = s5_prompt(u, *s5p)
    else:
        o_b = dsa_sample(q.reshape(bn, s, B_HEADS * B_DH), kv, qi.reshape(bn, s, IDX_HEADS * IDX_DH), ki, wi,
                         cache_kv, cache_idx, page_table)
        y_s5, new_a = s5_sample(u, state, *s5p)
    out = jnp.concatenate([y_s5, o_b], -1) @ w_out
    return out, new_a, kv, ki


def mixer_c(x, pos, w_in, cmp_pos, cmp_w1, cmp_w2, w_out,
            cache_cmp=None, cache_slc=None, win=None, page_table=None):
    bn, s, _ = x.shape
    q, kv_c, kv_s, kv_w, gates = split_cols(x @ w_in, C_SIZES)
    q = q.reshape(bn, s, C_HEADS, C_DH)
    q_rot = rope_partial(q, pos, C_ROT)
    kv_c = kv_c.reshape(bn, s, C_KV, 2, C_DH)
    kv_s = rope_kv(kv_s.reshape(bn, s, C_KV, 2, C_DH), pos, C_ROT)
    kv_w = rope_kv(kv_w.reshape(bn, s, C_KV, 2, C_DH), pos, C_ROT)
    qd = C_HEADS * C_DH
    q_raw2, q_rot2 = q.reshape(bn, s, qd), q_rot.reshape(bn, s, qd)
    if cache_cmp is None:
        n_pages = s // PAGE_SIZE
        ident = jnp.arange(bn * n_pages, dtype=jnp.int32).reshape(bn, n_pages)
        kck, kcv = nsa_compress(kv_c.reshape(bn * n_pages, PAGE_SIZE, C_KV, 2, C_DH), ident, cmp_pos, cmp_w1, cmp_w2)
        o = nsa_prompt(q_raw2, q_rot2, gates, kck, kcv, kv_s, kv_w)
        return o @ w_out, kv_c, kv_s, kv_w[:, -min(WINDOW, s):]
    kck, kcv = nsa_compress(cache_cmp, page_table, cmp_pos, cmp_w1, cmp_w2)
    o_cmp, o_win, sel_idx = nsa_sample_attend(q_raw2, q_rot2, kck, kcv, win, kv_w)
    o = nsa_sample_selected(sel_idx, q_rot2, gates, o_cmp, o_win, kv_s, cache_slc, page_table)
    new_win = jnp.concatenate([win, kv_w], axis=1)[:, -win.shape[1]:]
    return o @ w_out, kv_c, kv_s, new_win


def kernel(x_prompt, x_sample, state_a, cache_b_kv, cache_b_idx, cache_c_cmp, cache_c_slc, state_c_win, page_table,
           w_in_ab, s5_a_re, s5_a_im, s5_log_dt, s5_b_re, s5_b_im, s5_c_re, s5_c_im, s5_d, s5_w_glu, w_out_ab,
           w_in_c, cmp_pos, cmp_w1, cmp_w2, w_out_c,
           ln1_g, ln1_b, ln2_g, ln2_b, moe_w_group, moe_w_expert, moe_w_gate, moe_w_up, moe_w_down):
    past_len = page_table.shape[1] * PAGE_SIZE
    pos_p = jnp.arange(x_prompt.shape[1])
    pos_s = past_len + jnp.arange(x_sample.shape[1])
    xp, xs = x_prompt, x_sample
    pshape, sshape = xp.shape, xs.shape
    n_p = pshape[0] * pshape[1]
    outs = {k: [] for k in ('a_p', 'a_s', 'bkv_p', 'bkv_s', 'bidx_p', 'bidx_s',
                            'ccmp_p', 'ccmp_s', 'cslc_p', 'cslc_s', 'cwin_p', 'cwin_s')}
    for i in range(DEPTH):
        j = i // 2
        if i % 2 == 0:
            ab = (w_in_ab[j], s5_a_re[j], s5_a_im[j], s5_log_dt[j], s5_b_re[j], s5_b_im[j],
                  s5_c_re[j], s5_c_im[j], s5_d[j], s5_w_glu[j], w_out_ab[j])
            mp, st_p, kv_p, ki_p = mixer_ab(xp, pos_p, *ab)
            ms, st_s, kv_n, ki_n = mixer_ab(xs, pos_s, *ab, state=state_a[j], cache_kv=cache_b_kv[j],
                                            cache_idx=cache_b_idx[j], page_table=page_table)
            outs['a_p'].append(st_p); outs['a_s'].append(st_s)
            outs['bkv_p'].append(kv_p); outs['bkv_s'].append(kv_n)
            outs['bidx_p'].append(ki_p); outs['bidx_s'].append(ki_n)
        else:
            cp = (w_in_c[j], cmp_pos[j], cmp_w1[j], cmp_w2[j], w_out_c[j])
            mp, kc_p, ksl_p, win_p = mixer_c(xp, pos_p, *cp)
            ms, kc_n, ksl_n, win_n = mixer_c(xs, pos_s, *cp, cache_cmp=cache_c_cmp[j], cache_slc=cache_c_slc[j],
                                             win=state_c_win[j], page_table=page_table)
            outs['ccmp_p'].append(kc_p); outs['ccmp_s'].append(kc_n)
            outs['cslc_p'].append(ksl_p); outs['cslc_s'].append(ksl_n)
            outs['cwin_p'].append(win_p); outs['cwin_s'].append(win_n)
        xp = layer_norm(ALPHA * xp + mp, ln1_g[i], ln1_b[i])
        xs = layer_norm(ALPHA * xs + ms, ln1_g[i], ln1_b[i])
        moe = (moe_w_group[i], moe_w_expert[i], moe_w_gate[i], moe_w_up[i], moe_w_down[i])
        yp, ys = hier_moe([xp.reshape(n_p, D_MODEL), xs.reshape(-1, D_MODEL)], *moe)
        xp = layer_norm(ALPHA * xp + yp.reshape(pshape), ln2_g[i], ln2_b[i])
        xs = layer_norm(ALPHA * xs + ys.reshape(sshape), ln2_g[i], ln2_b[i])
    st = lambda k: jnp.stack(outs[k])
    return (xp, xs, st('a_p'), st('a_s'), st('bkv_p'), st('bkv_s'), st('bidx_p'), st('bidx_s'),
            st('ccmp_p'), st('ccmp_s'), st('cslc_p'), st('cslc_s'), st('cwin_p'), st('cwin_s'))
```

```python
import functools
import math

import jax
import jax.numpy as jnp
import numpy as np
from jax import lax
from jax.experimental import pallas as pl
from jax.experimental.pallas import tpu as pltpu

D_MODEL = 1024
DEPTH = 2
PAGE_SIZE = 128

S5_WIDTH = D_MODEL // 2
S5_GROUP = 16
S5_GROUPS = S5_WIDTH // S5_GROUP
S5_STATE = 64
B_HEADS = 8
B_DH = 64
B_KV_HEADS = 2
B_ROT = B_DH // 4
IDX_HEADS = 4
IDX_DH = 64
IDX_ROT = IDX_DH // 4
DSA_TOPK = 256
C_HEADS = 8
C_DH = 128
C_KV = 2
C_REP = C_HEADS // C_KV
C_ROT = C_DH // 4
CMP_BLOCK = 64
SEL_BLOCK = 64
N_SEL = 16
WINDOW = 512
N_EGROUPS = 4
E_PER_GROUP = 8
N_EXPERTS = N_EGROUPS * E_PER_GROUP
TOP_E = 2
D_FF_E = 512

ROPE_THETA = 500000.0
ALPHA = (2 * DEPTH) ** 0.25
Q_BLOCK = 128
LN_EPS = 1e-5
NEG = -1e30
FORCE = 1e4
LOG2E = math.log2(math.e)

AB_SIZES = (S5_WIDTH, B_HEADS * B_DH, B_KV_HEADS * 2 * B_DH, IDX_HEADS * IDX_DH, IDX_DH, IDX_HEADS)
C_SIZES = (C_HEADS * C_DH, C_KV * 2 * C_DH, C_KV * 2 * C_DH, C_KV * 2 * C_DH, C_HEADS * 3)

V7X_LANES = 128
V7X_SUBLANES = 8
V7X_VMEM_BYTES = 64 * 1024 * 1024
VMEM_LIMIT_BYTES = 48 * 1024 * 1024

MOE_ROUTER_TILE = 512
MOE_EXPERT_TILE = 256


def _moe_router_kernel(x_ref, w_ref, o_ref):
    logits = jnp.dot(x_ref[...], w_ref[...], precision=lax.Precision.HIGHEST,
                     preferred_element_type=jnp.float32)
    col = lax.broadcasted_iota(jnp.int32, logits.shape, 1)
    big = jnp.int32(1 << 20)
    ninf = jnp.float32(-jnp.inf)
    gl = jnp.where(col < N_EGROUPS, logits, ninf)
    gmax = jnp.max(gl, axis=-1, keepdims=True)
    g_top = jnp.min(jnp.where(gl == gmax, col, big), axis=-1, keepdims=True)
    g_w = 1.0 / jnp.sum(jnp.exp(gl - gmax), axis=-1, keepdims=True)
    lo = N_EGROUPS + E_PER_GROUP * g_top
    el = jnp.where((col >= lo) & (col < lo + E_PER_GROUP), logits, ninf)
    v1 = jnp.max(el, axis=-1, keepdims=True)
    i1 = jnp.min(jnp.where(el == v1, col, big), axis=-1, keepdims=True)
    el2 = jnp.where(col == i1, ninf, el)
    v2 = jnp.max(el2, axis=-1, keepdims=True)
    i2 = jnp.min(jnp.where(el2 == v2, col, big), axis=-1, keepdims=True)
    e2 = jnp.exp(v2 - v1)
    den = 1.0 + e2
    w1 = g_w / den
    w2 = g_w * e2 / den
    out = jnp.where(col == 0, (i1 - N_EGROUPS).astype(jnp.float32),
                    jnp.where(col == 1, (i2 - N_EGROUPS).astype(jnp.float32),
                              jnp.where(col == 2, w1, jnp.where(col == 3, w2, 0.0))))
    o_ref[...] = out


def _moe_route(x, w_router):
    t = x.shape[0]
    tm = min(MOE_ROUTER_TILE, t)
    assert t % tm == 0
    return pl.pallas_call(
        _moe_router_kernel,
        grid=(t // tm,),
        in_specs=[pl.BlockSpec((tm, D_MODEL), lambda i: (i, 0)),
                  pl.BlockSpec((D_MODEL, V7X_LANES), lambda i: (0, 0))],
        out_specs=pl.BlockSpec((tm, V7X_LANES), lambda i: (i, 0)),
        out_shape=jax.ShapeDtypeStruct((t, V7X_LANES), jnp.float32),
        name="moe_router",
    )(x, w_router)


def _moe_expert_kernel(tile_e_ref, nvalid_ref, xs_ref, wg_ref, wu_ref, wd_ref, sw_ref, o_ref,
                       wg_s, wu_s, wd_s):
    i = pl.program_id(0)
    valid = i < nvalid_ref[0]
    prev_e = tile_e_ref[jnp.maximum(i - 1, 0)]
    new_expert = jnp.logical_or(i == 0, tile_e_ref[i] != prev_e)

    @pl.when(jnp.logical_and(valid, new_expert))
    def _():
        wg_s[...] = wg_ref[0].astype(jnp.bfloat16)
        wu_s[...] = wu_ref[0].astype(jnp.bfloat16)
        wd_s[...] = wd_ref[0].astype(jnp.bfloat16)

    @pl.when(valid)
    def _():
        x = xs_ref[...]
        g = jnp.dot(x, wg_s[...], preferred_element_type=jnp.float32)
        u = jnp.dot(x, wu_s[...], preferred_element_type=jnp.float32)
        h = (g * jax.nn.sigmoid(g)) * u
        y = jnp.dot(h.astype(jnp.bfloat16), wd_s[...], preferred_element_type=jnp.float32)
        o_ref[...] = y * sw_ref[...]

    @pl.when(jnp.logical_not(valid))
    def _():
        o_ref[...] = jnp.zeros_like(o_ref)


def _moe_experts(tile_e, nvalid, xs, slot_w, w_gate, w_up, w_down):
    np_rows = xs.shape[0]
    tm = MOE_EXPERT_TILE
    n_tiles = np_rows // tm
    grid_spec = pltpu.PrefetchScalarGridSpec(
        num_scalar_prefetch=2,
        grid=(n_tiles,),
        in_specs=[
            pl.BlockSpec((tm, D_MODEL), lambda i, te, nv: (i, 0)),
            pl.BlockSpec((1, D_MODEL, D_FF_E), lambda i, te, nv: (te[i], 0, 0)),
            pl.BlockSpec((1, D_MODEL, D_FF_E), lambda i, te, nv: (te[i], 0, 0)),
            pl.BlockSpec((1, D_FF_E, D_MODEL), lambda i, te, nv: (te[i], 0, 0)),
            pl.BlockSpec((tm, 1), lambda i, te, nv: (i, 0)),
        ],
        out_specs=pl.BlockSpec((tm, D_MODEL), lambda i, te, nv: (i, 0)),
        scratch_shapes=[pltpu.VMEM((D_MODEL, D_FF_E), jnp.bfloat16),
                        pltpu.VMEM((D_MODEL, D_FF_E), jnp.bfloat16),
                        pltpu.VMEM((D_FF_E, D_MODEL), jnp.bfloat16)],
    )
    return pl.pallas_call(
        _moe_expert_kernel,
        grid_spec=grid_spec,
        out_shape=jax.ShapeDtypeStruct((np_rows, D_MODEL), jnp.float32),
        compiler_params=pltpu.CompilerParams(dimension_semantics=("arbitrary",),
                                             vmem_limit_bytes=VMEM_LIMIT_BYTES),
        name="moe_experts",
    )(tile_e, nvalid, xs, w_gate, w_up, w_down, slot_w)


def _moe_padded_rows(t):
    tm = MOE_EXPERT_TILE
    slots = t * TOP_E
    return ((slots + N_EXPERTS * (tm - 1)) // tm + 1) * tm


def hier_moe(xs_list, w_group, w_expert, w_gate, w_up, w_down):
    tm = MOE_EXPERT_TILE
    w_router = jnp.zeros((D_MODEL, V7X_LANES), jnp.float32)
    w_router = w_router.at[:, :N_EGROUPS].set(w_group).at[:, N_EGROUPS:N_EGROUPS + N_EXPERTS].set(w_expert)
    routed = jnp.concatenate([_moe_route(x, w_router) for x in xs_list], axis=0)
    t = routed.shape[0]
    eid = routed[:, 0:2].astype(jnp.int32).reshape(-1)
    ew = routed[:, 2:4].reshape(-1)
    n_slots = t * TOP_E
    np_rows = _moe_padded_rows(t)

    experts = jnp.arange(N_EXPERTS, dtype=jnp.int32)
    counts = jnp.sum((eid[:, None] == experts[None, :]).astype(jnp.int32), axis=0)
    order = jnp.argsort(eid, stable=True).astype(jnp.int32)
    rank = jnp.argsort(order).astype(jnp.int32)
    padded = ((counts + tm - 1) // tm) * tm
    pad_end = jnp.cumsum(padded)
    pad_off = pad_end - padded
    off = jnp.cumsum(counts) - counts
    pos = pad_off[eid] + rank - off[eid]
    tile_start = jnp.arange(np_rows // tm, dtype=jnp.int32) * tm
    tile_e = jnp.minimum(jnp.sum((pad_end[None, :] <= tile_start[:, None]).astype(jnp.int32), axis=1), N_EXPERTS - 1)
    nvalid = (pad_end[-1] // tm).astype(jnp.int32).reshape(1)
    row = jnp.arange(np_rows, dtype=jnp.int32)
    row_e = tile_e[row // tm]
    local = row - pad_off[row_e]
    row_ok = local < counts[row_e]
    slot = order[jnp.clip(off[row_e] + local, 0, n_slots - 1)]
    src_tok = jnp.where(row_ok, slot // TOP_E, 0)
    slot_w = jnp.where(row_ok, ew[slot], 0.0)

    x_all = jnp.concatenate([x.astype(jnp.bfloat16) for x in xs_list], axis=0)
    ys = _moe_experts(tile_e, nvalid, x_all[src_tok], slot_w[:, None], w_gate, w_up, w_down)
    pos2 = pos.reshape(t, TOP_E)
    y = ys[pos2[:, 0]] + ys[pos2[:, 1]]
    out, start = [], 0
    for x in xs_list:
        out.append(y[start:start + x.shape[0]])
        start += x.shape[0]
    return out


INT_MIN = -(2 ** 31)
INT_MAX = 2 ** 31 - 1


def _f32_order_key(x):
    b = lax.bitcast_convert_type(x + 0.0, jnp.int32)
    return jnp.where(b >= 0, b, b ^ jnp.int32(INT_MAX))


def _count_rows(key_ref, n_chunks, pred_fns):
    _, rows, w = key_ref.shape

    def body(c, accs):
        blk = key_ref[c]
        idx = c * w + lax.broadcasted_iota(jnp.int32, (rows, w), 1)
        out = []
        for fn, acc in zip(pred_fns, accs):
            m = jnp.where(fn(blk, idx), 1, 0)
            part = m[:, 0:V7X_LANES]
            for s in range(1, w // V7X_LANES):
                part = part + m[:, s * V7X_LANES:(s + 1) * V7X_LANES]
            out.append(acc + part)
        return tuple(out)

    init = tuple(jnp.zeros((rows, V7X_LANES), jnp.int32) for _ in pred_fns)
    accs = lax.fori_loop(0, n_chunks, body, init)
    return [jnp.sum(a, axis=-1, keepdims=True) for a in accs]


def _any_row(mask):
    return jnp.max(jnp.where(mask, 1.0, 0.0)) > 0.5


def _topk_threshold(key_ref, n_chunks, k, index_bits):
    _, rows, w = key_ref.shape
    col = lambda v: jnp.full((rows, 1), v, jnp.int32)

    def v_step(st):
        lo, hi, exact = st
        mid = (lo & hi) + ((lo ^ hi) >> 1)
        cnt, = _count_rows(key_ref, n_chunks, [lambda blk, idx: blk > mid])
        active = lo < hi
        hit = active & (cnt == k)
        less = active & (cnt < k)
        more = active & (cnt > k)
        hi = jnp.where(hit | less, mid, hi)
        lo = jnp.where(hit, mid, jnp.where(more, mid + 1, lo))
        exact = jnp.where(hit, 1, exact)
        return lo, hi, exact

    def v_body(st):
        lo, hi, exact = v_step(v_step(st[:3]))
        return lo, hi, exact, _any_row(lo < hi)

    thr, _, exact, _ = lax.while_loop(lambda st: st[3], v_body,
                                      (col(INT_MIN), col(INT_MAX), col(0), jnp.bool_(True)))
    c_gt, c_eq = _count_rows(key_ref, n_chunks, [lambda blk, idx: blk > thr, lambda blk, idx: blk == thr])
    r = k - c_gt
    take_all_ties = (exact == 0) & (thr != INT_MIN)
    need = take_all_ties & (c_eq > r)
    cut_default = jnp.where(take_all_ties, INT_MAX, -1)

    def tie_phase():
        def body(_, st):
            lo_i, hi_i = st
            mid = (lo_i + hi_i) >> 1
            g, = _count_rows(key_ref, n_chunks, [lambda blk, idx: (blk == thr) & (idx <= mid)])
            ok = g >= r
            return jnp.where(ok, lo_i, mid + 1), jnp.where(ok, mid, hi_i)
        lo_i, _ = lax.fori_loop(0, index_bits, body, (col(0), col(0) + (n_chunks * w - 1)))
        return jnp.where(need, lo_i, cut_default)

    cut = lax.cond(_any_row(need), tie_phase, lambda: cut_default)
    return thr, cut


def _selected(key, idx, thr, cut):
    return (key > thr) | ((key == thr) & (idx <= cut))


DSA_CHUNK = 1024


def _split3_lhs(x):
    hi = x.astype(jnp.bfloat16)
    lo = (x - hi.astype(jnp.float32)).astype(jnp.bfloat16)
    return jnp.concatenate([hi, hi, lo], axis=-1)


def _split3_rhs(x):
    hi = x.astype(jnp.bfloat16)
    lo = (x - hi.astype(jnp.float32)).astype(jnp.bfloat16)
    return jnp.concatenate([hi, lo, hi], axis=-1)


_NT = (((1,), (1,)), ((), ()))


def _dsa_prompt_kernel(qi_ref, wi_ref, kcat_ref, q_ref, k_ref, v_ref, o_ref, key_s, *, topk):
    cw = DSA_CHUNK
    rep = B_HEADS // B_KV_HEADS
    j = pl.program_id(1)
    t0 = j * Q_BLOCK
    n_chunks = (t0 + Q_BLOCK + cw - 1) // cw
    qpos = t0 + lax.broadcasted_iota(jnp.int32, (Q_BLOCK, 1), 0)

    qi = qi_ref[0]
    wi = wi_ref[0]
    qcat = jnp.concatenate([_split3_lhs(qi[:, h * IDX_DH:(h + 1) * IDX_DH]) for h in range(IDX_HEADS)], axis=0)

    def score_chunk(c, carry):
        start = pl.multiple_of(c * cw, cw)
        kc = kcat_ref[0, pl.ds(start, cw), :]
        s = lax.dot_general(qcat, kc, _NT, preferred_element_type=jnp.float32) * (IDX_DH ** -0.5)
        acc = None
        for h in range(IDX_HEADS):
            term = wi[:, h:h + 1] * jnp.maximum(s[h * Q_BLOCK:(h + 1) * Q_BLOCK], 0.0)
            acc = term if acc is None else acc + term
        kpos = start + lax.broadcasted_iota(jnp.int32, (Q_BLOCK, cw), 1)
        key_s[c] = jnp.where(kpos <= qpos, _f32_order_key(acc), INT_MIN)
        return carry

    lax.fori_loop(0, n_chunks, score_chunk, 0)
    thr, cut = _topk_threshold(key_s, n_chunks, topk, index_bits=int(math.log2(key_s.shape[0] * cw)))

    q = q_ref[0]
    qg = [jnp.concatenate([q[:, (g * rep + r) * B_DH:(g * rep + r + 1) * B_DH] for r in range(rep)],
                          axis=0).astype(jnp.bfloat16) for g in range(B_KV_HEADS)]

    def att_chunk(c, carry):
        start = pl.multiple_of(c * cw, cw)
        kidx = start + lax.broadcasted_iota(jnp.int32, (Q_BLOCK, cw), 1)
        bias = jnp.where(_selected(key_s[c], kidx, thr, cut), 0.0, NEG)[None]
        out = []
        for g in range(B_KV_HEADS):
            m, l, acc = carry[g]
            kg = k_ref[0, g, pl.ds(start, cw), :]
            vg = v_ref[0, g, pl.ds(start, cw), :]
            s = lax.dot_general(qg[g], kg, _NT, preferred_element_type=jnp.float32) * (B_DH ** -0.5 * LOG2E)
            s = s.reshape(rep, Q_BLOCK, cw) + bias
            m_new = jnp.maximum(m, jnp.max(s, axis=-1, keepdims=True))
            p = jnp.exp2(s - m_new)
            alpha = jnp.exp2(m - m_new)
            l = alpha * l + jnp.sum(p, axis=-1, keepdims=True)
            pv = jnp.dot(p.reshape(rep * Q_BLOCK, cw).astype(jnp.bfloat16), vg,
                         preferred_element_type=jnp.float32)
            acc = alpha * acc + pv.reshape(rep, Q_BLOCK, B_DH)
            out.append((m_new, l, acc))
        return tuple(out)

    init = tuple((jnp.full((rep, Q_BLOCK, 1), NEG, jnp.float32),
                  jnp.zeros((rep, Q_BLOCK, 1), jnp.float32),
                  jnp.zeros((rep, Q_BLOCK, B_DH), jnp.float32)) for _ in range(B_KV_HEADS))
    res = lax.fori_loop(0, n_chunks, att_chunk, init)
    heads = []
    for g in range(B_KV_HEADS):
        _, l, acc = res[g]
        o = jnp.where(l > 0.0, acc / jnp.where(l > 0.0, l, 1.0), 0.0)
        heads += [o[r] for r in range(rep)]
    o_ref[0] = jnp.concatenate(heads, axis=-1)


def dsa_prompt(q, kv, qi, ki, wi):
    bn, s, _ = q.shape
    topk = min(DSA_TOPK, s // 4)
    cw = DSA_CHUNK
    assert s % cw == 0 and s % Q_BLOCK == 0
    kcat = _split3_rhs(ki)
    k = jnp.transpose(kv[:, :, :, 0, :], (0, 2, 1, 3)).astype(jnp.bfloat16)
    v = jnp.transpose(kv[:, :, :, 1, :], (0, 2, 1, 3)).astype(jnp.bfloat16)
    qw = B_HEADS * B_DH
    return pl.pallas_call(
        functools.partial(_dsa_prompt_kernel, topk=topk),
        grid=(bn, s // Q_BLOCK),
        in_specs=[
            pl.BlockSpec((1, Q_BLOCK, IDX_HEADS * IDX_DH), lambda b, j: (b, j, 0)),
            pl.BlockSpec((1, Q_BLOCK, IDX_HEADS), lambda b, j: (b, j, 0)),
            pl.BlockSpec((1, s, 3 * IDX_DH), lambda b, j: (b, 0, 0)),
            pl.BlockSpec((1, Q_BLOCK, qw), lambda b, j: (b, j, 0)),
            pl.BlockSpec((1, B_KV_HEADS, s, B_DH), lambda b, j: (b, 0, 0, 0)),
            pl.BlockSpec((1, B_KV_HEADS, s, B_DH), lambda b, j: (b, 0, 0, 0)),
        ],
        out_specs=pl.BlockSpec((1, Q_BLOCK, qw), lambda b, j: (b, j, 0)),
        out_shape=jax.ShapeDtypeStruct((bn, s, qw), jnp.float32),
        scratch_shapes=[pltpu.VMEM((s // cw, Q_BLOCK, cw), jnp.int32)],
        compiler_params=pltpu.CompilerParams(dimension_semantics=("arbitrary", "arbitrary"),
                                             vmem_limit_bytes=VMEM_LIMIT_BYTES),
        name="dsa_prompt",
    )(qi, wi, kcat, q, k, v)


NSA_CHUNK = 1024
NSA_WIN_BLOCKS = WINDOW // Q_BLOCK + 1


def _nsa_prompt_kernel(qraw_ref, qrot_ref, gates_ref, kck_ref, kcv_ref, e_ref, ks_ref, vs_ref, *rest, n_sel):
    kw_refs = rest[0:NSA_WIN_BLOCKS]
    vw_refs = rest[NSA_WIN_BLOCKS:2 * NSA_WIN_BLOCKS]
    o_ref, key_s = rest[2 * NSA_WIN_BLOCKS:]
    cw = NSA_CHUNK
    rep = C_REP
    scale = C_DH ** -0.5
    j = pl.program_id(1)
    t0 = j * Q_BLOCK
    n_chunks = (t0 + Q_BLOCK + cw - 1) // cw
    qpos = t0 + lax.broadcasted_iota(jnp.int32, (Q_BLOCK, 1), 0)
    nbp = kck_ref.shape[2]
    blk = lax.broadcasted_iota(jnp.int32, (Q_BLOCK, nbp), 1)
    cmask = ((blk + 1) * CMP_BLOCK - 1) <= qpos
    cur = qpos // SEL_BLOCK
    forced = (blk == 0) | (blk == cur) | (blk == cur - 1)
    valid = blk * SEL_BLOCK <= qpos
    wlen = NSA_WIN_BLOCKS * Q_BLOCK
    wpos = t0 - WINDOW + lax.broadcasted_iota(jnp.int32, (Q_BLOCK, wlen), 1)
    wmask = ((wpos >= 0) & (wpos <= qpos) & (wpos > qpos - WINDOW))[None]

    qraw = qraw_ref[0]
    qrot = qrot_ref[0]
    sig = jax.nn.sigmoid(gates_ref[0])

    def stack(x, g):
        return jnp.concatenate([x[:, (g * rep + r) * C_DH:(g * rep + r + 1) * C_DH] for r in range(rep)], axis=0)

    for g in range(C_KV):
        s = lax.dot_general(stack(qraw, g), kck_ref[0, g], _NT, precision=lax.Precision.HIGHEST,
                            preferred_element_type=jnp.float32) * scale
        s = jnp.where(cmask[None], s.reshape(rep, Q_BLOCK, nbp), NEG)
        e = jnp.exp(s - jnp.max(s, axis=-1, keepdims=True))
        p = e / jnp.sum(e, axis=-1, keepdims=True)
        p = p * jnp.where(jnp.max(jnp.where(cmask, 1.0, 0.0), axis=-1, keepdims=True) > 0.5, 1.0, 0.0)[None]
        o_cmp = jnp.dot(p.reshape(rep * Q_BLOCK, nbp), kcv_ref[0, g], precision=lax.Precision.HIGHEST,
                        preferred_element_type=jnp.float32).reshape(rep, Q_BLOCK, C_DH)
        imp = p[0]
        for r in range(1, rep):
            imp = imp + p[r]
        score = jnp.where(forced, FORCE, imp)
        key_s[0] = jnp.where(valid, _f32_order_key(score), INT_MIN)
        thr, cut = _topk_threshold(key_s, 1, n_sel, index_bits=int(math.log2(nbp)))
        selblk = jnp.where(_selected(key_s[0], blk, thr, cut), 1.0, 0.0).astype(jnp.bfloat16)

        qg = stack(qrot, g).astype(jnp.bfloat16)

        def att_chunk(c, carry):
            m, l, acc = carry
            start = pl.multiple_of(c * cw, cw)
            kpos = start + lax.broadcasted_iota(jnp.int32, (Q_BLOCK, cw), 1)
            hit = jnp.dot(selblk, e_ref[c], preferred_element_type=jnp.float32)
            bias = jnp.where((hit > 0.5) & (kpos <= qpos), 0.0, NEG)[None]
            kg = ks_ref[0, g, pl.ds(start, cw), :]
            vg = vs_ref[0, g, pl.ds(start, cw), :]
            sc = lax.dot_general(qg, kg, _NT, preferred_element_type=jnp.float32) * (scale * LOG2E)
            sc = sc.reshape(rep, Q_BLOCK, cw) + bias
            m_new = jnp.maximum(m, jnp.max(sc, axis=-1, keepdims=True))
            pp = jnp.exp2(sc - m_new)
            alpha = jnp.exp2(m - m_new)
            l = alpha * l + jnp.sum(pp, axis=-1, keepdims=True)
            pv = jnp.dot(pp.reshape(rep * Q_BLOCK, cw).astype(jnp.bfloat16), vg, preferred_element_type=jnp.float32)
            return m_new, l, alpha * acc + pv.reshape(rep, Q_BLOCK, C_DH)

        init = (jnp.full((rep, Q_BLOCK, 1), NEG, jnp.float32), jnp.zeros((rep, Q_BLOCK, 1), jnp.float32),
                jnp.zeros((rep, Q_BLOCK, C_DH), jnp.float32))
        _, l, acc = lax.fori_loop(0, n_chunks, att_chunk, init)
        o_slc = jnp.where(l > 0.0, acc / jnp.where(l > 0.0, l, 1.0), 0.0)

        kwin = jnp.concatenate([r_[0, g] for r_ in kw_refs], axis=0)
        vwin = jnp.concatenate([r_[0, g] for r_ in vw_refs], axis=0)
        sw = lax.dot_general(qg, kwin, _NT, preferred_element_type=jnp.float32) * scale
        sw = jnp.where(wmask, sw.reshape(rep, Q_BLOCK, wlen), NEG)
        ew = jnp.where(wmask, jnp.exp(sw - jnp.max(sw, axis=-1, keepdims=True)), 0.0)
        lw = jnp.sum(ew, axis=-1, keepdims=True)
        pw = ew / jnp.where(lw > 0.0, lw, 1.0)
        o_win = jnp.dot(pw.reshape(rep * Q_BLOCK, wlen).astype(jnp.bfloat16), vwin,
                        preferred_element_type=jnp.float32).reshape(rep, Q_BLOCK, C_DH)

        for r in range(rep):
            hh = g * rep + r
            o = (sig[:, 3 * hh:3 * hh + 1] * o_cmp[r] + sig[:, 3 * hh + 1:3 * hh + 2] * o_slc[r]
                 + sig[:, 3 * hh + 2:3 * hh + 3] * o_win[r])
            o_ref[0, :, hh * C_DH:(hh + 1) * C_DH] = o


def nsa_prompt(q_raw, q_rot, gates, kck, kcv, kv_s, kv_w):
    bn, s, _ = q_raw.shape
    cw = NSA_CHUNK
    assert s % cw == 0
    nb = kck.shape[2]
    nbs = -(-s // SEL_BLOCK)
    assert nb == nbs
    nbp = -(-nb // V7X_LANES) * V7X_LANES
    n_sel = min(N_SEL, nbs)
    kck = jnp.pad(kck, ((0, 0), (0, 0), (0, nbp - nb), (0, 0)))
    kcv = jnp.pad(kcv, ((0, 0), (0, 0), (0, nbp - nb), (0, 0)))
    split = lambda kv, c: jnp.transpose(kv[:, :, :, c, :], (0, 2, 1, 3)).astype(jnp.bfloat16)
    ks, vs, kw, vw = split(kv_s, 0), split(kv_s, 1), split(kv_w, 0), split(kv_w, 1)
    expand = (jnp.arange(s, dtype=jnp.int32)[None, :] // SEL_BLOCK == jnp.arange(nbp, dtype=jnp.int32)[:, None])
    expand = jnp.transpose(expand.astype(jnp.bfloat16).reshape(nbp, s // cw, cw), (1, 0, 2))
    qd = C_HEADS * C_DH
    nq = s // Q_BLOCK
    wb = NSA_WIN_BLOCKS

    def win_spec(slot):
        return pl.BlockSpec((1, C_KV, Q_BLOCK, C_DH),
                            lambda b, j: (b, 0, jnp.maximum(j - (wb - 1) + slot, 0), 0))

    return pl.pallas_call(
        functools.partial(_nsa_prompt_kernel, n_sel=n_sel),
        grid=(bn, nq),
        in_specs=[
            pl.BlockSpec((1, Q_BLOCK, qd), lambda b, j: (b, j, 0)),
            pl.BlockSpec((1, Q_BLOCK, qd), lambda b, j: (b, j, 0)),
            pl.BlockSpec((1, Q_BLOCK, C_HEADS * 3), lambda b, j: (b, j, 0)),
            pl.BlockSpec((1, C_KV, nbp, C_DH), lambda b, j: (b, 0, 0, 0)),
            pl.BlockSpec((1, C_KV, nbp, C_DH), lambda b, j: (b, 0, 0, 0)),
            pl.BlockSpec((s // cw, nbp, cw), lambda b, j: (0, 0, 0)),
            pl.BlockSpec((1, C_KV, s, C_DH), lambda b, j: (b, 0, 0, 0)),
            pl.BlockSpec((1, C_KV, s, C_DH), lambda b, j: (b, 0, 0, 0)),
        ] + [win_spec(i) for i in range(wb)] + [win_spec(i) for i in range(wb)],
        out_specs=pl.BlockSpec((1, Q_BLOCK, qd), lambda b, j: (b, j, 0)),
        out_shape=jax.ShapeDtypeStruct((bn, s, qd), jnp.float32),
        scratch_shapes=[pltpu.VMEM((1, Q_BLOCK, nbp), jnp.int32)],
        compiler_params=pltpu.CompilerParams(dimension_semantics=("arbitrary", "arbitrary"),
                                             vmem_limit_bytes=VMEM_LIMIT_BYTES),
        name="nsa_prompt",
    )(q_raw, q_rot, gates, kck, kcv, expand, ks, vs, *([kw] * wb), *([vw] * wb))


S5_N = S5_GROUPS * S5_STATE
S5_TILES = 4
S5_TILE_IN = S5_WIDTH // S5_TILES
S5_TILE_N = S5_N // S5_TILES
S5_CHUNK = 256
S5_SUB = S5_CHUNK // V7X_SUBLANES


def _s5_input_map(u, bcat_ref, store):
    for i in range(S5_TILES):
        ucat = _split3_lhs(u[:, i * S5_TILE_IN:(i + 1) * S5_TILE_IN])
        r = jnp.dot(ucat, bcat_ref[i], preferred_element_type=jnp.float32)
        store(i, r[:, :S5_TILE_N], r[:, S5_TILE_N:])


def _s5_output_map(h_tile, u, ccat_ref, d_ref, wglu_ref):
    ys = []
    for i in range(S5_TILES):
        re, im = h_tile(i)
        hcat = jnp.concatenate([re, im], axis=-1).astype(jnp.bfloat16)
        ys.append(jnp.dot(hcat, ccat_ref[i], preferred_element_type=jnp.float32))
    y = jax.nn.gelu(jnp.concatenate(ys, axis=-1) + d_ref[...] * u)
    z = jnp.dot(y.astype(jnp.bfloat16), wglu_ref[...], preferred_element_type=jnp.float32)
    return y * jax.nn.sigmoid(z)


def _cmul_add(a_re, a_im, x_re, x_im, b_re, b_im):
    return a_re * x_re - a_im * x_im + b_re, a_re * x_im + a_im * x_re + b_im


def _s5_prompt_kernel(u_ref, h0_ref, bcat_ref, lam8_ref, lamm_ref, ccat_ref, d_ref, wglu_ref,
                      y_ref, hlast_ref, hs, ein_s, carry_s):
    n = S5_N
    m = S5_SUB
    c = pl.program_id(1)

    @pl.when(c == 0)
    def _():
        carry_s[...] = h0_ref[0]

    u = u_ref[0]

    def store_bu(i, re, im):
        hs[:, i * S5_TILE_N:(i + 1) * S5_TILE_N] = re
        hs[:, n + i * S5_TILE_N:n + (i + 1) * S5_TILE_N] = im

    _s5_input_map(u, bcat_ref, store_bu)

    def scan_body(k, h):
        row = pl.multiple_of(k * V7X_SUBLANES, V7X_SUBLANES)
        rows = pl.ds(row, V7X_SUBLANES)
        n_re, n_im = _cmul_add(lam8_ref[:, :n], lam8_ref[:, n:], h[0], h[1], hs[rows, :n], hs[rows, n:])
        hs[rows, :n] = n_re
        hs[rows, n:] = n_im
        return n_re, n_im

    zero = jnp.zeros((V7X_SUBLANES, n), jnp.float32)
    ends = lax.fori_loop(0, m, scan_body, (zero, zero))

    lm_re, lm_im = lamm_ref[:, :n], lamm_ref[:, n:]
    e_re, e_im = carry_s[:, :n], carry_s[:, n:]
    for s in range(V7X_SUBLANES):
        ein_s[s:s + 1, :n] = e_re
        ein_s[s:s + 1, n:] = e_im
        e_re, e_im = _cmul_add(lm_re, lm_im, e_re, e_im, ends[0][s:s + 1], ends[1][s:s + 1])
    carry_s[:, :n] = e_re
    carry_s[:, n:] = e_im
    hlast_ref[0] = carry_s[...]

    def fix_body(k, corr):
        row = pl.multiple_of(k * V7X_SUBLANES, V7X_SUBLANES)
        rows = pl.ds(row, V7X_SUBLANES)
        c_re, c_im = _cmul_add(lam8_ref[:, :n], lam8_ref[:, n:], corr[0], corr[1], 0.0, 0.0)
        hs[rows, :n] = hs[rows, :n] + c_re
        hs[rows, n:] = hs[rows, n:] + c_im
        return c_re, c_im

    lax.fori_loop(0, m, fix_body, (ein_s[:, :n], ein_s[:, n:]))

    def h_tile(i):
        return hs[:, i * S5_TILE_N:(i + 1) * S5_TILE_N], hs[:, n + i * S5_TILE_N:n + (i + 1) * S5_TILE_N]

    y_ref[0] = _s5_output_map(h_tile, u, ccat_ref, d_ref, wglu_ref)


def _s5_sample_kernel(u_ref, h0_ref, bcat_ref, lam_ref, ccat_ref, d_ref, wglu_ref, y_ref, hlast_ref, hs):
    n = S5_N
    steps, bd, _ = u_ref.shape
    h_re, h_im = h0_ref[:, :n], h0_ref[:, n:]
    lam_re, lam_im = lam_ref[:, :n], lam_ref[:, n:]
    for t in range(steps):
        u = u_ref[t]

        def store_bu(i, re, im):
            hs[:, i * S5_TILE_N:(i + 1) * S5_TILE_N] = re
            hs[:, n + i * S5_TILE_N:n + (i + 1) * S5_TILE_N] = im

        _s5_input_map(u, bcat_ref, store_bu)
        h_re, h_im = _cmul_add(lam_re, lam_im, h_re, h_im, hs[:, :n], hs[:, n:])
        hs[:, :n] = h_re
        hs[:, n:] = h_im

        def h_tile(i):
            return hs[:, i * S5_TILE_N:(i + 1) * S5_TILE_N], hs[:, n + i * S5_TILE_N:n + (i + 1) * S5_TILE_N]

        y_ref[t] = _s5_output_map(h_tile, u, ccat_ref, d_ref, wglu_ref)
    hlast_ref[:, :n] = h_re
    hlast_ref[:, n:] = h_im


def _s5_params(a_re, a_im, log_dt, b_re, b_im, c_re, c_im):
    lam = lax.complex(a_re, a_im)
    dt = jnp.exp(log_dt)[:, None]
    lam_bar = jnp.exp(lam * dt)
    b_bar = ((lam_bar - 1.0) / lam)[:, :, None] * lax.complex(b_re, b_im)
    gpt = S5_GROUPS // S5_TILES
    eye = jnp.eye(gpt, dtype=jnp.float32)

    def in_blockdiag(w):
        w = w.reshape(S5_TILES, gpt, S5_STATE, S5_GROUP)
        return jnp.einsum('tgpc,gh->tgchp', w, eye).reshape(S5_TILES, S5_TILE_IN, S5_TILE_N)

    def out_blockdiag(w):
        w = w.reshape(S5_TILES, gpt, S5_GROUP, S5_STATE)
        return jnp.einsum('tgcp,gh->tgphc', w, eye).reshape(S5_TILES, S5_TILE_N, S5_TILE_IN)

    b_full = jnp.concatenate([in_blockdiag(b_bar.real), in_blockdiag(b_bar.imag)], axis=-1)
    b_hi = b_full.astype(jnp.bfloat16)
    b_lo = (b_full - b_hi.astype(jnp.float32)).astype(jnp.bfloat16)
    bcat = jnp.concatenate([b_hi, b_lo, b_hi], axis=1)
    ccat = jnp.concatenate([out_blockdiag(c_re), out_blockdiag(-c_im)], axis=1).astype(jnp.bfloat16)
    flat = lambda z: jnp.concatenate([z.real.reshape(1, S5_N), z.imag.reshape(1, S5_N)], axis=-1)
    lam_row = flat(lam_bar)
    lamm_row = flat(jnp.exp(lam * dt * S5_SUB))
    return bcat, ccat, lam_row, lamm_row


def s5_prompt(u, a_re, a_im, log_dt, b_re, b_im, c_re, c_im, d, w_glu):
    bn, s, _ = u.shape
    tc, m = S5_CHUNK, S5_SUB
    assert s % tc == 0
    nc = s // tc
    bcat, ccat, lam_row, lamm_row = _s5_params(a_re, a_im, log_dt, b_re, b_im, c_re, c_im)
    lam8 = jnp.broadcast_to(lam_row, (V7X_SUBLANES, 2 * S5_N))
    to_ks = lambda a: a.reshape(bn, nc, V7X_SUBLANES, m, S5_WIDTH).swapaxes(2, 3).reshape(bn, s, S5_WIDTH)
    from_ks = lambda a: a.reshape(bn, nc, m, V7X_SUBLANES, S5_WIDTH).swapaxes(2, 3).reshape(bn, s, S5_WIDTH)
    h0 = jnp.zeros((bn, 1, 2 * S5_N), jnp.float32)
    const = lambda shape: pl.BlockSpec(shape, lambda b, c: (0,) * len(shape))
    y, hlast = pl.pallas_call(
        _s5_prompt_kernel,
        grid=(bn, nc),
        in_specs=[
            pl.BlockSpec((1, tc, S5_WIDTH), lambda b, c: (b, c, 0)),
            pl.BlockSpec((1, 1, 2 * S5_N), lambda b, c: (b, 0, 0)),
            const(bcat.shape), const(lam8.shape), const(lamm_row.shape), const(ccat.shape),
            const((1, S5_WIDTH)), const((S5_WIDTH, S5_WIDTH)),
        ],
        out_specs=[pl.BlockSpec((1, tc, S5_WIDTH), lambda b, c: (b, c, 0)),
                   pl.BlockSpec((1, 1, 2 * S5_N), lambda b, c: (b, 0, 0))],
        out_shape=[jax.ShapeDtypeStruct((bn, s, S5_WIDTH), jnp.float32),
                   jax.ShapeDtypeStruct((bn, 1, 2 * S5_N), jnp.float32)],
        scratch_shapes=[pltpu.VMEM((tc, 2 * S5_N), jnp.float32),
                        pltpu.VMEM((V7X_SUBLANES, 2 * S5_N), jnp.float32),
                        pltpu.VMEM((1, 2 * S5_N), jnp.float32)],
        compiler_params=pltpu.CompilerParams(dimension_semantics=("arbitrary", "arbitrary"),
                                             vmem_limit_bytes=VMEM_LIMIT_BYTES),
        name="s5_prompt",
    )(to_ks(u), h0, bcat, lam8, lamm_row, ccat, d.reshape(1, S5_WIDTH), w_glu.astype(jnp.bfloat16))
    return from_ks(y), hlast.reshape(bn, 2, S5_GROUPS, S5_STATE)


def s5_sample(u, state, a_re, a_im, log_dt, b_re, b_im, c_re, c_im, d, w_glu):
    bd, steps, _ = u.shape
    bcat, ccat, lam_row, _ = _s5_params(a_re, a_im, log_dt, b_re, b_im, c_re, c_im)
    y, hlast = pl.pallas_call(
        _s5_sample_kernel,
        out_shape=[jax.ShapeDtypeStruct((steps, bd, S5_WIDTH), jnp.float32),
                   jax.ShapeDtypeStruct((bd, 2 * S5_N), jnp.float32)],
        scratch_shapes=[pltpu.VMEM((bd, 2 * S5_N), jnp.float32)],
        compiler_params=pltpu.CompilerParams(vmem_limit_bytes=VMEM_LIMIT_BYTES),
        name="s5_sample",
    )(jnp.swapaxes(u, 0, 1), state.reshape(bd, 2 * S5_N), bcat, lam_row, ccat,
      d.reshape(1, S5_WIDTH), w_glu.astype(jnp.bfloat16))
    return jnp.swapaxes(y, 0, 1), hlast.reshape(bd, 2, S5_GROUPS, S5_STATE)


SAMPLE_PAGES_PER_STEP = 16


def _dsa_sample_kernel(pt_ref, qi_ref, wi_ref, q_ref, kinew_ref, kvnew_ref, *rest, topk, past_len):
    pg = SAMPLE_PAGES_PER_STEP
    idx_refs = rest[0:pg]
    kv_refs = rest[pg:2 * pg]
    o_ref, key_s, kv_all = rest[2 * pg:]
    cw = pg * PAGE_SIZE
    n_steps = past_len // cw
    steps_q = q_ref.shape[1]
    rows = key_s.shape[1]
    rep = B_HEADS // B_KV_HEADS
    kvw = B_KV_HEADS * 2 * B_DH
    s_id = pl.program_id(1)
    qpos = past_len + lax.broadcasted_iota(jnp.int32, (rows, 1), 0)

    qi = qi_ref[0]
    wi = wi_ref[0]
    qcat = jnp.concatenate([_split3_lhs(qi[:, h * IDX_DH:(h + 1) * IDX_DH]) for h in range(IDX_HEADS)], axis=0)
    row_ok = lax.broadcasted_iota(jnp.int32, (rows, 1), 0) < steps_q

    def scores(kidx_rows, kpos):
        s = lax.dot_general(qcat, _split3_rhs(kidx_rows), _NT, preferred_element_type=jnp.float32) * (IDX_DH ** -0.5)
        acc = None
        for h in range(IDX_HEADS):
            term = wi[:, h:h + 1] * jnp.maximum(s[h * steps_q:(h + 1) * steps_q], 0.0)
            acc = term if acc is None else acc + term
        acc = jnp.concatenate([acc, jnp.zeros((rows - steps_q, acc.shape[1]), jnp.float32)], axis=0)
        return jnp.where((kpos <= qpos) & row_ok, _f32_order_key(acc), INT_MIN)

    kpos = s_id * cw + lax.broadcasted_iota(jnp.int32, (rows, cw), 1)
    key_s[s_id] = scores(jnp.concatenate([r[0] for r in idx_refs], axis=0), kpos)
    for i in range(pg):
        start = pl.multiple_of(s_id * cw + i * PAGE_SIZE, PAGE_SIZE)
        kv_all[pl.ds(start, PAGE_SIZE), :] = kv_refs[i][0].astype(jnp.bfloat16)

    @pl.when(s_id == n_steps - 1)
    def _():
        pad = jnp.zeros((cw - steps_q, IDX_DH), jnp.float32)
        kpos_new = past_len + lax.broadcasted_iota(jnp.int32, (rows, cw), 1)
        new_key = scores(jnp.concatenate([kinew_ref[0], pad], axis=0), kpos_new)
        lane = lax.broadcasted_iota(jnp.int32, (rows, cw), 1)
        key_s[n_steps] = jnp.where(lane < steps_q, new_key, INT_MIN)
        kv_all[pl.ds(past_len, cw), :] = jnp.concatenate(
            [kvnew_ref[0], jnp.zeros((cw - steps_q, kvw), jnp.float32)], axis=0).astype(jnp.bfloat16)
        n_chunks = n_steps + 1
        thr, cut = _topk_threshold(key_s, n_chunks, topk, index_bits=int(math.ceil(math.log2(n_chunks * cw))))

        q = q_ref[0]
        qrows = []
        for h in range(B_HEADS):
            g = h // rep
            qh = q[:, h * B_DH:(h + 1) * B_DH]
            pieces = [jnp.zeros((steps_q, g * 2 * B_DH), jnp.float32)] if g else []
            pieces += [qh, jnp.zeros((steps_q, kvw - g * 2 * B_DH - B_DH), jnp.float32)]
            qrows.append(jnp.concatenate(pieces, axis=-1))
        qx = jnp.concatenate(qrows, axis=0).astype(jnp.bfloat16)
        nr = B_HEADS * steps_q

        def att_chunk(c, carry):
            m, l, acc = carry
            start = pl.multiple_of(c * cw, cw)
            kidx = start + lax.broadcasted_iota(jnp.int32, (rows, cw), 1)
            sel = _selected(key_s[c], kidx, thr, cut)[0:steps_q][None]
            kvc = kv_all[pl.ds(start, cw), :]
            s = lax.dot_general(qx, kvc, _NT, preferred_element_type=jnp.float32) * (B_DH ** -0.5)
            s = jnp.where(sel, s.reshape(B_HEADS, steps_q, cw), NEG)
            m_new = jnp.maximum(m, jnp.max(s, axis=-1, keepdims=True))
            p = jnp.where(sel, jnp.exp(s - m_new), 0.0)
            alpha = jnp.exp(m - m_new)
            l = alpha * l + jnp.sum(p, axis=-1, keepdims=True)
            pv = jnp.dot(p.reshape(nr, cw).astype(jnp.bfloat16), kvc, preferred_element_type=jnp.float32)
            return m_new, l, alpha * acc + pv.reshape(B_HEADS, steps_q, kvw)

        init = (jnp.full((B_HEADS, steps_q, 1), NEG, jnp.float32), jnp.zeros((B_HEADS, steps_q, 1), jnp.float32),
                jnp.zeros((B_HEADS, steps_q, kvw), jnp.float32))
        _, l, acc = lax.fori_loop(0, n_chunks, att_chunk, init)
        o = jnp.where(l > 0.0, acc / jnp.where(l > 0.0, l, 1.0), 0.0)
        heads = []
        for h in range(B_HEADS):
            g = h // rep
            heads.append(o[h][:, g * 2 * B_DH + B_DH:(g + 1) * 2 * B_DH])
        o_ref[0] = jnp.concatenate(heads, axis=-1)


def dsa_sample(q, kv, qi, ki, wi, cache_kv, cache_idx, page_table):
    bd, steps_q, _ = q.shape
    n_pages = page_table.shape[1]
    past_len = n_pages * PAGE_SIZE
    pg = SAMPLE_PAGES_PER_STEP
    assert n_pages % pg == 0
    n_steps = n_pages // pg
    cw = pg * PAGE_SIZE
    topk = min(DSA_TOPK, (past_len + steps_q) // 4)
    kvw = B_KV_HEADS * 2 * B_DH
    n_pool = cache_kv.shape[0]
    rows = V7X_SUBLANES
    assert steps_q <= rows

    def page_spec(i, width):
        return pl.BlockSpec((1, PAGE_SIZE, width), lambda b, s, pt: (pt[b * n_pages + s * pg + i], 0, 0))

    per_b = lambda shape: pl.BlockSpec((1,) + shape, lambda b, s, pt: (b, 0, 0))
    grid_spec = pltpu.PrefetchScalarGridSpec(
        num_scalar_prefetch=1,
        grid=(bd, n_steps),
        in_specs=[per_b((steps_q, IDX_HEADS * IDX_DH)), per_b((steps_q, IDX_HEADS)), per_b((steps_q, B_HEADS * B_DH)),
                  per_b((steps_q, IDX_DH)), per_b((steps_q, kvw))]
        + [page_spec(i, IDX_DH) for i in range(pg)] + [page_spec(i, kvw) for i in range(pg)],
        out_specs=per_b((steps_q, B_HEADS * B_DH)),
        scratch_shapes=[pltpu.VMEM((n_steps + 1, rows, cw), jnp.int32),
                        pltpu.VMEM((past_len + cw, kvw), jnp.bfloat16)],
    )
    return pl.pallas_call(
        functools.partial(_dsa_sample_kernel, topk=topk, past_len=past_len),
        grid_spec=grid_spec,
        out_shape=jax.ShapeDtypeStruct((bd, steps_q, B_HEADS * B_DH), jnp.float32),
        compiler_params=pltpu.CompilerParams(dimension_semantics=("arbitrary", "arbitrary"),
                                             vmem_limit_bytes=VMEM_LIMIT_BYTES),
        name="dsa_sample",
    )(page_table.reshape(-1), qi, wi, q, ki, kv.reshape(bd, steps_q, kvw),
      *([cache_idx] * pg), *([cache_kv.reshape(n_pool, PAGE_SIZE, kvw)] * pg))


def _nsa_compress_kernel(pt_ref, cmp_pos_ref, w1_ref, w2_ref, *rest):
    pg = SAMPLE_PAGES_PER_STEP
    page_refs = rest[:pg]
    kck_ref, kcv_ref, xs = rest[pg:]
    bpp = PAGE_SIZE // CMP_BLOCK
    nblk = pg * bpp
    slabs = C_KV * 2
    prow = PAGE_SIZE * slabs
    for i in range(pg):
        xs[i * prow:(i + 1) * prow, :] = page_refs[i][0]
    for c, out_ref in enumerate((kck_ref, kcv_ref)):
        acc = jnp.zeros((C_KV * nblk, C_DH), jnp.float32)
        for pp in range(CMP_BLOCK // 2):
            cols = []
            for pos in (2 * pp, 2 * pp + 1):
                bias = cmp_pos_ref[pos:pos + 1, c * C_DH:(c + 1) * C_DH]
                cols.append(jnp.concatenate(
                    [xs[pl.ds(pos * slabs + g * 2 + c, nblk, stride=CMP_BLOCK * slabs), :] + bias
                     for g in range(C_KV)], axis=0))
            lhs = jnp.concatenate(cols, axis=-1).astype(jnp.bfloat16)
            acc = acc + jnp.dot(lhs, w1_ref[c, pp], preferred_element_type=jnp.float32)
        kc = jnp.dot(jax.nn.gelu(acc).astype(jnp.bfloat16), w2_ref[c], preferred_element_type=jnp.float32)
        for g in range(C_KV):
            out_ref[0, g] = kc[g * nblk:(g + 1) * nblk]


def nsa_compress(pages, page_table, cmp_pos, cmp_w1, cmp_w2):
    bn, n_pages = page_table.shape
    pg = SAMPLE_PAGES_PER_STEP
    assert n_pages % pg == 0
    n_pool = pages.shape[0]
    prow = PAGE_SIZE * C_KV * 2
    bpp = PAGE_SIZE // CMP_BLOCK
    nb = n_pages * bpp
    w1 = cmp_w1.reshape(2, CMP_BLOCK // 2, 2 * C_DH, C_DH).astype(jnp.bfloat16)
    w2 = cmp_w2.astype(jnp.bfloat16)

    def page_spec(i):
        return pl.BlockSpec((1, prow, C_DH), lambda b, s, pt: (pt[b * n_pages + s * pg + i], 0, 0))

    const = lambda shape: pl.BlockSpec(shape, lambda b, s, pt: (0,) * len(shape))
    out_spec = pl.BlockSpec((1, C_KV, pg * bpp, C_DH), lambda b, s, pt: (b, 0, s, 0))
    grid_spec = pltpu.PrefetchScalarGridSpec(
        num_scalar_prefetch=1,
        grid=(bn, n_pages // pg),
        in_specs=[const((CMP_BLOCK, 2 * C_DH)), const(w1.shape), const(w2.shape)] + [page_spec(i) for i in range(pg)],
        out_specs=[out_spec, out_spec],
        scratch_shapes=[pltpu.VMEM((pg * prow, C_DH), jnp.float32)],
    )
    shape = jax.ShapeDtypeStruct((bn, C_KV, nb, C_DH), jnp.float32)
    return pl.pallas_call(
        _nsa_compress_kernel,
        grid_spec=grid_spec,
        out_shape=[shape, shape],
        compiler_params=pltpu.CompilerParams(dimension_semantics=("arbitrary", "arbitrary"),
                                             vmem_limit_bytes=VMEM_LIMIT_BYTES),
        name="nsa_compress",
    )(page_table.reshape(-1), cmp_pos.reshape(CMP_BLOCK, 2 * C_DH), w1, w2,
      *([pages.reshape(n_pool, prow, C_DH)] * pg))


def _nsa_sample_attend_kernel(qraw_ref, qrot_ref, kck_ref, kcv_ref, win_ref, kvw_ref, tri_ref,
                              ocmp_ref, owin_ref, sel_ref, key_s, *, past_len, n_sel):
    rep = C_REP
    scale = C_DH ** -0.5
    steps_q = qraw_ref.shape[1]
    rows = key_s.shape[1]
    nbp = key_s.shape[2]
    nb = kck_ref.shape[2]
    wbuf = win_ref.shape[1] // (C_KV * 2)
    qpos = past_len + lax.broadcasted_iota(jnp.int32, (steps_q, 1), 0)
    blk_c = lax.broadcasted_iota(jnp.int32, (steps_q, nb), 1)
    cmask = ((blk_c + 1) * CMP_BLOCK - 1) <= qpos
    any_c = jnp.where(jnp.max(jnp.where(cmask, 1.0, 0.0), axis=-1, keepdims=True) > 0.5, 1.0, 0.0)
    blk = lax.broadcasted_iota(jnp.int32, (steps_q, nbp), 1)
    cur = qpos // SEL_BLOCK
    forced = (blk == 0) | (blk == cur) | (blk == cur - 1)
    nbs = -(-(past_len + steps_q) // SEL_BLOCK)
    valid = (blk * SEL_BLOCK <= qpos) & (blk < nbs)
    lane = lax.broadcasted_iota(jnp.int32, (rows, nbp), 1)
    wlen = wbuf + V7X_SUBLANES
    wcol = lax.broadcasted_iota(jnp.int32, (steps_q, wlen), 1)
    wpos = jnp.where(wcol < wbuf, past_len - wbuf + wcol, past_len + wcol - wbuf)
    wmask = ((wpos >= 0) & (wpos <= qpos) & (wpos > qpos - WINDOW) & (wcol < wbuf + steps_q))[None]
    qraw = qraw_ref[0]
    qrot = qrot_ref[0]

    def stack(x, g):
        return jnp.concatenate([x[:, (g * rep + r) * C_DH:(g * rep + r + 1) * C_DH] for r in range(rep)], axis=0)

    pad_rows = lambda x: jnp.concatenate([x, jnp.zeros((rows - steps_q,) + x.shape[1:], x.dtype)], axis=0)

    for g in range(C_KV):
        s = lax.dot_general(stack(qraw, g), kck_ref[0, g], _NT, precision=lax.Precision.HIGHEST,
                            preferred_element_type=jnp.float32) * scale
        s = jnp.where(cmask[None], s.reshape(rep, steps_q, nb), NEG)
        e = jnp.exp(s - jnp.max(s, axis=-1, keepdims=True))
        p = e / jnp.sum(e, axis=-1, keepdims=True) * any_c[None]
        o_cmp = jnp.dot(p.reshape(rep * steps_q, nb), kcv_ref[0, g], precision=lax.Precision.HIGHEST,
                        preferred_element_type=jnp.float32)
        imp = p[0]
        for r in range(1, rep):
            imp = imp + p[r]
        imp = jnp.concatenate([imp, jnp.zeros((steps_q, nbp - nb), jnp.float32)], axis=-1)
        score = jnp.where(forced, FORCE, imp)
        key_s[0] = pad_rows(jnp.where(valid, _f32_order_key(score), INT_MIN))
        thr, cut = _topk_threshold(key_s, 1, n_sel, index_bits=int(math.ceil(math.log2(nbp))))
        mask = _selected(key_s[0], lane, thr, cut)
        prefix = jnp.dot(jnp.where(mask, 1.0, 0.0).astype(jnp.bfloat16), tri_ref[...],
                         preferred_element_type=jnp.float32)
        ids = [jnp.sum(jnp.where(mask & (prefix == float(t + 1)), lane, 0), axis=-1, keepdims=True)
               for t in range(n_sel)]
        ids.append(jnp.zeros((rows, V7X_LANES - n_sel), jnp.int32))
        sel_ref[0, g] = jnp.concatenate(ids, axis=-1)

        qg = stack(qrot, g).astype(jnp.bfloat16)
        newkv = kvw_ref[0]
        padn = jnp.zeros((V7X_SUBLANES - steps_q, C_DH), jnp.float32)
        slabs = C_KV * 2
        kwin = jnp.concatenate([win_ref.at[0][pl.ds(2 * g, wbuf, stride=slabs), :],
                                newkv[:, (2 * g) * C_DH:(2 * g + 1) * C_DH], padn], axis=0).astype(jnp.bfloat16)
        vwin = jnp.concatenate([win_ref.at[0][pl.ds(2 * g + 1, wbuf, stride=slabs), :],
                                newkv[:, (2 * g + 1) * C_DH:(2 * g + 2) * C_DH], padn], axis=0).astype(jnp.bfloat16)
        sw = lax.dot_general(qg, kwin, _NT, preferred_element_type=jnp.float32) * scale
        sw = jnp.where(wmask, sw.reshape(rep, steps_q, wlen), NEG)
        ew = jnp.where(wmask, jnp.exp(sw - jnp.max(sw, axis=-1, keepdims=True)), 0.0)
        lw = jnp.sum(ew, axis=-1, keepdims=True)
        pw = ew / jnp.where(lw > 0.0, lw, 1.0)
        o_win = jnp.dot(pw.reshape(rep * steps_q, wlen).astype(jnp.bfloat16), vwin, preferred_element_type=jnp.float32)
        for r in range(rep):
            hh = g * rep + r
            ocmp_ref[0, :, hh * C_DH:(hh + 1) * C_DH] = o_cmp[r * steps_q:(r + 1) * steps_q]
            owin_ref[0, :, hh * C_DH:(hh + 1) * C_DH] = o_win[r * steps_q:(r + 1) * steps_q]


def nsa_sample_attend(q_raw, q_rot, kck, kcv, win, kv_w):
    bd, steps_q, qd = q_raw.shape
    nb = kck.shape[2]
    past_len = nb * CMP_BLOCK
    nbs = -(-(past_len + steps_q) // SEL_BLOCK)
    assert nbs >= N_SEL and steps_q <= V7X_SUBLANES
    nbp = -(-nbs // V7X_LANES) * V7X_LANES
    wbuf = win.shape[1]
    roww = C_KV * 2 * C_DH
    tri = (jnp.arange(nbp)[:, None] <= jnp.arange(nbp)[None, :]).astype(jnp.bfloat16)
    per_b = lambda shape: pl.BlockSpec((1,) + shape, lambda b: (b,) + (0,) * len(shape))
    return pl.pallas_call(
        functools.partial(_nsa_sample_attend_kernel, past_len=past_len, n_sel=N_SEL),
        grid=(bd,),
        in_specs=[per_b((steps_q, qd)), per_b((steps_q, qd)), per_b((C_KV, nb, C_DH)), per_b((C_KV, nb, C_DH)),
                  per_b((wbuf * C_KV * 2, C_DH)), per_b((steps_q, roww)), pl.BlockSpec((nbp, nbp), lambda b: (0, 0))],
        out_specs=[per_b((steps_q, qd)), per_b((steps_q, qd)), per_b((C_KV, V7X_SUBLANES, V7X_LANES))],
        out_shape=[jax.ShapeDtypeStruct((bd, steps_q, qd), jnp.float32),
                   jax.ShapeDtypeStruct((bd, steps_q, qd), jnp.float32),
                   jax.ShapeDtypeStruct((bd, C_KV, V7X_SUBLANES, V7X_LANES), jnp.int32)],
        scratch_shapes=[pltpu.VMEM((1, V7X_SUBLANES, nbp), jnp.int32)],
        compiler_params=pltpu.CompilerParams(dimension_semantics=("arbitrary",), vmem_limit_bytes=VMEM_LIMIT_BYTES),
        name="nsa_sample_attend",
    )(q_raw, q_rot, kck, kcv, win.reshape(bd, wbuf * C_KV * 2, C_DH), kv_w.reshape(bd, steps_q, roww), tri)


def _nsa_sample_selected_kernel(phys_ref, blk_ref, q_ref, gates_ref, ocmp_ref, owin_ref, new_ref, *rest,
                                past_len, steps_q):
    blk_refs = rest[:C_KV * N_SEL]
    o_ref = rest[C_KV * N_SEL]
    i = pl.program_id(0)
    qpos = past_len + i % steps_q
    scale = C_DH ** -0.5
    slabs = C_KV * 2
    row = lax.broadcasted_iota(jnp.int32, (1, SEL_BLOCK), 1)
    ncol = lax.broadcasted_iota(jnp.int32, (1, V7X_SUBLANES), 1)
    nb_past = past_len // SEL_BLOCK
    padn = jnp.zeros((V7X_SUBLANES - steps_q, C_DH), jnp.float32)
    sig = jax.nn.sigmoid(gates_ref[0])
    newkv = new_ref[0]
    for g in range(C_KV):
        q = q_ref[0, g * C_REP:(g + 1) * C_REP, :].astype(jnp.bfloat16)
        ks, vs, kpos = [], [], []
        has_new = jnp.int32(0)
        for t in range(N_SEL):
            b_id = blk_ref[(i * C_KV + g) * N_SEL + t]
            is_new = b_id >= nb_past
            has_new = jnp.maximum(has_new, is_new.astype(jnp.int32))
            ref = blk_refs[g * N_SEL + t].at[0]
            ks.append(ref[pl.ds(2 * g, SEL_BLOCK, stride=slabs), :])
            vs.append(ref[pl.ds(2 * g + 1, SEL_BLOCK, stride=slabs), :])
            kpos.append(jnp.where(is_new, jnp.int32(INT_MAX), b_id * SEL_BLOCK + row))
        ks.append(jnp.concatenate([newkv[:, (2 * g) * C_DH:(2 * g + 1) * C_DH], padn], axis=0))
        vs.append(jnp.concatenate([newkv[:, (2 * g + 1) * C_DH:(2 * g + 2) * C_DH], padn], axis=0))
        kpos.append(jnp.where((has_new > 0) & (ncol < steps_q), past_len + ncol, jnp.int32(INT_MAX)))
        k = jnp.concatenate(ks, axis=0).astype(jnp.bfloat16)
        v = jnp.concatenate(vs, axis=0).astype(jnp.bfloat16)
        mask = jnp.concatenate(kpos, axis=-1) <= qpos
        s = lax.dot_general(q, k, _NT, preferred_element_type=jnp.float32) * scale
        s = jnp.where(mask, s, NEG)
        e = jnp.where(mask, jnp.exp(s - jnp.max(s, axis=-1, keepdims=True)), 0.0)
        l = jnp.sum(e, axis=-1, keepdims=True)
        p = e / jnp.where(l > 0.0, l, 1.0)
        o_slc = jnp.dot(p.astype(jnp.bfloat16), v, preferred_element_type=jnp.float32)
        hs = slice(g * C_REP, (g + 1) * C_REP)
        o_ref[0, hs, :] = (sig[hs, 0:1] * ocmp_ref[0, hs, :] + sig[hs, 1:2] * o_slc
                           + sig[hs, 2:3] * owin_ref[0, hs, :])


def nsa_sample_selected(sel_idx, q_rot, gates, o_cmp, o_win, kv_s, cache_slc, page_table):
    bd, steps_q, qd = q_rot.shape
    n_pages = page_table.shape[1]
    past_len = n_pages * PAGE_SIZE
    n_pool = cache_slc.shape[0]
    bpp = PAGE_SIZE // SEL_BLOCK
    slabs = C_KV * 2
    blk = jnp.transpose(sel_idx[:, :, :steps_q, :N_SEL], (0, 2, 1, 3))
    page = jnp.minimum(blk // bpp, n_pages - 1)
    phys = jnp.take_along_axis(page_table, page.reshape(bd, -1), axis=1).reshape(blk.shape) * bpp + blk % bpp
    n = bd * steps_q
    heads = lambda a: a.reshape(n, C_HEADS, C_DH)

    def blk_spec(g, t):
        return pl.BlockSpec((1, SEL_BLOCK * slabs, C_DH),
                            lambda i, ph, bl: (ph[(i * C_KV + g) * N_SEL + t], 0, 0))

    row_spec = lambda shape: pl.BlockSpec((1,) + shape, lambda i, ph, bl: (i,) + (0,) * len(shape))
    grid_spec = pltpu.PrefetchScalarGridSpec(
        num_scalar_prefetch=2,
        grid=(n,),
        in_specs=[row_spec((C_HEADS, C_DH)), row_spec((C_HEADS, 3)), row_spec((C_HEADS, C_DH)),
                  row_spec((C_HEADS, C_DH)),
                  pl.BlockSpec((1, steps_q, slabs * C_DH), lambda i, ph, bl: (i // steps_q, 0, 0))]
        + [blk_spec(g, t) for g in range(C_KV) for t in range(N_SEL)],
        out_specs=row_spec((C_HEADS, C_DH)),
    )
    o = pl.pallas_call(
        functools.partial(_nsa_sample_selected_kernel, past_len=past_len, steps_q=steps_q),
        grid_spec=grid_spec,
        out_shape=jax.ShapeDtypeStruct((n, C_HEADS, C_DH), jnp.float32),
        compiler_params=pltpu.CompilerParams(dimension_semantics=("arbitrary",), vmem_limit_bytes=VMEM_LIMIT_BYTES),
        name="nsa_sample_selected",
    )(phys.reshape(-1).astype(jnp.int32), blk.reshape(-1).astype(jnp.int32),
      heads(q_rot), gates.reshape(n, C_HEADS, 3), heads(o_cmp), heads(o_win),
      kv_s.reshape(bd, steps_q, slabs * C_DH),
      *([cache_slc.reshape(n_pool * bpp, SEL_BLOCK * slabs, C_DH)] * (C_KV * N_SEL)))
    return o.reshape(bd, steps_q, qd)


def split_cols(h, sizes):
    out, start = [], 0
    for n in sizes:
        out.append(h[..., start:start + n])
        start += n
    return out


def layer_norm(x, g, b):
    mu = x.mean(-1, keepdims=True)
    var = jnp.square(x - mu).mean(-1, keepdims=True)
    return (x - mu) * lax.rsqrt(var + LN_EPS) * g + b


def rope_partial(x, pos, rot_dim):
    half = rot_dim // 2
    inv = ROPE_THETA ** (-jnp.arange(half, dtype=jnp.float32) / half)
    ang = pos.astype(jnp.float32)[:, None] * inv
    cos, sin = jnp.cos(ang)[:, None, :], jnp.sin(ang)[:, None, :]
    x1 = x[..., :half]
    x2 = x[..., half:rot_dim]
    rot = jnp.concatenate([x1 * cos - x2 * sin, x2 * cos + x1 * sin], -1)
    return jnp.concatenate([rot, x[..., rot_dim:]], -1)


def rope_kv(kv, pos, rot_dim):
    k = rope_partial(kv[..., 0, :], pos, rot_dim)
    return jnp.stack([k, kv[..., 1, :]], axis=-2)


def mixer_ab(x, pos, w_in, a_re, a_im, log_dt, b_re, b_im, c_re, c_im, d, w_glu, w_out,
             state=None, cache_kv=None, cache_idx=None, page_table=None):
    bn, s, _ = x.shape
    s5p = (a_re, a_im, log_dt, b_re, b_im, c_re, c_im, d, w_glu)
    u, q, kv, qi, ki, wi = split_cols(x @ w_in, AB_SIZES)
    q = rope_partial(q.reshape(bn, s, B_HEADS, B_DH), pos, B_ROT)
    kv = rope_kv(kv.reshape(bn, s, B_KV_HEADS, 2, B_DH), pos, B_ROT)
    qi = rope_partial(qi.reshape(bn, s, IDX_HEADS, IDX_DH), pos, IDX_ROT)
    ki = rope_partial(ki.reshape(bn, s, 1, IDX_DH), pos, IDX_ROT)[:, :, 0]
    if state is None:
        o_b = dsa_prompt(q.reshape(bn, s, B_HEADS * B_DH), kv, qi.reshape(bn, s, IDX_HEADS * IDX_DH), ki, wi)
        y_s5, new_a = s5_prompt(u, *s5p)
    else:
        o_b = dsa_sample(q.reshape(bn, s, B_HEADS * B_DH), kv, qi.reshape(bn, s, IDX_HEADS * IDX_DH), ki, wi,
                         cache_kv, cache_idx, page_table)
        y_s5, new_a = s5_sample(u, state, *s5p)
    out = jnp.concatenate([y_s5, o_b], -1) @ w_out
    return out, new_a, kv, ki


def mixer_c(x, pos, w_in, cmp_pos, cmp_w1, cmp_w2, w_out,
            cache_cmp=None, cache_slc=None, win=None, page_table=None):
    bn, s, _ = x.shape
    q, kv_c, kv_s, kv_w, gates = split_cols(x @ w_in, C_SIZES)
    q = q.reshape(bn, s, C_HEADS, C_DH)
    q_rot = rope_partial(q, pos, C_ROT)
    kv_c = kv_c.reshape(bn, s, C_KV, 2, C_DH)
    kv_s = rope_kv(kv_s.reshape(bn, s, C_KV, 2, C_DH), pos, C_ROT)
    kv_w = rope_kv(kv_w.reshape(bn, s, C_KV, 2, C_DH), pos, C_ROT)
    qd = C_HEADS * C_DH
    q_raw2, q_rot2 = q.reshape(bn, s, qd), q_rot.reshape(bn, s, qd)
    if cache_cmp is None:
        n_pages = s // PAGE_SIZE
        ident = jnp.arange(bn * n_pages, dtype=jnp.int32).reshape(bn, n_pages)
        kck, kcv = nsa_compress(kv_c.reshape(bn * n_pages, PAGE_SIZE, C_KV, 2, C_DH), ident, cmp_pos, cmp_w1, cmp_w2)
        o = nsa_prompt(q_raw2, q_rot2, gates, kck, kcv, kv_s, kv_w)
        return o @ w_out, kv_c, kv_s, kv_w[:, -min(WINDOW, s):]
    kck, kcv = nsa_compress(cache_cmp, page_table, cmp_pos, cmp_w1, cmp_w2)
    o_cmp, o_win, sel_idx = nsa_sample_attend(q_raw2, q_rot2, kck, kcv, win, kv_w)
    o = nsa_sample_selected(sel_idx, q_rot2, gates, o_cmp, o_win, kv_s, cache_slc, page_table)
    new_win = jnp.concatenate([win, kv_w], axis=1)[:, -win.shape[1]:]
    return o @ w_out, kv_c, kv_s, new_win


def kernel(x_prompt, x_sample, state_a, cache_b_kv, cache_b_idx, cache_c_cmp, cache_c_slc, state_c_win, page_table,
           w_in_ab, s5_a_re, s5_a_im, s5_log_dt, s5_b_re, s5_b_im, s5_c_re, s5_c_im, s5_d, s5_w_glu, w_out_ab,
           w_in_c, cmp_pos, cmp_w1, cmp_w2, w_out_c,
           ln1_g, ln1_b, ln2_g, ln2_b, moe_w_group, moe_w_expert, moe_w_gate, moe_w_up, moe_w_down):
    past_len = page_table.shape[1] * PAGE_SIZE
    pos_p = jnp.arange(x_prompt.shape[1])
    pos_s = past_len + jnp.arange(x_sample.shape[1])
    xp, xs = x_prompt, x_sample
    pshape, sshape = xp.shape, xs.shape
    n_p = pshape[0] * pshape[1]
    outs = {k: [] for k in ('a_p', 'a_s', 'bkv_p', 'bkv_s', 'bidx_p', 'bidx_s',
                            'ccmp_p', 'ccmp_s', 'cslc_p', 'cslc_s', 'cwin_p', 'cwin_s')}
    for i in range(DEPTH):
        j = i // 2
        if i % 2 == 0:
            ab = (w_in_ab[j], s5_a_re[j], s5_a_im[j], s5_log_dt[j], s5_b_re[j], s5_b_im[j],
                  s5_c_re[j], s5_c_im[j], s5_d[j], s5_w_glu[j], w_out_ab[j])
            mp, st_p, kv_p, ki_p = mixer_ab(xp, pos_p, *ab)
            ms, st_s, kv_n, ki_n = mixer_ab(xs, pos_s, *ab, state=state_a[j], cache_kv=cache_b_kv[j],
                                            cache_idx=cache_b_idx[j], page_table=page_table)
            outs['a_p'].append(st_p); outs['a_s'].append(st_s)
            outs['bkv_p'].append(kv_p); outs['bkv_s'].append(kv_n)
            outs['bidx_p'].append(ki_p); outs['bidx_s'].append(ki_n)
        else:
            cp = (w_in_c[j], cmp_pos[j], cmp_w1[j], cmp_w2[j], w_out_c[j])
            mp, kc_p, ksl_p, win_p = mixer_c(xp, pos_p, *cp)
            ms, kc_n, ksl_n, win_n = mixer_c(xs, pos_s, *cp, cache_cmp=cache_c_cmp[j], cache_slc=cache_c_slc[j],
                                             win=state_c_win[j], page_table=page_table)
            outs['ccmp_p'].append(kc_p); outs['ccmp_s'].append(kc_n)
            outs['cslc_p'].append(ksl_p); outs['cslc_s'].append(ksl_n)
            outs['cwin_p'].append(win_p); outs['cwin_s'].append(win_n)
        xp = layer_norm(ALPHA * xp + mp, ln1_g[i], ln1_b[i])
        xs = layer_norm(ALPHA * xs + ms, ln1_g[i], ln1_b[i])
        moe = (moe_w_group[i], moe_w_expert[i], moe_w_gate[i], moe_w_up[i], moe_w_down[i])
        yp, ys = hier_moe([xp.reshape(n_p, D_MODEL), xs.reshape(-1, D_MODEL)], *moe)
        xp = layer_norm(ALPHA * xp + yp.reshape(pshape), ln2_g[i], ln2_b[i])
        xs = layer_norm(ALPHA * xs + ys.reshape(sshape), ln2_g[i], ln2_b[i])
    st = lambda k: jnp.stack(outs[k])
    return (xp, xs, st('a_p'), st('a_s'), st('bkv_p'), st('bkv_s'), st('bidx_p'), st('bidx_s'),
            st('ccmp_p'), st('ccmp_s'), st('cslc_p'), st('cslc_s'), st('cwin_p'), st('cwin_s'))
```

```python
import functools
import math

import jax
import jax.numpy as jnp
import numpy as np
from jax import lax
from jax.experimental import pallas as pl
from jax.experimental.pallas import tpu as pltpu

D_MODEL = 1024
DEPTH = 2
PAGE_SIZE = 128

S5_WIDTH = D_MODEL // 2
S5_GROUP = 16
S5_GROUPS = S5_WIDTH // S5_GROUP
S5_STATE = 64
B_HEADS = 8
B_DH = 64
B_KV_HEADS = 2
B_ROT = B_DH // 4
IDX_HEADS = 4
IDX_DH = 64
IDX_ROT = IDX_DH // 4
DSA_TOPK = 256
C_HEADS = 8
C_DH = 128
C_KV = 2
C_REP = C_HEADS // C_KV
C_ROT = C_DH // 4
CMP_BLOCK = 64
SEL_BLOCK = 64
N_SEL = 16
WINDOW = 512
N_EGROUPS = 4
E_PER_GROUP = 8
N_EXPERTS = N_EGROUPS * E_PER_GROUP
TOP_E = 2
D_FF_E = 512

ROPE_THETA = 500000.0
ALPHA = (2 * DEPTH) ** 0.25
Q_BLOCK = 128
LN_EPS = 1e-5
NEG = -1e30
FORCE = 1e4
LOG2E = math.log2(math.e)

AB_SIZES = (S5_WIDTH, B_HEADS * B_DH, B_KV_HEADS * 2 * B_DH, IDX_HEADS * IDX_DH, IDX_DH, IDX_HEADS)
C_SIZES = (C_HEADS * C_DH, C_KV * 2 * C_DH, C_KV * 2 * C_DH, C_KV * 2 * C_DH, C_HEADS * 3)

V7X_LANES = 128
V7X_SUBLANES = 8
V7X_VMEM_BYTES = 64 * 1024 * 1024
VMEM_LIMIT_BYTES = 48 * 1024 * 1024

MOE_ROUTER_TILE = 512
MOE_EXPERT_TILE = 256


def _moe_router_kernel(x_ref, w_ref, o_ref):
    logits = jnp.dot(x_ref[...], w_ref[...], precision=lax.Precision.HIGHEST,
                     preferred_element_type=jnp.float32)
    col = lax.broadcasted_iota(jnp.int32, logits.shape, 1)
    big = jnp.int32(1 << 20)
    ninf = jnp.float32(-jnp.inf)
    gl = jnp.where(col < N_EGROUPS, logits, ninf)
    gmax = jnp.max(gl, axis=-1, keepdims=True)
    g_top = jnp.min(jnp.where(gl == gmax, col, big), axis=-1, keepdims=True)
    g_w = 1.0 / jnp.sum(jnp.exp(gl - gmax), axis=-1, keepdims=True)
    lo = N_EGROUPS + E_PER_GROUP * g_top
    el = jnp.where((col >= lo) & (col < lo + E_PER_GROUP), logits, ninf)
    v1 = jnp.max(el, axis=-1, keepdims=True)
    i1 = jnp.min(jnp.where(el == v1, col, big), axis=-1, keepdims=True)
    el2 = jnp.where(col == i1, ninf, el)
    v2 = jnp.max(el2, axis=-1, keepdims=True)
    i2 = jnp.min(jnp.where(el2 == v2, col, big), axis=-1, keepdims=True)
    e2 = jnp.exp(v2 - v1)
    den = 1.0 + e2
    w1 = g_w / den
    w2 = g_w * e2 / den
    out = jnp.where(col == 0, (i1 - N_EGROUPS).astype(jnp.float32),
                    jnp.where(col == 1, (i2 - N_EGROUPS).astype(jnp.float32),
                              jnp.where(col == 2, w1, jnp.where(col == 3, w2, 0.0))))
    o_ref[...] = out


def _moe_route(x, w_router):
    t = x.shape[0]
    tm = min(MOE_ROUTER_TILE, t)
    assert t % tm == 0
    return pl.pallas_call(
        _moe_router_kernel,
        grid=(t // tm,),
        in_specs=[pl.BlockSpec((tm, D_MODEL), lambda i: (i, 0)),
                  pl.BlockSpec((D_MODEL, V7X_LANES), lambda i: (0, 0))],
        out_specs=pl.BlockSpec((tm, V7X_LANES), lambda i: (i, 0)),
        out_shape=jax.ShapeDtypeStruct((t, V7X_LANES), jnp.float32),
        name="moe_router",
    )(x, w_router)


def _moe_expert_kernel(tile_e_ref, nvalid_ref, xs_ref, wg_ref, wu_ref, wd_ref, sw_ref, o_ref,
                       wg_s, wu_s, wd_s):
    i = pl.program_id(0)
    valid = i < nvalid_ref[0]
    prev_e = tile_e_ref[jnp.maximum(i - 1, 0)]
    new_expert = jnp.logical_or(i == 0, tile_e_ref[i] != prev_e)

    @pl.when(jnp.logical_and(valid, new_expert))
    def _():
        wg_s[...] = wg_ref[0].astype(jnp.bfloat16)
        wu_s[...] = wu_ref[0].astype(jnp.bfloat16)
        wd_s[...] = wd_ref[0].astype(jnp.bfloat16)

    @pl.when(valid)
    def _():
        x = xs_ref[...]
        g = jnp.dot(x, wg_s[...], preferred_element_type=jnp.float32)
        u = jnp.dot(x, wu_s[...], preferred_element_type=jnp.float32)
        h = (g * jax.nn.sigmoid(g)) * u
        y = jnp.dot(h.astype(jnp.bfloat16), wd_s[...], preferred_element_type=jnp.float32)
        o_ref[...] = y * sw_ref[...]

    @pl.when(jnp.logical_not(valid))
    def _():
        o_ref[...] = jnp.zeros_like(o_ref)


def _moe_experts(tile_e, nvalid, xs, slot_w, w_gate, w_up, w_down):
    np_rows = xs.shape[0]
    tm = MOE_EXPERT_TILE
    n_tiles = np_rows // tm
    grid_spec = pltpu.PrefetchScalarGridSpec(
        num_scalar_prefetch=2,
        grid=(n_tiles,),
        in_specs=[
            pl.BlockSpec((tm, D_MODEL), lambda i, te, nv: (i, 0)),
            pl.BlockSpec((1, D_MODEL, D_FF_E), lambda i, te, nv: (te[i], 0, 0)),
            pl.BlockSpec((1, D_MODEL, D_FF_E), lambda i, te, nv: (te[i], 0, 0)),
            pl.BlockSpec((1, D_FF_E, D_MODEL), lambda i, te, nv: (te[i], 0, 0)),
            pl.BlockSpec((tm, 1), lambda i, te, nv: (i, 0)),
        ],
        out_specs=pl.BlockSpec((tm, D_MODEL), lambda i, te, nv: (i, 0)),
        scratch_shapes=[pltpu.VMEM((D_MODEL, D_FF_E), jnp.bfloat16),
                        pltpu.VMEM((D_MODEL, D_FF_E), jnp.bfloat16),
                        pltpu.VMEM((D_FF_E, D_MODEL), jnp.bfloat16)],
    )
    return pl.pallas_call(
        _moe_expert_kernel,
        grid_spec=grid_spec,
        out_shape=jax.ShapeDtypeStruct((np_rows, D_MODEL), jnp.float32),
        compiler_params=pltpu.CompilerParams(dimension_semantics=("arbitrary",),
                                             vmem_limit_bytes=VMEM_LIMIT_BYTES),
        name="moe_experts",
    )(tile_e, nvalid, xs, w_gate, w_up, w_down, slot_w)


def _moe_padded_rows(t):
    tm = MOE_EXPERT_TILE
    slots = t * TOP_E
    return ((slots + N_EXPERTS * (tm - 1)) // tm + 1) * tm


def hier_moe(xs_list, w_group, w_expert, w_gate, w_up, w_down):
    tm = MOE_EXPERT_TILE
    w_router = jnp.zeros((D_MODEL, V7X_LANES), jnp.float32)
    w_router = w_router.at[:, :N_EGROUPS].set(w_group).at[:, N_EGROUPS:N_EGROUPS + N_EXPERTS].set(w_expert)
    routed = jnp.concatenate([_moe_route(x, w_router) for x in xs_list], axis=0)
    t = routed.shape[0]
    eid = routed[:, 0:2].astype(jnp.int32).reshape(-1)
    ew = routed[:, 2:4].reshape(-1)
    n_slots = t * TOP_E
    np_rows = _moe_padded_rows(t)

    experts = jnp.arange(N_EXPERTS, dtype=jnp.int32)
    counts = jnp.sum((eid[:, None] == experts[None, :]).astype(jnp.int32), axis=0)
    order = jnp.argsort(eid, stable=True).astype(jnp.int32)
    rank = jnp.argsort(order).astype(jnp.int32)
    padded = ((counts + tm - 1) // tm) * tm
    pad_end = jnp.cumsum(padded)
    pad_off = pad_end - padded
    off = jnp.cumsum(counts) - counts
    pos = pad_off[eid] + rank - off[eid]
    tile_start = jnp.arange(np_rows // tm, dtype=jnp.int32) * tm
    tile_e = jnp.minimum(jnp.sum((pad_end[None, :] <= tile_start[:, None]).astype(jnp.int32), axis=1), N_EXPERTS - 1)
    nvalid = (pad_end[-1] // tm).astype(jnp.int32).reshape(1)
    row = jnp.arange(np_rows, dtype=jnp.int32)
    row_e = tile_e[row // tm]
    local = row - pad_off[row_e]
    row_ok = local < counts[row_e]
    slot = order[jnp.clip(off[row_e] + local, 0, n_slots - 1)]
    src_tok = jnp.where(row_ok, slot // TOP_E, 0)
    slot_w = jnp.where(row_ok, ew[slot], 0.0)

    x_all = jnp.concatenate([x.astype(jnp.bfloat16) for x in xs_list], axis=0)
    ys = _moe_experts(tile_e, nvalid, x_all[src_tok], slot_w[:, None], w_gate, w_up, w_down)
    pos2 = pos.reshape(t, TOP_E)
    y = ys[pos2[:, 0]] + ys[pos2[:, 1]]
    out, start = [], 0
    for x in xs_list:
        out.append(y[start:start + x.shape[0]])
        start += x.shape[0]
    return out


INT_MIN = -(2 ** 31)
INT_MAX = 2 ** 31 - 1


def _f32_order_key(x):
    b = lax.bitcast_convert_type(x + 0.0, jnp.int32)
    return jnp.where(b >= 0, b, b ^ jnp.int32(INT_MAX))


def _lanes(x, w):
    return x if w == V7X_LANES else jnp.concatenate([x] * (w // V7X_LANES), axis=-1)


def _count_rows(key_ref, n_chunks, pred_fns):
    n_max, rows, w = key_ref.shape
    slabs = w // V7X_LANES
    assert slabs * n_max <= 256
    lane = lax.broadcasted_iota(jnp.int32, (rows, V7X_LANES), 1)

    def body(c, accs):
        out = list(accs)
        for s in range(slabs):
            blk = key_ref[c, :, s * V7X_LANES:(s + 1) * V7X_LANES]
            idx = c * w + s * V7X_LANES + lane
            for t, fn in enumerate(pred_fns):
                out[t] = out[t] + jnp.where(fn(blk, idx), 1, 0)
        return tuple(out)

    init = tuple(jnp.zeros((rows, V7X_LANES), jnp.int32) for _ in pred_fns)
    accs = lax.fori_loop(0, n_chunks, body, init)
    ones = jnp.ones((V7X_LANES, V7X_LANES), jnp.bfloat16)
    return [jnp.dot(a.astype(jnp.float32).astype(jnp.bfloat16), ones,
                    preferred_element_type=jnp.float32).astype(jnp.int32) for a in accs]


def _any_row(mask):
    return jnp.max(jnp.where(mask, 1.0, 0.0)) > 0.5


def _topk_threshold(key_ref, n_chunks, k, index_bits):
    _, rows, w = key_ref.shape
    col = lambda v: jnp.full((rows, V7X_LANES), v, jnp.int32)

    def v_step(st):
        lo, hi, exact = st
        mid = (lo & hi) + ((lo ^ hi) >> 1)
        cnt, = _count_rows(key_ref, n_chunks, [lambda blk, idx: blk > mid])
        active = lo < hi
        hit = active & (cnt == k)
        less = active & (cnt < k)
        more = active & (cnt > k)
        hi = jnp.where(hit | less, mid, hi)
        lo = jnp.where(hit, mid, jnp.where(more, mid + 1, lo))
        exact = jnp.where(hit, 1, exact)
        return lo, hi, exact

    def v_body(st):
        lo, hi, exact = v_step(v_step(st[:3]))
        return lo, hi, exact, _any_row(lo < hi)

    thr, _, exact, _ = lax.while_loop(lambda st: st[3], v_body,
                                      (col(INT_MIN), col(INT_MAX), col(0), jnp.bool_(True)))
    c_gt, c_eq = _count_rows(key_ref, n_chunks, [lambda blk, idx: blk > thr, lambda blk, idx: blk == thr])
    r = k - c_gt
    take_all_ties = (exact == 0) & (thr != INT_MIN)
    need = take_all_ties & (c_eq > r)
    cut_default = jnp.where(take_all_ties, INT_MAX, -1)

    def tie_phase():
        def body(_, st):
            lo_i, hi_i = st
            mid = (lo_i + hi_i) >> 1
            g, = _count_rows(key_ref, n_chunks, [lambda blk, idx: (blk == thr) & (idx <= mid)])
            ok = g >= r
            return jnp.where(ok, lo_i, mid + 1), jnp.where(ok, mid, hi_i)
        lo_i, _ = lax.fori_loop(0, index_bits, body, (col(0), col(0) + (n_chunks * w - 1)))
        return jnp.where(need, lo_i, cut_default)

    cut = lax.cond(_any_row(need), tie_phase, lambda: cut_default)
    return thr, cut


def _selected(key, idx, thr, cut):
    w = key.shape[-1]
    thr, cut = _lanes(thr, w), _lanes(cut, w)
    return (key > thr) | ((key == thr) & (idx <= cut))


def _count_cols(key_ref, n_chunks, pred_fns):
    _, w, cols = key_ref.shape
    acc_rows = min(w, COUNT_ACC_ROWS)

    def body(c, accs):
        blk = key_ref[c]
        idx = c * w + lax.broadcasted_iota(jnp.int32, (w, cols), 0)
        return tuple(acc + jnp.sum(jnp.where(fn(blk, idx), 1, 0).reshape(w // acc_rows, acc_rows, cols), axis=0)
                     for fn, acc in zip(pred_fns, accs))

    init = tuple(jnp.zeros((acc_rows, cols), jnp.int32) for _ in pred_fns)
    accs = lax.fori_loop(0, n_chunks, body, init)
    return [jnp.sum(a, axis=0, keepdims=True) for a in accs]


TOPK_PASSES_PER_TRIP = 4
COUNT_ACC_ROWS = 64


def _topk_threshold_cols(key_ref, n_chunks, k, index_bits):
    _, w, cols = key_ref.shape
    row = lambda v: jnp.full((1, cols), v, jnp.int32)

    def v_step(st):
        lo, hi, exact = st
        mid = (lo & hi) + ((lo ^ hi) >> 1)
        cnt, = _count_cols(key_ref, n_chunks, [lambda blk, idx: blk > mid])
        active = lo < hi
        hit = active & (cnt == k)
        less = active & (cnt < k)
        more = active & (cnt > k)
        hi = jnp.where(hit | less, mid, hi)
        lo = jnp.where(hit, mid, jnp.where(more, mid + 1, lo))
        exact = jnp.where(hit, 1, exact)
        return lo, hi, exact

    def v_body(st):
        inner = st[:3]
        for _ in range(TOPK_PASSES_PER_TRIP):
            inner = v_step(inner)
        return inner + (_any_row(inner[0] < inner[1]),)

    thr, _, exact, _ = lax.while_loop(lambda st: st[3], v_body,
                                      (row(INT_MIN), row(INT_MAX), row(0), jnp.bool_(True)))
    c_gt, c_eq = _count_cols(key_ref, n_chunks, [lambda blk, idx: blk > thr, lambda blk, idx: blk == thr])
    r = k - c_gt
    take_all_ties = (exact == 0) & (thr != INT_MIN)
    need = take_all_ties & (c_eq > r)
    cut_default = jnp.where(take_all_ties, INT_MAX, -1)

    def tie_phase():
        def body(_, st):
            lo_i, hi_i = st
            mid = (lo_i + hi_i) >> 1
            g, = _count_cols(key_ref, n_chunks, [lambda blk, idx: (blk == thr) & (idx <= mid)])
            ok = g >= r
            return jnp.where(ok, lo_i, mid + 1), jnp.where(ok, mid, hi_i)
        lo_i, _ = lax.fori_loop(0, index_bits, body, (row(0), row(0) + (n_chunks * w - 1)))
        return jnp.where(need, lo_i, cut_default)

    cut = lax.cond(_any_row(need), tie_phase, lambda: cut_default)
    return thr, cut


def _selected_cols(key, idx, thr, cut):
    return (key > thr) | ((key == thr) & (idx <= cut))


DSA_CHUNK = 1024


def _split3_lhs(x):
    hi = x.astype(jnp.bfloat16)
    lo = (x - hi.astype(jnp.float32)).astype(jnp.bfloat16)
    return jnp.concatenate([hi, hi, lo], axis=-1)


def _split3_rhs(x):
    hi = x.astype(jnp.bfloat16)
    lo = (x - hi.astype(jnp.float32)).astype(jnp.bfloat16)
    return jnp.concatenate([hi, lo, hi], axis=-1)


_NT = (((1,), (1,)), ((), ()))


def _with_ones_column(v):
    dh = v.shape[-1]
    width = (dh // V7X_LANES + 1) * V7X_LANES
    ones = jnp.ones(v.shape[:-1] + (1,), v.dtype)
    return jnp.concatenate([v, ones, jnp.zeros(v.shape[:-1] + (width - dh - 1,), v.dtype)], axis=-1)


def _dsa_prompt_kernel(qi_ref, wi_ref, kcat_ref, q_ref, k_ref, v_ref, o_ref, key_s, *, topk):
    cw = DSA_CHUNK
    rep = B_HEADS // B_KV_HEADS
    j = pl.program_id(1)
    t0 = j * Q_BLOCK
    n_chunks = (t0 + Q_BLOCK + cw - 1) // cw
    qpos = t0 + lax.broadcasted_iota(jnp.int32, (Q_BLOCK, 1), 0)

    qi = qi_ref[0]
    wi_t = wi_ref[0]
    qcat = jnp.concatenate([_split3_lhs(qi[:, h * IDX_DH:(h + 1) * IDX_DH]) for h in range(IDX_HEADS)], axis=0)
    qpos_t = t0 + lax.broadcasted_iota(jnp.int32, (1, Q_BLOCK), 1)

    def score_chunk(c, carry):
        start = pl.multiple_of(c * cw, cw)
        kc = kcat_ref[0, pl.ds(start, cw), :]
        s = lax.dot_general(kc, qcat, _NT, preferred_element_type=jnp.float32) * (IDX_DH ** -0.5)
        acc = None
        for h in range(IDX_HEADS):
            term = wi_t[h:h + 1, :] * jnp.maximum(s[:, h * Q_BLOCK:(h + 1) * Q_BLOCK], 0.0)
            acc = term if acc is None else acc + term
        kpos = start + lax.broadcasted_iota(jnp.int32, (cw, Q_BLOCK), 0)
        key_s[c] = jnp.where(kpos <= qpos_t, _f32_order_key(acc), INT_MIN)
        return carry

    lax.fori_loop(0, n_chunks, score_chunk, 0)
    thr, cut = _topk_threshold_cols(key_s, n_chunks, topk, index_bits=int(math.log2(key_s.shape[0] * cw)))

    q = q_ref[0]
    qg = [jnp.concatenate([q[:, (g * rep + r) * B_DH:(g * rep + r + 1) * B_DH] for r in range(rep)],
                          axis=0).astype(jnp.bfloat16) for g in range(B_KV_HEADS)]

    def att_chunk(c, carry):
        start = pl.multiple_of(c * cw, cw)
        kidx = start + lax.broadcasted_iota(jnp.int32, (cw, Q_BLOCK), 0)
        bias = jnp.transpose(jnp.where(_selected_cols(key_s[c], kidx, thr, cut), 0.0, NEG))[None]
        out = []
        for g in range(B_KV_HEADS):
            m, acc = carry[g]
            kg = k_ref[0, g, pl.ds(start, cw), :]
            vg = v_ref[0, g, pl.ds(start, cw), :]
            s = lax.dot_general(qg[g], kg, _NT, preferred_element_type=jnp.float32) * (B_DH ** -0.5 * LOG2E)
            s = s.reshape(rep, Q_BLOCK, cw) + bias
            m_new = jnp.maximum(m, jnp.max(s, axis=-1, keepdims=True))
            p = jnp.exp2((s - m_new).astype(jnp.bfloat16))
            pv = jnp.dot(p.reshape(rep * Q_BLOCK, cw), vg, preferred_element_type=jnp.float32)
            acc = jnp.exp2(m - m_new) * acc + pv.reshape(rep, Q_BLOCK, vg.shape[-1])
            out.append((m_new, acc))
        return tuple(out)

    vw = v_ref.shape[-1]
    init = tuple((jnp.full((rep, Q_BLOCK, 1), NEG, jnp.float32),
                  jnp.zeros((rep, Q_BLOCK, vw), jnp.float32)) for _ in range(B_KV_HEADS))
    res = lax.fori_loop(0, n_chunks, att_chunk, init)
    heads = []
    for g in range(B_KV_HEADS):
        acc = res[g][1]
        l = acc[:, :, B_DH:B_DH + 1]
        o = jnp.where(l > 0.0, acc[:, :, :B_DH] / jnp.where(l > 0.0, l, 1.0), 0.0)
        heads += [o[r] for r in range(rep)]
    o_ref[0] = jnp.concatenate(heads, axis=-1)


def dsa_prompt(q, kv, qi, ki, wi):
    bn, s, _ = q.shape
    topk = min(DSA_TOPK, s // 4)
    cw = DSA_CHUNK
    assert s % cw == 0 and s % Q_BLOCK == 0
    kcat = _split3_rhs(ki)
    k = jnp.transpose(kv[:, :, :, 0, :], (0, 2, 1, 3)).astype(jnp.bfloat16)
    v = _with_ones_column(jnp.transpose(kv[:, :, :, 1, :], (0, 2, 1, 3)).astype(jnp.bfloat16))
    qw = B_HEADS * B_DH
    return pl.pallas_call(
        functools.partial(_dsa_prompt_kernel, topk=topk),
        grid=(bn, s // Q_BLOCK),
        in_specs=[
            pl.BlockSpec((1, Q_BLOCK, IDX_HEADS * IDX_DH), lambda b, j: (b, j, 0)),
            pl.BlockSpec((1, IDX_HEADS, Q_BLOCK), lambda b, j: (b, 0, j)),
            pl.BlockSpec((1, s, 3 * IDX_DH), lambda b, j: (b, 0, 0)),
            pl.BlockSpec((1, Q_BLOCK, qw), lambda b, j: (b, j, 0)),
            pl.BlockSpec((1, B_KV_HEADS, s, B_DH), lambda b, j: (b, 0, 0, 0)),
            pl.BlockSpec((1, B_KV_HEADS, s, v.shape[-1]), lambda b, j: (b, 0, 0, 0)),
        ],
        out_specs=pl.BlockSpec((1, Q_BLOCK, qw), lambda b, j: (b, j, 0)),
        out_shape=jax.ShapeDtypeStruct((bn, s, qw), jnp.float32),
        scratch_shapes=[pltpu.VMEM((s // cw, cw, Q_BLOCK), jnp.int32)],
        compiler_params=pltpu.CompilerParams(dimension_semantics=("arbitrary", "arbitrary"),
                                             vmem_limit_bytes=VMEM_LIMIT_BYTES),
        name="dsa_prompt",
    )(qi, jnp.swapaxes(wi, 1, 2), kcat, q, k, v)


NSA_CHUNK = 1024
NSA_WIN_BLOCKS = WINDOW // Q_BLOCK + 1


def _nsa_prompt_kernel(qraw_ref, qrot_ref, gates_ref, kck_ref, kcv_ref, e_ref, ks_ref, vs_ref, *rest, n_sel):
    kw_refs = rest[0:NSA_WIN_BLOCKS]
    vw_refs = rest[NSA_WIN_BLOCKS:2 * NSA_WIN_BLOCKS]
    o_ref, key_s = rest[2 * NSA_WIN_BLOCKS:]
    cw = NSA_CHUNK
    rep = C_REP
    scale = C_DH ** -0.5
    j = pl.program_id(1)
    t0 = j * Q_BLOCK
    n_chunks = (t0 + Q_BLOCK + cw - 1) // cw
    qpos = t0 + lax.broadcasted_iota(jnp.int32, (Q_BLOCK, 1), 0)
    nbp = kck_ref.shape[2]
    blk = lax.broadcasted_iota(jnp.int32, (Q_BLOCK, nbp), 1)
    cmask = ((blk + 1) * CMP_BLOCK - 1) <= qpos
    cur = qpos // SEL_BLOCK
    forced = (blk == 0) | (blk == cur) | (blk == cur - 1)
    valid = blk * SEL_BLOCK <= qpos
    wlen = NSA_WIN_BLOCKS * Q_BLOCK
    wpos = t0 - WINDOW + lax.broadcasted_iota(jnp.int32, (Q_BLOCK, wlen), 1)
    wmask = ((wpos >= 0) & (wpos <= qpos) & (wpos > qpos - WINDOW))[None]

    qraw = qraw_ref[0]
    qrot = qrot_ref[0]
    sig = jax.nn.sigmoid(gates_ref[0])

    def stack(x, g):
        return jnp.concatenate([x[:, (g * rep + r) * C_DH:(g * rep + r + 1) * C_DH] for r in range(rep)], axis=0)

    o_cmps = []
    for g in range(C_KV):
        s = lax.dot_general(stack(qraw, g), kck_ref[0, g], _NT, precision=lax.Precision.HIGHEST,
                            preferred_element_type=jnp.float32) * scale
        s = jnp.where(cmask[None], s.reshape(rep, Q_BLOCK, nbp), NEG)
        e = jnp.exp(s - jnp.max(s, axis=-1, keepdims=True))
        p = e / jnp.sum(e, axis=-1, keepdims=True)
        p = p * jnp.where(jnp.max(jnp.where(cmask, 1.0, 0.0), axis=-1, keepdims=True) > 0.5, 1.0, 0.0)[None]
        o_cmps.append(jnp.dot(p.reshape(rep * Q_BLOCK, nbp), kcv_ref[0, g], precision=lax.Precision.HIGHEST,
                              preferred_element_type=jnp.float32).reshape(rep, Q_BLOCK, C_DH))
        imp = p[0]
        for r in range(1, rep):
            imp = imp + p[r]
        score = jnp.where(forced, FORCE, imp)
        key_s[0, :, g * Q_BLOCK:(g + 1) * Q_BLOCK] = jnp.transpose(
            jnp.where(valid, _f32_order_key(score), INT_MIN))

    thr, cut = _topk_threshold_cols(key_s, 1, n_sel, index_bits=int(math.log2(nbp)))
    blk_t = lax.broadcasted_iota(jnp.int32, (nbp, C_KV * Q_BLOCK), 0)
    sel_t = jnp.where(_selected_cols(key_s[0], blk_t, thr, cut), 1.0, 0.0)

    for g in range(C_KV):
        o_cmp = o_cmps[g]
        selblk = jnp.transpose(sel_t[:, g * Q_BLOCK:(g + 1) * Q_BLOCK]).astype(jnp.bfloat16)

        qg = stack(qrot, g).astype(jnp.bfloat16)

        def att_chunk(c, carry):
            m, acc = carry
            start = pl.multiple_of(c * cw, cw)
            kpos = start + lax.broadcasted_iota(jnp.int32, (Q_BLOCK, cw), 1)
            hit = jnp.dot(selblk, e_ref[c], preferred_element_type=jnp.float32)
            bias = jnp.where((hit > 0.5) & (kpos <= qpos), 0.0, NEG)[None]
            kg = ks_ref[0, g, pl.ds(start, cw), :]
            vg = vs_ref[0, g, pl.ds(start, cw), :]
            sc = lax.dot_general(qg, kg, _NT, preferred_element_type=jnp.float32) * (scale * LOG2E)
            sc = sc.reshape(rep, Q_BLOCK, cw) + bias
            m_new = jnp.maximum(m, jnp.max(sc, axis=-1, keepdims=True))
            pp = jnp.exp2((sc - m_new).astype(jnp.bfloat16))
            pv = jnp.dot(pp.reshape(rep * Q_BLOCK, cw), vg, preferred_element_type=jnp.float32)
            return m_new, jnp.exp2(m - m_new) * acc + pv.reshape(rep, Q_BLOCK, vg.shape[-1])

        init = (jnp.full((rep, Q_BLOCK, 1), NEG, jnp.float32),
                jnp.zeros((rep, Q_BLOCK, vs_ref.shape[-1]), jnp.float32))
        _, acc = lax.fori_loop(0, n_chunks, att_chunk, init)
        l = acc[:, :, C_DH:C_DH + 1]
        o_slc = jnp.where(l > 0.0, acc[:, :, :C_DH] / jnp.where(l > 0.0, l, 1.0), 0.0)

        kwin = jnp.concatenate([r_[0, g] for r_ in kw_refs], axis=0)
        vwin = jnp.concatenate([r_[0, g] for r_ in vw_refs], axis=0)
        sw = lax.dot_general(qg, kwin, _NT, preferred_element_type=jnp.float32) * scale
        sw = jnp.where(wmask, sw.reshape(rep, Q_BLOCK, wlen), NEG)
        ew = jnp.where(wmask, jnp.exp(sw - jnp.max(sw, axis=-1, keepdims=True)), 0.0)
        lw = jnp.sum(ew, axis=-1, keepdims=True)
        pw = ew / jnp.where(lw > 0.0, lw, 1.0)
        o_win = jnp.dot(pw.reshape(rep * Q_BLOCK, wlen).astype(jnp.bfloat16), vwin,
                        preferred_element_type=jnp.float32).reshape(rep, Q_BLOCK, C_DH)

        for r in range(rep):
            hh = g * rep + r
            o = (sig[:, 3 * hh:3 * hh + 1] * o_cmp[r] + sig[:, 3 * hh + 1:3 * hh + 2] * o_slc[r]
                 + sig[:, 3 * hh + 2:3 * hh + 3] * o_win[r])
            o_ref[0, :, hh * C_DH:(hh + 1) * C_DH] = o


def nsa_prompt(q_raw, q_rot, gates, kck, kcv, kv_s, kv_w):
    bn, s, _ = q_raw.shape
    cw = NSA_CHUNK
    assert s % cw == 0
    nb = kck.shape[2]
    nbs = -(-s // SEL_BLOCK)
    assert nb == nbs
    nbp = -(-nb // V7X_LANES) * V7X_LANES
    n_sel = min(N_SEL, nbs)
    kck = jnp.pad(kck, ((0, 0), (0, 0), (0, nbp - nb), (0, 0)))
    kcv = jnp.pad(kcv, ((0, 0), (0, 0), (0, nbp - nb), (0, 0)))
    split = lambda kv, c: jnp.transpose(kv[:, :, :, c, :], (0, 2, 1, 3)).astype(jnp.bfloat16)
    ks, vs, kw, vw = split(kv_s, 0), _with_ones_column(split(kv_s, 1)), split(kv_w, 0), split(kv_w, 1)
    expand = (jnp.arange(s, dtype=jnp.int32)[None, :] // SEL_BLOCK == jnp.arange(nbp, dtype=jnp.int32)[:, None])
    expand = jnp.transpose(expand.astype(jnp.bfloat16).reshape(nbp, s // cw, cw), (1, 0, 2))
    qd = C_HEADS * C_DH
    nq = s // Q_BLOCK
    wb = NSA_WIN_BLOCKS

    def win_spec(slot):
        return pl.BlockSpec((1, C_KV, Q_BLOCK, C_DH),
                            lambda b, j: (b, 0, jnp.maximum(j - (wb - 1) + slot, 0), 0))

    return pl.pallas_call(
        functools.partial(_nsa_prompt_kernel, n_sel=n_sel),
        grid=(bn, nq),
        in_specs=[
            pl.BlockSpec((1, Q_BLOCK, qd), lambda b, j: (b, j, 0)),
            pl.BlockSpec((1, Q_BLOCK, qd), lambda b, j: (b, j, 0)),
            pl.BlockSpec((1, Q_BLOCK, C_HEADS * 3), lambda b, j: (b, j, 0)),
            pl.BlockSpec((1, C_KV, nbp, C_DH), lambda b, j: (b, 0, 0, 0)),
            pl.BlockSpec((1, C_KV, nbp, C_DH), lambda b, j: (b, 0, 0, 0)),
            pl.BlockSpec((s // cw, nbp, cw), lambda b, j: (0, 0, 0)),
            pl.BlockSpec((1, C_KV, s, C_DH), lambda b, j: (b, 0, 0, 0)),
            pl.BlockSpec((1, C_KV, s, vs.shape[-1]), lambda b, j: (b, 0, 0, 0)),
        ] + [win_spec(i) for i in range(wb)] + [win_spec(i) for i in range(wb)],
        out_specs=pl.BlockSpec((1, Q_BLOCK, qd), lambda b, j: (b, j, 0)),
        out_shape=jax.ShapeDtypeStruct((bn, s, qd), jnp.float32),
        scratch_shapes=[pltpu.VMEM((1, nbp, C_KV * Q_BLOCK), jnp.int32)],
        compiler_params=pltpu.CompilerParams(dimension_semantics=("arbitrary", "arbitrary"),
                                             vmem_limit_bytes=VMEM_LIMIT_BYTES),
        name="nsa_prompt",
    )(q_raw, q_rot, gates, kck, kcv, expand, ks, vs, *([kw] * wb), *([vw] * wb))


S5_N = S5_GROUPS * S5_STATE
S5_TILES = 4
S5_TILE_IN = S5_WIDTH // S5_TILES
S5_TILE_N = S5_N // S5_TILES
S5_CHUNK = 256
S5_SUB = S5_CHUNK // V7X_SUBLANES


def _s5_input_map(u, bcat_ref, store):
    for i in range(S5_TILES):
        ucat = _split3_lhs(u[:, i * S5_TILE_IN:(i + 1) * S5_TILE_IN])
        r = jnp.dot(ucat, bcat_ref[i], preferred_element_type=jnp.float32)
        store(i, r[:, :S5_TILE_N], r[:, S5_TILE_N:])


def _s5_output_map(h_tile, u, ccat_ref, d_ref, wglu_ref):
    ys = []
    for i in range(S5_TILES):
        re, im = h_tile(i)
        hcat = jnp.concatenate([re, im], axis=-1).astype(jnp.bfloat16)
        ys.append(jnp.dot(hcat, ccat_ref[i], preferred_element_type=jnp.float32))
    y = jax.nn.gelu(jnp.concatenate(ys, axis=-1) + d_ref[...] * u)
    z = jnp.dot(y.astype(jnp.bfloat16), wglu_ref[...], preferred_element_type=jnp.float32)
    return y * jax.nn.sigmoid(z)


def _cmul_add(a_re, a_im, x_re, x_im, b_re, b_im):
    return a_re * x_re - a_im * x_im + b_re, a_re * x_im + a_im * x_re + b_im


def _s5_prompt_kernel(u_ref, h0_ref, bcat_ref, lam8_ref, lamm_ref, ccat_ref, d_ref, wglu_ref,
                      y_ref, hlast_ref, hs, ein_s, carry_s):
    n = S5_N
    m = S5_SUB
    c = pl.program_id(1)

    @pl.when(c == 0)
    def _():
        carry_s[...] = h0_ref[0]

    u = u_ref[0]

    def store_bu(i, re, im):
        hs[:, i * S5_TILE_N:(i + 1) * S5_TILE_N] = re
        hs[:, n + i * S5_TILE_N:n + (i + 1) * S5_TILE_N] = im

    _s5_input_map(u, bcat_ref, store_bu)

    def scan_body(k, h):
        row = pl.multiple_of(k * V7X_SUBLANES, V7X_SUBLANES)
        rows = pl.ds(row, V7X_SUBLANES)
        n_re, n_im = _cmul_add(lam8_ref[:, :n], lam8_ref[:, n:], h[0], h[1], hs[rows, :n], hs[rows, n:])
        hs[rows, :n] = n_re
        hs[rows, n:] = n_im
        return n_re, n_im

    zero = jnp.zeros((V7X_SUBLANES, n), jnp.float32)
    ends = lax.fori_loop(0, m, scan_body, (zero, zero))

    lm_re, lm_im = lamm_ref[:, :n], lamm_ref[:, n:]
    e_re, e_im = carry_s[:, :n], carry_s[:, n:]
    for s in range(V7X_SUBLANES):
        ein_s[s:s + 1, :n] = e_re
        ein_s[s:s + 1, n:] = e_im
        e_re, e_im = _cmul_add(lm_re, lm_im, e_re, e_im, ends[0][s:s + 1], ends[1][s:s + 1])
    carry_s[:, :n] = e_re
    carry_s[:, n:] = e_im
    hlast_ref[0] = carry_s[...]

    def fix_body(k, corr):
        row = pl.multiple_of(k * V7X_SUBLANES, V7X_SUBLANES)
        rows = pl.ds(row, V7X_SUBLANES)
        c_re, c_im = _cmul_add(lam8_ref[:, :n], lam8_ref[:, n:], corr[0], corr[1], 0.0, 0.0)
        hs[rows, :n] = hs[rows, :n] + c_re
        hs[rows, n:] = hs[rows, n:] + c_im
        return c_re, c_im

    lax.fori_loop(0, m, fix_body, (ein_s[:, :n], ein_s[:, n:]))

    def h_tile(i):
        return hs[:, i * S5_TILE_N:(i + 1) * S5_TILE_N], hs[:, n + i * S5_TILE_N:n + (i + 1) * S5_TILE_N]

    y_ref[0] = _s5_output_map(h_tile, u, ccat_ref, d_ref, wglu_ref)


def _s5_sample_kernel(u_ref, h0_ref, bcat_ref, lam_ref, ccat_ref, d_ref, wglu_ref, y_ref, hlast_ref, hs):
    n = S5_N
    steps, bd, _ = u_ref.shape
    h_re, h_im = h0_ref[:, :n], h0_ref[:, n:]
    lam_re, lam_im = lam_ref[:, :n], lam_ref[:, n:]
    for t in range(steps):
        u = u_ref[t]

        def store_bu(i, re, im):
            hs[:, i * S5_TILE_N:(i + 1) * S5_TILE_N] = re
            hs[:, n + i * S5_TILE_N:n + (i + 1) * S5_TILE_N] = im

        _s5_input_map(u, bcat_ref, store_bu)
        h_re, h_im = _cmul_add(lam_re, lam_im, h_re, h_im, hs[:, :n], hs[:, n:])
        hs[:, :n] = h_re
        hs[:, n:] = h_im

        def h_tile(i):
            return hs[:, i * S5_TILE_N:(i + 1) * S5_TILE_N], hs[:, n + i * S5_TILE_N:n + (i + 1) * S5_TILE_N]

        y_ref[t] = _s5_output_map(h_tile, u, ccat_ref, d_ref, wglu_ref)
    hlast_ref[:, :n] = h_re
    hlast_ref[:, n:] = h_im


def _s5_params(a_re, a_im, log_dt, b_re, b_im, c_re, c_im):
    lam = lax.complex(a_re, a_im)
    dt = jnp.exp(log_dt)[:, None]
    lam_bar = jnp.exp(lam * dt)
    b_bar = ((lam_bar - 1.0) / lam)[:, :, None] * lax.complex(b_re, b_im)
    gpt = S5_GROUPS // S5_TILES
    eye = jnp.eye(gpt, dtype=jnp.float32)

    def in_blockdiag(w):
        w = w.reshape(S5_TILES, gpt, S5_STATE, S5_GROUP)
        return jnp.einsum('tgpc,gh->tgchp', w, eye).reshape(S5_TILES, S5_TILE_IN, S5_TILE_N)

    def out_blockdiag(w):
        w = w.reshape(S5_TILES, gpt, S5_GROUP, S5_STATE)
        return jnp.einsum('tgcp,gh->tgphc', w, eye).reshape(S5_TILES, S5_TILE_N, S5_TILE_IN)

    b_full = jnp.concatenate([in_blockdiag(b_bar.real), in_blockdiag(b_bar.imag)], axis=-1)
    b_hi = b_full.astype(jnp.bfloat16)
    b_lo = (b_full - b_hi.astype(jnp.float32)).astype(jnp.bfloat16)
    bcat = jnp.concatenate([b_hi, b_lo, b_hi], axis=1)
    ccat = jnp.concatenate([out_blockdiag(c_re), out_blockdiag(-c_im)], axis=1).astype(jnp.bfloat16)
    flat = lambda z: jnp.concatenate([z.real.reshape(1, S5_N), z.imag.reshape(1, S5_N)], axis=-1)
    lam_row = flat(lam_bar)
    lamm_row = flat(jnp.exp(lam * dt * S5_SUB))
    return bcat, ccat, lam_row, lamm_row


def s5_prompt(u, a_re, a_im, log_dt, b_re, b_im, c_re, c_im, d, w_glu):
    bn, s, _ = u.shape
    tc, m = S5_CHUNK, S5_SUB
    assert s % tc == 0
    nc = s // tc
    bcat, ccat, lam_row, lamm_row = _s5_params(a_re, a_im, log_dt, b_re, b_im, c_re, c_im)
    lam8 = jnp.broadcast_to(lam_row, (V7X_SUBLANES, 2 * S5_N))
    to_ks = lambda a: a.reshape(bn, nc, V7X_SUBLANES, m, S5_WIDTH).swapaxes(2, 3).reshape(bn, s, S5_WIDTH)
    from_ks = lambda a: a.reshape(bn, nc, m, V7X_SUBLANES, S5_WIDTH).swapaxes(2, 3).reshape(bn, s, S5_WIDTH)
    h0 = jnp.zeros((bn, 1, 2 * S5_N), jnp.float32)
    const = lambda shape: pl.BlockSpec(shape, lambda b, c: (0,) * len(shape))
    y, hlast = pl.pallas_call(
        _s5_prompt_kernel,
        grid=(bn, nc),
        in_specs=[
            pl.BlockSpec((1, tc, S5_WIDTH), lambda b, c: (b, c, 0)),
            pl.BlockSpec((1, 1, 2 * S5_N), lambda b, c: (b, 0, 0)),
            const(bcat.shape), const(lam8.shape), const(lamm_row.shape), const(ccat.shape),
            const((1, S5_WIDTH)), const((S5_WIDTH, S5_WIDTH)),
        ],
        out_specs=[pl.BlockSpec((1, tc, S5_WIDTH), lambda b, c: (b, c, 0)),
                   pl.BlockSpec((1, 1, 2 * S5_N), lambda b, c: (b, 0, 0))],
        out_shape=[jax.ShapeDtypeStruct((bn, s, S5_WIDTH), jnp.float32),
                   jax.ShapeDtypeStruct((bn, 1, 2 * S5_N), jnp.float32)],
        scratch_shapes=[pltpu.VMEM((tc, 2 * S5_N), jnp.float32),
                        pltpu.VMEM((V7X_SUBLANES, 2 * S5_N), jnp.float32),
                        pltpu.VMEM((1, 2 * S5_N), jnp.float32)],
        compiler_params=pltpu.CompilerParams(dimension_semantics=("arbitrary", "arbitrary"),
                                             vmem_limit_bytes=VMEM_LIMIT_BYTES),
        name="s5_prompt",
    )(to_ks(u), h0, bcat, lam8, lamm_row, ccat, d.reshape(1, S5_WIDTH), w_glu.astype(jnp.bfloat16))
    return from_ks(y), hlast.reshape(bn, 2, S5_GROUPS, S5_STATE)


def s5_sample(u, state, a_re, a_im, log_dt, b_re, b_im, c_re, c_im, d, w_glu):
    bd, steps, _ = u.shape
    bcat, ccat, lam_row, _ = _s5_params(a_re, a_im, log_dt, b_re, b_im, c_re, c_im)
    y, hlast = pl.pallas_call(
        _s5_sample_kernel,
        out_shape=[jax.ShapeDtypeStruct((steps, bd, S5_WIDTH), jnp.float32),
                   jax.ShapeDtypeStruct((bd, 2 * S5_N), jnp.float32)],
        scratch_shapes=[pltpu.VMEM((bd, 2 * S5_N), jnp.float32)],
        compiler_params=pltpu.CompilerParams(vmem_limit_bytes=VMEM_LIMIT_BYTES),
        name="s5_sample",
    )(jnp.swapaxes(u, 0, 1), state.reshape(bd, 2 * S5_N), bcat, lam_row, ccat,
      d.reshape(1, S5_WIDTH), w_glu.astype(jnp.bfloat16))
    return jnp.swapaxes(y, 0, 1), hlast.reshape(bd, 2, S5_GROUPS, S5_STATE)


SAMPLE_PAGES_PER_STEP = 16


def _dsa_sample_kernel(pt_ref, qi_ref, wi_ref, q_ref, kinew_ref, kvnew_ref, *rest, topk, past_len):
    pg = SAMPLE_PAGES_PER_STEP
    idx_refs = rest[0:pg]
    kv_refs = rest[pg:2 * pg]
    o_ref, key_s, kv_all = rest[2 * pg:]
    cw = pg * PAGE_SIZE
    n_steps = past_len // cw
    steps_q = q_ref.shape[1]
    rows = key_s.shape[1]
    rep = B_HEADS // B_KV_HEADS
    kvw = B_KV_HEADS * 2 * B_DH
    s_id = pl.program_id(1)
    qpos = past_len + lax.broadcasted_iota(jnp.int32, (rows, 1), 0)

    qi = qi_ref[0]
    wi = wi_ref[0]
    qcat = jnp.concatenate([_split3_lhs(qi[:, h * IDX_DH:(h + 1) * IDX_DH]) for h in range(IDX_HEADS)], axis=0)
    row_ok = lax.broadcasted_iota(jnp.int32, (rows, 1), 0) < steps_q

    def scores(kidx_rows, kpos):
        s = lax.dot_general(qcat, _split3_rhs(kidx_rows), _NT, preferred_element_type=jnp.float32) * (IDX_DH ** -0.5)
        acc = None
        for h in range(IDX_HEADS):
            term = wi[:, h:h + 1] * jnp.maximum(s[h * steps_q:(h + 1) * steps_q], 0.0)
            acc = term if acc is None else acc + term
        acc = jnp.concatenate([acc, jnp.zeros((rows - steps_q, acc.shape[1]), jnp.float32)], axis=0)
        return jnp.where((kpos <= qpos) & row_ok, _f32_order_key(acc), INT_MIN)

    kpos = s_id * cw + lax.broadcasted_iota(jnp.int32, (rows, cw), 1)
    key_s[s_id] = scores(jnp.concatenate([r[0] for r in idx_refs], axis=0), kpos)
    for i in range(pg):
        start = pl.multiple_of(s_id * cw + i * PAGE_SIZE, PAGE_SIZE)
        kv_all[pl.ds(start, PAGE_SIZE), :] = kv_refs[i][0].astype(jnp.bfloat16)

    @pl.when(s_id == n_steps - 1)
    def _():
        pad = jnp.zeros((cw - steps_q, IDX_DH), jnp.float32)
        kpos_new = past_len + lax.broadcasted_iota(jnp.int32, (rows, cw), 1)
        new_key = scores(jnp.concatenate([kinew_ref[0], pad], axis=0), kpos_new)
        lane = lax.broadcasted_iota(jnp.int32, (rows, cw), 1)
        key_s[n_steps] = jnp.where(lane < steps_q, new_key, INT_MIN)
        kv_all[pl.ds(past_len, cw), :] = jnp.concatenate(
            [kvnew_ref[0], jnp.zeros((cw - steps_q, kvw), jnp.float32)], axis=0).astype(jnp.bfloat16)
        n_chunks = n_steps + 1
        thr, cut = _topk_threshold(key_s, n_chunks, topk, index_bits=int(math.ceil(math.log2(n_chunks * cw))))

        q = q_ref[0]
        qrows = []
        for h in range(B_HEADS):
            g = h // rep
            qh = q[:, h * B_DH:(h + 1) * B_DH]
            pieces = [jnp.zeros((steps_q, g * 2 * B_DH), jnp.float32)] if g else []
            pieces += [qh, jnp.zeros((steps_q, kvw - g * 2 * B_DH - B_DH), jnp.float32)]
            qrows.append(jnp.concatenate(pieces, axis=-1))
        qx = jnp.concatenate(qrows, axis=0).astype(jnp.bfloat16)
        nr = B_HEADS * steps_q

        def att_chunk(c, carry):
            m, l, acc = carry
            start = pl.multiple_of(c * cw, cw)
            kidx = start + lax.broadcasted_iota(jnp.int32, (rows, cw), 1)
            sel = _selected(key_s[c], kidx, thr, cut)[0:steps_q][None]
            kvc = kv_all[pl.ds(start, cw), :]
            s = lax.dot_general(qx, kvc, _NT, preferred_element_type=jnp.float32) * (B_DH ** -0.5)
            s = jnp.where(sel, s.reshape(B_HEADS, steps_q, cw), NEG)
            m_new = jnp.maximum(m, jnp.max(s, axis=-1, keepdims=True))
            p = jnp.where(sel, jnp.exp(s - m_new), 0.0)
            alpha = jnp.exp(m - m_new)
            l = alpha * l + jnp.sum(p, axis=-1, keepdims=True)
            pv = jnp.dot(p.reshape(nr, cw).astype(jnp.bfloat16), kvc, preferred_element_type=jnp.float32)
            return m_new, l, alpha * acc + pv.reshape(B_HEADS, steps_q, kvw)

        init = (jnp.full((B_HEADS, steps_q, 1), NEG, jnp.float32), jnp.zeros((B_HEADS, steps_q, 1), jnp.float32),
                jnp.zeros((B_HEADS, steps_q, kvw), jnp.float32))
        _, l, acc = lax.fori_loop(0, n_chunks, att_chunk, init)
        o = jnp.where(l > 0.0, acc / jnp.where(l > 0.0, l, 1.0), 0.0)
        heads = []
        for h in range(B_HEADS):
            g = h // rep
            heads.append(o[h][:, g * 2 * B_DH + B_DH:(g + 1) * 2 * B_DH])
        o_ref[0] = jnp.concatenate(heads, axis=-1)


def dsa_sample(q, kv, qi, ki, wi, cache_kv, cache_idx, page_table):
    bd, steps_q, _ = q.shape
    n_pages = page_table.shape[1]
    past_len = n_pages * PAGE_SIZE
    pg = SAMPLE_PAGES_PER_STEP
    assert n_pages % pg == 0
    n_steps = n_pages // pg
    cw = pg * PAGE_SIZE
    topk = min(DSA_TOPK, (past_len + steps_q) // 4)
    kvw = B_KV_HEADS * 2 * B_DH
    n_pool = cache_kv.shape[0]
    rows = V7X_SUBLANES
    assert steps_q <= rows

    def page_spec(i, width):
        return pl.BlockSpec((1, PAGE_SIZE, width), lambda b, s, pt: (pt[b * n_pages + s * pg + i], 0, 0))

    per_b = lambda shape: pl.BlockSpec((1,) + shape, lambda b, s, pt: (b, 0, 0))
    grid_spec = pltpu.PrefetchScalarGridSpec(
        num_scalar_prefetch=1,
        grid=(bd, n_steps),
        in_specs=[per_b((steps_q, IDX_HEADS * IDX_DH)), per_b((steps_q, IDX_HEADS)), per_b((steps_q, B_HEADS * B_DH)),
                  per_b((steps_q, IDX_DH)), per_b((steps_q, kvw))]
        + [page_spec(i, IDX_DH) for i in range(pg)] + [page_spec(i, kvw) for i in range(pg)],
        out_specs=per_b((steps_q, B_HEADS * B_DH)),
        scratch_shapes=[pltpu.VMEM((n_steps + 1, rows, cw), jnp.int32),
                        pltpu.VMEM((past_len + cw, kvw), jnp.bfloat16)],
    )
    return pl.pallas_call(
        functools.partial(_dsa_sample_kernel, topk=topk, past_len=past_len),
        grid_spec=grid_spec,
        out_shape=jax.ShapeDtypeStruct((bd, steps_q, B_HEADS * B_DH), jnp.float32),
        compiler_params=pltpu.CompilerParams(dimension_semantics=("arbitrary", "arbitrary"),
                                             vmem_limit_bytes=VMEM_LIMIT_BYTES),
        name="dsa_sample",
    )(page_table.reshape(-1), qi, wi, q, ki, kv.reshape(bd, steps_q, kvw),
      *([cache_idx] * pg), *([cache_kv.reshape(n_pool, PAGE_SIZE, kvw)] * pg))


def _nsa_compress_kernel(pt_ref, cmp_pos_ref, w1_ref, w2_ref, *rest):
    pg = SAMPLE_PAGES_PER_STEP
    page_refs = rest[:pg]
    kck_ref, kcv_ref, xs = rest[pg:]
    bpp = PAGE_SIZE // CMP_BLOCK
    nblk = pg * bpp
    slabs = C_KV * 2
    prow = PAGE_SIZE * slabs
    for i in range(pg):
        xs[i * prow:(i + 1) * prow, :] = page_refs[i][0]
    for c, out_ref in enumerate((kck_ref, kcv_ref)):
        acc = jnp.zeros((C_KV * nblk, C_DH), jnp.float32)
        for pp in range(CMP_BLOCK // 2):
            cols = []
            for pos in (2 * pp, 2 * pp + 1):
                bias = cmp_pos_ref[pos:pos + 1, c * C_DH:(c + 1) * C_DH]
                cols.append(jnp.concatenate(
                    [xs[pl.ds(pos * slabs + g * 2 + c, nblk, stride=CMP_BLOCK * slabs), :] + bias
                     for g in range(C_KV)], axis=0))
            lhs = jnp.concatenate(cols, axis=-1).astype(jnp.bfloat16)
            acc = acc + jnp.dot(lhs, w1_ref[c, pp], preferred_element_type=jnp.float32)
        kc = jnp.dot(jax.nn.gelu(acc).astype(jnp.bfloat16), w2_ref[c], preferred_element_type=jnp.float32)
        for g in range(C_KV):
            out_ref[0, g] = kc[g * nblk:(g + 1) * nblk]


def nsa_compress(pages, page_table, cmp_pos, cmp_w1, cmp_w2):
    bn, n_pages = page_table.shape
    pg = SAMPLE_PAGES_PER_STEP
    assert n_pages % pg == 0
    n_pool = pages.shape[0]
    prow = PAGE_SIZE * C_KV * 2
    bpp = PAGE_SIZE // CMP_BLOCK
    nb = n_pages * bpp
    w1 = cmp_w1.reshape(2, CMP_BLOCK // 2, 2 * C_DH, C_DH).astype(jnp.bfloat16)
    w2 = cmp_w2.astype(jnp.bfloat16)

    def page_spec(i):
        return pl.BlockSpec((1, prow, C_DH), lambda b, s, pt: (pt[b * n_pages + s * pg + i], 0, 0))

    const = lambda shape: pl.BlockSpec(shape, lambda b, s, pt: (0,) * len(shape))
    out_spec = pl.BlockSpec((1, C_KV, pg * bpp, C_DH), lambda b, s, pt: (b, 0, s, 0))
    grid_spec = pltpu.PrefetchScalarGridSpec(
        num_scalar_prefetch=1,
        grid=(bn, n_pages // pg),
        in_specs=[const((CMP_BLOCK, 2 * C_DH)), const(w1.shape), const(w2.shape)] + [page_spec(i) for i in range(pg)],
        out_specs=[out_spec, out_spec],
        scratch_shapes=[pltpu.VMEM((pg * prow, C_DH), jnp.float32)],
    )
    shape = jax.ShapeDtypeStruct((bn, C_KV, nb, C_DH), jnp.float32)
    return pl.pallas_call(
        _nsa_compress_kernel,
        grid_spec=grid_spec,
        out_shape=[shape, shape],
        compiler_params=pltpu.CompilerParams(dimension_semantics=("arbitrary", "arbitrary"),
                                             vmem_limit_bytes=VMEM_LIMIT_BYTES),
        name="nsa_compress",
    )(page_table.reshape(-1), cmp_pos.reshape(CMP_BLOCK, 2 * C_DH), w1, w2,
      *([pages.reshape(n_pool, prow, C_DH)] * pg))


def _nsa_sample_attend_kernel(qraw_ref, qrot_ref, kck_ref, kcv_ref, win_ref, kvw_ref, tri_ref,
                              ocmp_ref, owin_ref, sel_ref, key_s, *, past_len, n_sel):
    rep = C_REP
    scale = C_DH ** -0.5
    steps_q = qraw_ref.shape[1]
    rows = key_s.shape[1]
    nbp = key_s.shape[2]
    nb = kck_ref.shape[2]
    wbuf = win_ref.shape[1] // (C_KV * 2)
    qpos = past_len + lax.broadcasted_iota(jnp.int32, (steps_q, 1), 0)
    blk_c = lax.broadcasted_iota(jnp.int32, (steps_q, nb), 1)
    cmask = ((blk_c + 1) * CMP_BLOCK - 1) <= qpos
    any_c = jnp.where(jnp.max(jnp.where(cmask, 1.0, 0.0), axis=-1, keepdims=True) > 0.5, 1.0, 0.0)
    blk = lax.broadcasted_iota(jnp.int32, (steps_q, nbp), 1)
    cur = qpos // SEL_BLOCK
    forced = (blk == 0) | (blk == cur) | (blk == cur - 1)
    nbs = -(-(past_len + steps_q) // SEL_BLOCK)
    valid = (blk * SEL_BLOCK <= qpos) & (blk < nbs)
    lane = lax.broadcasted_iota(jnp.int32, (rows, nbp), 1)
    wlen = wbuf + V7X_SUBLANES
    wcol = lax.broadcasted_iota(jnp.int32, (steps_q, wlen), 1)
    wpos = jnp.where(wcol < wbuf, past_len - wbuf + wcol, past_len + wcol - wbuf)
    wmask = ((wpos >= 0) & (wpos <= qpos) & (wpos > qpos - WINDOW) & (wcol < wbuf + steps_q))[None]
    qraw = qraw_ref[0]
    qrot = qrot_ref[0]

    def stack(x, g):
        return jnp.concatenate([x[:, (g * rep + r) * C_DH:(g * rep + r + 1) * C_DH] for r in range(rep)], axis=0)

    pad_rows = lambda x: jnp.concatenate([x, jnp.zeros((rows - steps_q,) + x.shape[1:], x.dtype)], axis=0)

    for g in range(C_KV):
        s = lax.dot_general(stack(qraw, g), kck_ref[0, g], _NT, precision=lax.Precision.HIGHEST,
                            preferred_element_type=jnp.float32) * scale
        s = jnp.where(cmask[None], s.reshape(rep, steps_q, nb), NEG)
        e = jnp.exp(s - jnp.max(s, axis=-1, keepdims=True))
        p = e / jnp.sum(e, axis=-1, keepdims=True) * any_c[None]
        o_cmp = jnp.dot(p.reshape(rep * steps_q, nb), kcv_ref[0, g], precision=lax.Precision.HIGHEST,
                        preferred_element_type=jnp.float32)
        imp = p[0]
        for r in range(1, rep):
            imp = imp + p[r]
        imp = jnp.concatenate([imp, jnp.zeros((steps_q, nbp - nb), jnp.float32)], axis=-1)
        score = jnp.where(forced, FORCE, imp)
        key_s[0] = pad_rows(jnp.where(valid, _f32_order_key(score), INT_MIN))
        thr, cut = _topk_threshold(key_s, 1, n_sel, index_bits=int(math.ceil(math.log2(nbp))))
        mask = _selected(key_s[0], lane, thr, cut)
        prefix = jnp.dot(jnp.where(mask, 1.0, 0.0).astype(jnp.bfloat16), tri_ref[...],
                         preferred_element_type=jnp.float32)
        ids = [jnp.sum(jnp.where(mask & (prefix == float(t + 1)), lane, 0), axis=-1, keepdims=True)
               for t in range(n_sel)]
        ids.append(jnp.zeros((rows, V7X_LANES - n_sel), jnp.int32))
        sel_ref[0, g] = jnp.concatenate(ids, axis=-1)

        qg = stack(qrot, g).astype(jnp.bfloat16)
        newkv = kvw_ref[0]
        padn = jnp.zeros((V7X_SUBLANES - steps_q, C_DH), jnp.float32)
        slabs = C_KV * 2
        kwin = jnp.concatenate([win_ref.at[0][pl.ds(2 * g, wbuf, stride=slabs), :],
                                newkv[:, (2 * g) * C_DH:(2 * g + 1) * C_DH], padn], axis=0).astype(jnp.bfloat16)
        vwin = jnp.concatenate([win_ref.at[0][pl.ds(2 * g + 1, wbuf, stride=slabs), :],
                                newkv[:, (2 * g + 1) * C_DH:(2 * g + 2) * C_DH], padn], axis=0).astype(jnp.bfloat16)
        sw = lax.dot_general(qg, kwin, _NT, preferred_element_type=jnp.float32) * scale
        sw = jnp.where(wmask, sw.reshape(rep, steps_q, wlen), NEG)
        ew = jnp.where(wmask, jnp.exp(sw - jnp.max(sw, axis=-1, keepdims=True)), 0.0)
        lw = jnp.sum(ew, axis=-1, keepdims=True)
        pw = ew / jnp.where(lw > 0.0, lw, 1.0)
        o_win = jnp.dot(pw.reshape(rep * steps_q, wlen).astype(jnp.bfloat16), vwin, preferred_element_type=jnp.float32)
        for r in range(rep):
            hh = g * rep + r
            ocmp_ref[0, :, hh * C_DH:(hh + 1) * C_DH] = o_cmp[r * steps_q:(r + 1) * steps_q]
            owin_ref[0, :, hh * C_DH:(hh + 1) * C_DH] = o_win[r * steps_q:(r + 1) * steps_q]


def nsa_sample_attend(q_raw, q_rot, kck, kcv, win, kv_w):
    bd, steps_q, qd = q_raw.shape
    nb = kck.shape[2]
    past_len = nb * CMP_BLOCK
    nbs = -(-(past_len + steps_q) // SEL_BLOCK)
    assert nbs >= N_SEL and steps_q <= V7X_SUBLANES
    nbp = -(-nbs // V7X_LANES) * V7X_LANES
    wbuf = win.shape[1]
    roww = C_KV * 2 * C_DH
    tri = (jnp.arange(nbp)[:, None] <= jnp.arange(nbp)[None, :]).astype(jnp.bfloat16)
    per_b = lambda shape: pl.BlockSpec((1,) + shape, lambda b: (b,) + (0,) * len(shape))
    return pl.pallas_call(
        functools.partial(_nsa_sample_attend_kernel, past_len=past_len, n_sel=N_SEL),
        grid=(bd,),
        in_specs=[per_b((steps_q, qd)), per_b((steps_q, qd)), per_b((C_KV, nb, C_DH)), per_b((C_KV, nb, C_DH)),
                  per_b((wbuf * C_KV * 2, C_DH)), per_b((steps_q, roww)), pl.BlockSpec((nbp, nbp), lambda b: (0, 0))],
        out_specs=[per_b((steps_q, qd)), per_b((steps_q, qd)), per_b((C_KV, V7X_SUBLANES, V7X_LANES))],
        out_shape=[jax.ShapeDtypeStruct((bd, steps_q, qd), jnp.float32),
                   jax.ShapeDtypeStruct((bd, steps_q, qd), jnp.float32),
                   jax.ShapeDtypeStruct((bd, C_KV, V7X_SUBLANES, V7X_LANES), jnp.int32)],
        scratch_shapes=[pltpu.VMEM((1, V7X_SUBLANES, nbp), jnp.int32)],
        compiler_params=pltpu.CompilerParams(dimension_semantics=("arbitrary",), vmem_limit_bytes=VMEM_LIMIT_BYTES),
        name="nsa_sample_attend",
    )(q_raw, q_rot, kck, kcv, win.reshape(bd, wbuf * C_KV * 2, C_DH), kv_w.reshape(bd, steps_q, roww), tri)


def _nsa_sample_selected_kernel(phys_ref, blk_ref, q_ref, gates_ref, ocmp_ref, owin_ref, new_ref, *rest,
                                past_len, steps_q):
    blk_refs = rest[:C_KV * N_SEL]
    o_ref = rest[C_KV * N_SEL]
    i = pl.program_id(0)
    qpos = past_len + i % steps_q
    scale = C_DH ** -0.5
    slabs = C_KV * 2
    row = lax.broadcasted_iota(jnp.int32, (1, SEL_BLOCK), 1)
    ncol = lax.broadcasted_iota(jnp.int32, (1, V7X_SUBLANES), 1)
    nb_past = past_len // SEL_BLOCK
    padn = jnp.zeros((V7X_SUBLANES - steps_q, C_DH), jnp.float32)
    sig = jax.nn.sigmoid(gates_ref[0])
    newkv = new_ref[0]
    for g in range(C_KV):
        q = q_ref[0, g * C_REP:(g + 1) * C_REP, :].astype(jnp.bfloat16)
        ks, vs, kpos = [], [], []
        has_new = jnp.int32(0)
        for t in range(N_SEL):
            b_id = blk_ref[(i * C_KV + g) * N_SEL + t]
            is_new = b_id >= nb_past
            has_new = jnp.maximum(has_new, is_new.astype(jnp.int32))
            ref = blk_refs[g * N_SEL + t].at[0]
            ks.append(ref[pl.ds(2 * g, SEL_BLOCK, stride=slabs), :])
            vs.append(ref[pl.ds(2 * g + 1, SEL_BLOCK, stride=slabs), :])
            kpos.append(jnp.where(is_new, jnp.int32(INT_MAX), b_id * SEL_BLOCK + row))
        ks.append(jnp.concatenate([newkv[:, (2 * g) * C_DH:(2 * g + 1) * C_DH], padn], axis=0))
        vs.append(jnp.concatenate([newkv[:, (2 * g + 1) * C_DH:(2 * g + 2) * C_DH], padn], axis=0))
        kpos.append(jnp.where((has_new > 0) & (ncol < steps_q), past_len + ncol, jnp.int32(INT_MAX)))
        k = jnp.concatenate(ks, axis=0).astype(jnp.bfloat16)
        v = jnp.concatenate(vs, axis=0).astype(jnp.bfloat16)
        mask = jnp.concatenate(kpos, axis=-1) <= qpos
        s = lax.dot_general(q, k, _NT, preferred_element_type=jnp.float32) * scale
        s = jnp.where(mask, s, NEG)
        e = jnp.where(mask, jnp.exp(s - jnp.max(s, axis=-1, keepdims=True)), 0.0)
        l = jnp.sum(e, axis=-1, keepdims=True)
        p = e / jnp.where(l > 0.0, l, 1.0)
        o_slc = jnp.dot(p.astype(jnp.bfloat16), v, preferred_element_type=jnp.float32)
        hs = slice(g * C_REP, (g + 1) * C_REP)
        o_ref[0, hs, :] = (sig[hs, 0:1] * ocmp_ref[0, hs, :] + sig[hs, 1:2] * o_slc
                           + sig[hs, 2:3] * owin_ref[0, hs, :])


def nsa_sample_selected(sel_idx, q_rot, gates, o_cmp, o_win, kv_s, cache_slc, page_table):
    bd, steps_q, qd = q_rot.shape
    n_pages = page_table.shape[1]
    past_len = n_pages * PAGE_SIZE
    n_pool = cache_slc.shape[0]
    bpp = PAGE_SIZE // SEL_BLOCK
    slabs = C_KV * 2
    blk = jnp.transpose(sel_idx[:, :, :steps_q, :N_SEL], (0, 2, 1, 3))
    page = jnp.minimum(blk // bpp, n_pages - 1)
    phys = jnp.take_along_axis(page_table, page.reshape(bd, -1), axis=1).reshape(blk.shape) * bpp + blk % bpp
    n = bd * steps_q
    heads = lambda a: a.reshape(n, C_HEADS, C_DH)

    def blk_spec(g, t):
        return pl.BlockSpec((1, SEL_BLOCK * slabs, C_DH),
                            lambda i, ph, bl: (ph[(i * C_KV + g) * N_SEL + t], 0, 0))

    row_spec = lambda shape: pl.BlockSpec((1,) + shape, lambda i, ph, bl: (i,) + (0,) * len(shape))
    grid_spec = pltpu.PrefetchScalarGridSpec(
        num_scalar_prefetch=2,
        grid=(n,),
        in_specs=[row_spec((C_HEADS, C_DH)), row_spec((C_HEADS, 3)), row_spec((C_HEADS, C_DH)),
                  row_spec((C_HEADS, C_DH)),
                  pl.BlockSpec((1, steps_q, slabs * C_DH), lambda i, ph, bl: (i // steps_q, 0, 0))]
        + [blk_spec(g, t) for g in range(C_KV) for t in range(N_SEL)],
        out_specs=row_spec((C_HEADS, C_DH)),
    )
    o = pl.pallas_call(
        functools.partial(_nsa_sample_selected_kernel, past_len=past_len, steps_q=steps_q),
        grid_spec=grid_spec,
        out_shape=jax.ShapeDtypeStruct((n, C_HEADS, C_DH), jnp.float32),
        compiler_params=pltpu.CompilerParams(dimension_semantics=("arbitrary",), vmem_limit_bytes=VMEM_LIMIT_BYTES),
        name="nsa_sample_selected",
    )(phys.reshape(-1).astype(jnp.int32), blk.reshape(-1).astype(jnp.int32),
      heads(q_rot), gates.reshape(n, C_HEADS, 3), heads(o_cmp), heads(o_win),
      kv_s.reshape(bd, steps_q, slabs * C_DH),
      *([cache_slc.reshape(n_pool * bpp, SEL_BLOCK * slabs, C_DH)] * (C_KV * N_SEL)))
    return o.reshape(bd, steps_q, qd)


def split_cols(h, sizes):
    out, start = [], 0
    for n in sizes:
        out.append(h[..., start:start + n])
        start += n
    return out


def layer_norm(x, g, b):
    mu = x.mean(-1, keepdims=True)
    var = jnp.square(x - mu).mean(-1, keepdims=True)
    return (x - mu) * lax.rsqrt(var + LN_EPS) * g + b


def rope_partial(x, pos, rot_dim):
    half = rot_dim // 2
    inv = ROPE_THETA ** (-jnp.arange(half, dtype=jnp.float32) / half)
    ang = pos.astype(jnp.float32)[:, None] * inv
    cos, sin = jnp.cos(ang)[:, None, :], jnp.sin(ang)[:, None, :]
    x1 = x[..., :half]
    x2 = x[..., half:rot_dim]
    rot = jnp.concatenate([x1 * cos - x2 * sin, x2 * cos + x1 * sin], -1)
    return jnp.concatenate([rot, x[..., rot_dim:]], -1)


def rope_kv(kv, pos, rot_dim):
    k = rope_partial(kv[..., 0, :], pos, rot_dim)
    return jnp.stack([k, kv[..., 1, :]], axis=-2)


def mixer_ab(x, pos, w_in, a_re, a_im, log_dt, b_re, b_im, c_re, c_im, d, w_glu, w_out,
             state=None, cache_kv=None, cache_idx=None, page_table=None):
    bn, s, _ = x.shape
    s5p = (a_re, a_im, log_dt, b_re, b_im, c_re, c_im, d, w_glu)
    u, q, kv, qi, ki, wi = split_cols(x @ w_in, AB_SIZES)
    q = rope_partial(q.reshape(bn, s, B_HEADS, B_DH), pos, B_ROT)
    kv = rope_kv(kv.reshape(bn, s, B_KV_HEADS, 2, B_DH), pos, B_ROT)
    qi = rope_partial(qi.reshape(bn, s, IDX_HEADS, IDX_DH), pos, IDX_ROT)
    ki = rope_partial(ki.reshape(bn, s, 1, IDX_DH), pos, IDX_ROT)[:, :, 0]
    if state is None:
        o_b = dsa_prompt(q.reshape(bn, s, B_HEADS * B_DH), kv, qi.reshape(bn, s, IDX_HEADS * IDX_DH), ki, wi)
        y_s5, new_a = s5_prompt(u, *s5p)
    else:
        o_b = dsa_sample(q.reshape(bn, s, B_HEADS * B_DH), kv, qi.reshape(bn, s, IDX_HEADS * IDX_DH), ki, wi,
                         cache_kv, cache_idx, page_table)
        y_s5, new_a = s5_sample(u, state, *s5p)
    out = jnp.concatenate([y_s5, o_b], -1) @ w_out
    return out, new_a, kv, ki


def mixer_c(x, pos, w_in, cmp_pos, cmp_w1, cmp_w2, w_out,
            cache_cmp=None, cache_slc=None, win=None, page_table=None):
    bn, s, _ = x.shape
    q, kv_c, kv_s, kv_w, gates = split_cols(x @ w_in, C_SIZES)
    q = q.reshape(bn, s, C_HEADS, C_DH)
    q_rot = rope_partial(q, pos, C_ROT)
    kv_c = kv_c.reshape(bn, s, C_KV, 2, C_DH)
    kv_s = rope_kv(kv_s.reshape(bn, s, C_KV, 2, C_DH), pos, C_ROT)
    kv_w = rope_kv(kv_w.reshape(bn, s, C_KV, 2, C_DH), pos, C_ROT)
    qd = C_HEADS * C_DH
    q_raw2, q_rot2 = q.reshape(bn, s, qd), q_rot.reshape(bn, s, qd)
    if cache_cmp is None:
        n_pages = s // PAGE_SIZE
        ident = jnp.arange(bn * n_pages, dtype=jnp.int32).reshape(bn, n_pages)
        kck, kcv = nsa_compress(kv_c.reshape(bn * n_pages, PAGE_SIZE, C_KV, 2, C_DH), ident, cmp_pos, cmp_w1, cmp_w2)
        o = nsa_prompt(q_raw2, q_rot2, gates, kck, kcv, kv_s, kv_w)
        return o @ w_out, kv_c, kv_s, kv_w[:, -min(WINDOW, s):]
    kck, kcv = nsa_compress(cache_cmp, page_table, cmp_pos, cmp_w1, cmp_w2)
    o_cmp, o_win, sel_idx = nsa_sample_attend(q_raw2, q_rot2, kck, kcv, win, kv_w)
    o = nsa_sample_selected(sel_idx, q_rot2, gates, o_cmp, o_win, kv_s, cache_slc, page_table)
    new_win = jnp.concatenate([win, kv_w], axis=1)[:, -win.shape[1]:]
    return o @ w_out, kv_c, kv_s, new_win


def kernel(x_prompt, x_sample, state_a, cache_b_kv, cache_b_idx, cache_c_cmp, cache_c_slc, state_c_win, page_table,
           w_in_ab, s5_a_re, s5_a_im, s5_log_dt, s5_b_re, s5_b_im, s5_c_re, s5_c_im, s5_d, s5_w_glu, w_out_ab,
           w_in_c, cmp_pos, cmp_w1, cmp_w2, w_out_c,
           ln1_g, ln1_b, ln2_g, ln2_b, moe_w_group, moe_w_expert, moe_w_gate, moe_w_up, moe_w_down):
    past_len = page_table.shape[1] * PAGE_SIZE
    pos_p = jnp.arange(x_prompt.shape[1])
    pos_s = past_len + jnp.arange(x_sample.shape[1])
    xp, xs = x_prompt, x_sample
    pshape, sshape = xp.shape, xs.shape
    n_p = pshape[0] * pshape[1]
    outs = {k: [] for k in ('a_p', 'a_s', 'bkv_p', 'bkv_s', 'bidx_p', 'bidx_s',
                            'ccmp_p', 'ccmp_s', 'cslc_p', 'cslc_s', 'cwin_p', 'cwin_s')}
    for i in range(DEPTH):
        j = i // 2
        if i % 2 == 0:
            ab = (w_in_ab[j], s5_a_re[j], s5_a_im[j], s5_log_dt[j], s5_b_re[j], s5_b_im[j],
                  s5_c_re[j], s5_c_im[j], s5_d[j], s5_w_glu[j], w_out_ab[j])
            mp, st_p, kv_p, ki_p = mixer_ab(xp, pos_p, *ab)
            ms, st_s, kv_n, ki_n = mixer_ab(xs, pos_s, *ab, state=state_a[j], cache_kv=cache_b_kv[j],
                                            cache_idx=cache_b_idx[j], page_table=page_table)
            outs['a_p'].append(st_p); outs['a_s'].append(st_s)
            outs['bkv_p'].append(kv_p); outs['bkv_s'].append(kv_n)
            outs['bidx_p'].append(ki_p); outs['bidx_s'].append(ki_n)
        else:
            cp = (w_in_c[j], cmp_pos[j], cmp_w1[j], cmp_w2[j], w_out_c[j])
            mp, kc_p, ksl_p, win_p = mixer_c(xp, pos_p, *cp)
            ms, kc_n, ksl_n, win_n = mixer_c(xs, pos_s, *cp, cache_cmp=cache_c_cmp[j], cache_slc=cache_c_slc[j],
                                             win=state_c_win[j], page_table=page_table)
            outs['ccmp_p'].append(kc_p); outs['ccmp_s'].append(kc_n)
            outs['cslc_p'].append(ksl_p); outs['cslc_s'].append(ksl_n)
            outs['cwin_p'].append(win_p); outs['cwin_s'].append(win_n)
        xp = layer_norm(ALPHA * xp + mp, ln1_g[i], ln1_b[i])
        xs = layer_norm(ALPHA * xs + ms, ln1_g[i], ln1_b[i])
        moe = (moe_w_group[i], moe_w_expert[i], moe_w_gate[i], moe_w_up[i], moe_w_down[i])
        yp, ys = hier_moe([xp.reshape(n_p, D_MODEL), xs.reshape(-1, D_MODEL)], *moe)
        xp = layer_norm(ALPHA * xp + yp.reshape(pshape), ln2_g[i], ln2_b[i])
        xs = layer_norm(ALPHA * xs + ys.reshape(sshape), ln2_g[i], ln2_b[i])
    st = lambda k: jnp.stack(outs[k])
    return (xp, xs, st('a_p'), st('a_s'), st('bkv_p'), st('bkv_s'), st('bidx_p'), st('bidx_s'),
            st('ccmp_p'), st('ccmp_s'), st('cslc_p'), st('cslc_s'), st('cwin_p'), st('cwin_s'))
```

```python
import functools
import math

import jax
import jax.numpy as jnp
import numpy as np
from jax import lax
from jax.experimental import pallas as pl
from jax.experimental.pallas import tpu as pltpu

D_MODEL = 1024
DEPTH = 2
PAGE_SIZE = 128

S5_WIDTH = D_MODEL // 2
S5_GROUP = 16
S5_GROUPS = S5_WIDTH // S5_GROUP
S5_STATE = 64
B_HEADS = 8
B_DH = 64
B_KV_HEADS = 2
B_ROT = B_DH // 4
IDX_HEADS = 4
IDX_DH = 64
IDX_ROT = IDX_DH // 4
DSA_TOPK = 256
C_HEADS = 8
C_DH = 128
C_KV = 2
C_REP = C_HEADS // C_KV
C_ROT = C_DH // 4
CMP_BLOCK = 64
SEL_BLOCK = 64
N_SEL = 16
WINDOW = 512
N_EGROUPS = 4
E_PER_GROUP = 8
N_EXPERTS = N_EGROUPS * E_PER_GROUP
TOP_E = 2
D_FF_E = 512

ROPE_THETA = 500000.0
ALPHA = (2 * DEPTH) ** 0.25
Q_BLOCK = 128
LN_EPS = 1e-5
NEG = -1e30
FORCE = 1e4
LOG2E = math.log2(math.e)

AB_SIZES = (S5_WIDTH, B_HEADS * B_DH, B_KV_HEADS * 2 * B_DH, IDX_HEADS * IDX_DH, IDX_DH, IDX_HEADS)
C_SIZES = (C_HEADS * C_DH, C_KV * 2 * C_DH, C_KV * 2 * C_DH, C_KV * 2 * C_DH, C_HEADS * 3)

V7X_LANES = 128
V7X_SUBLANES = 8
V7X_VMEM_BYTES = 64 * 1024 * 1024
VMEM_LIMIT_BYTES = 48 * 1024 * 1024

MOE_ROUTER_TILE = 512
MOE_EXPERT_TILE = 256


def _moe_router_kernel(x_ref, w_ref, o_ref):
    logits = jnp.dot(x_ref[...], w_ref[...], precision=lax.Precision.HIGHEST,
                     preferred_element_type=jnp.float32)
    col = lax.broadcasted_iota(jnp.int32, logits.shape, 1)
    big = jnp.int32(1 << 20)
    ninf = jnp.float32(-jnp.inf)
    gl = jnp.where(col < N_EGROUPS, logits, ninf)
    gmax = jnp.max(gl, axis=-1, keepdims=True)
    g_top = jnp.min(jnp.where(gl == gmax, col, big), axis=-1, keepdims=True)
    g_w = 1.0 / jnp.sum(jnp.exp(gl - gmax), axis=-1, keepdims=True)
    lo = N_EGROUPS + E_PER_GROUP * g_top
    el = jnp.where((col >= lo) & (col < lo + E_PER_GROUP), logits, ninf)
    v1 = jnp.max(el, axis=-1, keepdims=True)
    i1 = jnp.min(jnp.where(el == v1, col, big), axis=-1, keepdims=True)
    el2 = jnp.where(col == i1, ninf, el)
    v2 = jnp.max(el2, axis=-1, keepdims=True)
    i2 = jnp.min(jnp.where(el2 == v2, col, big), axis=-1, keepdims=True)
    e2 = jnp.exp(v2 - v1)
    den = 1.0 + e2
    w1 = g_w / den
    w2 = g_w * e2 / den
    out = jnp.where(col == 0, (i1 - N_EGROUPS).astype(jnp.float32),
                    jnp.where(col == 1, (i2 - N_EGROUPS).astype(jnp.float32),
                              jnp.where(col == 2, w1, jnp.where(col == 3, w2, 0.0))))
    o_ref[...] = out


def _moe_route(x, w_router):
    t = x.shape[0]
    tm = min(MOE_ROUTER_TILE, t)
    assert t % tm == 0
    return pl.pallas_call(
        _moe_router_kernel,
        grid=(t // tm,),
        in_specs=[pl.BlockSpec((tm, D_MODEL), lambda i: (i, 0)),
                  pl.BlockSpec((D_MODEL, V7X_LANES), lambda i: (0, 0))],
        out_specs=pl.BlockSpec((tm, V7X_LANES), lambda i: (i, 0)),
        out_shape=jax.ShapeDtypeStruct((t, V7X_LANES), jnp.float32),
        name="moe_router",
    )(x, w_router)


def _moe_expert_kernel(tile_e_ref, nvalid_ref, xs_ref, wg_ref, wu_ref, wd_ref, sw_ref, o_ref,
                       wg_s, wu_s, wd_s):
    i = pl.program_id(0)
    valid = i < nvalid_ref[0]
    prev_e = tile_e_ref[jnp.maximum(i - 1, 0)]
    new_expert = jnp.logical_or(i == 0, tile_e_ref[i] != prev_e)

    @pl.when(jnp.logical_and(valid, new_expert))
    def _():
        wg_s[...] = wg_ref[0].astype(jnp.bfloat16)
        wu_s[...] = wu_ref[0].astype(jnp.bfloat16)
        wd_s[...] = wd_ref[0].astype(jnp.bfloat16)

    @pl.when(valid)
    def _():
        x = xs_ref[...]
        g = jnp.dot(x, wg_s[...], preferred_element_type=jnp.float32)
        u = jnp.dot(x, wu_s[...], preferred_element_type=jnp.float32)
        h = (g * jax.nn.sigmoid(g)) * u
        y = jnp.dot(h.astype(jnp.bfloat16), wd_s[...], preferred_element_type=jnp.float32)
        o_ref[...] = y * sw_ref[...]

    @pl.when(jnp.logical_not(valid))
    def _():
        o_ref[...] = jnp.zeros_like(o_ref)


def _moe_experts(tile_e, nvalid, xs, slot_w, w_gate, w_up, w_down):
    np_rows = xs.shape[0]
    tm = MOE_EXPERT_TILE
    n_tiles = np_rows // tm
    grid_spec = pltpu.PrefetchScalarGridSpec(
        num_scalar_prefetch=2,
        grid=(n_tiles,),
        in_specs=[
            pl.BlockSpec((tm, D_MODEL), lambda i, te, nv: (i, 0)),
            pl.BlockSpec((1, D_MODEL, D_FF_E), lambda i, te, nv: (te[i], 0, 0)),
            pl.BlockSpec((1, D_MODEL, D_FF_E), lambda i, te, nv: (te[i], 0, 0)),
            pl.BlockSpec((1, D_FF_E, D_MODEL), lambda i, te, nv: (te[i], 0, 0)),
            pl.BlockSpec((tm, 1), lambda i, te, nv: (i, 0)),
        ],
        out_specs=pl.BlockSpec((tm, D_MODEL), lambda i, te, nv: (i, 0)),
        scratch_shapes=[pltpu.VMEM((D_MODEL, D_FF_E), jnp.bfloat16),
                        pltpu.VMEM((D_MODEL, D_FF_E), jnp.bfloat16),
                        pltpu.VMEM((D_FF_E, D_MODEL), jnp.bfloat16)],
    )
    return pl.pallas_call(
        _moe_expert_kernel,
        grid_spec=grid_spec,
        out_shape=jax.ShapeDtypeStruct((np_rows, D_MODEL), jnp.float32),
        compiler_params=pltpu.CompilerParams(dimension_semantics=("arbitrary",),
                                             vmem_limit_bytes=VMEM_LIMIT_BYTES),
        name="moe_experts",
    )(tile_e, nvalid, xs, w_gate, w_up, w_down, slot_w)


def _moe_padded_rows(t):
    tm = MOE_EXPERT_TILE
    slots = t * TOP_E
    return ((slots + N_EXPERTS * (tm - 1)) // tm + 1) * tm


def hier_moe(xs_list, xs_bf16_list, w_group, w_expert, w_gate, w_up, w_down):
    tm = MOE_EXPERT_TILE
    w_router = jnp.zeros((D_MODEL, V7X_LANES), jnp.float32)
    w_router = w_router.at[:, :N_EGROUPS].set(w_group).at[:, N_EGROUPS:N_EGROUPS + N_EXPERTS].set(w_expert)
    routed = jnp.concatenate([_moe_route(x, w_router) for x in xs_list], axis=0)
    t = routed.shape[0]
    eid = routed[:, 0:2].astype(jnp.int32).reshape(-1)
    ew = routed[:, 2:4].reshape(-1)
    n_slots = t * TOP_E
    np_rows = _moe_padded_rows(t)

    experts = jnp.arange(N_EXPERTS, dtype=jnp.int32)
    counts = jnp.sum((eid[:, None] == experts[None, :]).astype(jnp.int32), axis=0)
    order = jnp.argsort(eid, stable=True).astype(jnp.int32)
    rank = jnp.argsort(order).astype(jnp.int32)
    padded = ((counts + tm - 1) // tm) * tm
    pad_end = jnp.cumsum(padded)
    pad_off = pad_end - padded
    off = jnp.cumsum(counts) - counts
    pos = pad_off[eid] + rank - off[eid]
    tile_start = jnp.arange(np_rows // tm, dtype=jnp.int32) * tm
    tile_e = jnp.minimum(jnp.sum((pad_end[None, :] <= tile_start[:, None]).astype(jnp.int32), axis=1), N_EXPERTS - 1)
    nvalid = (pad_end[-1] // tm).astype(jnp.int32).reshape(1)
    row = jnp.arange(np_rows, dtype=jnp.int32)
    row_e = tile_e[row // tm]
    local = row - pad_off[row_e]
    row_ok = local < counts[row_e]
    slot = order[jnp.clip(off[row_e] + local, 0, n_slots - 1)]
    src_tok = jnp.where(row_ok, slot // TOP_E, 0)
    slot_w = jnp.where(row_ok, ew[slot], 0.0)

    x_all = jnp.concatenate(xs_bf16_list, axis=0)
    ys = _moe_experts(tile_e, nvalid, x_all[src_tok], slot_w[:, None], w_gate, w_up, w_down)
    pos2 = pos.reshape(t, TOP_E)
    out, start = [], 0
    for x in xs_list:
        sl = pos2[start:start + x.shape[0]]
        out.append((ys[sl[:, 0]], ys[sl[:, 1]]))
        start += x.shape[0]
    return out


INT_MIN = -(2 ** 31)
INT_MAX = 2 ** 31 - 1


def _f32_order_key(x):
    b = lax.bitcast_convert_type(x + 0.0, jnp.int32)
    return jnp.where(b >= 0, b, b ^ jnp.int32(INT_MAX))


def _lanes(x, w):
    return x if w == V7X_LANES else jnp.concatenate([x] * (w // V7X_LANES), axis=-1)


def _count_rows(key_ref, n_chunks, pred_fns):
    n_max, rows, w = key_ref.shape
    slabs = w // V7X_LANES
    assert slabs * n_max <= 256
    lane = lax.broadcasted_iota(jnp.int32, (rows, V7X_LANES), 1)

    def body(c, accs):
        out = list(accs)
        for s in range(slabs):
            blk = key_ref[c, :, s * V7X_LANES:(s + 1) * V7X_LANES]
            idx = c * w + s * V7X_LANES + lane
            for t, fn in enumerate(pred_fns):
                out[t] = out[t] + jnp.where(fn(blk, idx), 1, 0)
        return tuple(out)

    init = tuple(jnp.zeros((rows, V7X_LANES), jnp.int32) for _ in pred_fns)
    accs = lax.fori_loop(0, n_chunks, body, init)
    ones = jnp.ones((V7X_LANES, V7X_LANES), jnp.bfloat16)
    return [jnp.dot(a.astype(jnp.float32).astype(jnp.bfloat16), ones,
                    preferred_element_type=jnp.float32).astype(jnp.int32) for a in accs]


def _any_row(mask):
    return jnp.max(jnp.where(mask, 1.0, 0.0)) > 0.5


def _topk_threshold(key_ref, n_chunks, k, index_bits):
    _, rows, w = key_ref.shape
    col = lambda v: jnp.full((rows, V7X_LANES), v, jnp.int32)

    def v_step(st):
        lo, hi, exact = st
        mid = (lo & hi) + ((lo ^ hi) >> 1)
        cnt, = _count_rows(key_ref, n_chunks, [lambda blk, idx: blk > mid])
        active = lo < hi
        hit = active & (cnt == k)
        less = active & (cnt < k)
        more = active & (cnt > k)
        hi = jnp.where(hit | less, mid, hi)
        lo = jnp.where(hit, mid, jnp.where(more, mid + 1, lo))
        exact = jnp.where(hit, 1, exact)
        return lo, hi, exact

    def v_body(st):
        lo, hi, exact = v_step(v_step(st[:3]))
        return lo, hi, exact, _any_row(lo < hi)

    thr, _, exact, _ = lax.while_loop(lambda st: st[3], v_body,
                                      (col(INT_MIN), col(INT_MAX), col(0), jnp.bool_(True)))
    c_gt, c_eq = _count_rows(key_ref, n_chunks, [lambda blk, idx: blk > thr, lambda blk, idx: blk == thr])
    r = k - c_gt
    take_all_ties = (exact == 0) & (thr != INT_MIN)
    need = take_all_ties & (c_eq > r)
    cut_default = jnp.where(take_all_ties, INT_MAX, -1)

    def tie_phase():
        def body(_, st):
            lo_i, hi_i = st
            mid = (lo_i + hi_i) >> 1
            g, = _count_rows(key_ref, n_chunks, [lambda blk, idx: (blk == thr) & (idx <= mid)])
            ok = g >= r
            return jnp.where(ok, lo_i, mid + 1), jnp.where(ok, mid, hi_i)
        lo_i, _ = lax.fori_loop(0, index_bits, body, (col(0), col(0) + (n_chunks * w - 1)))
        return jnp.where(need, lo_i, cut_default)

    cut = lax.cond(_any_row(need), tie_phase, lambda: cut_default)
    return thr, cut


def _selected(key, idx, thr, cut):
    w = key.shape[-1]
    thr, cut = _lanes(thr, w), _lanes(cut, w)
    return (key > thr) | ((key == thr) & (idx <= cut))


def _count_cols(key_ref, n_chunks, pred_fns):
    _, w, cols = key_ref.shape
    acc_rows = min(w, COUNT_ACC_ROWS)

    def body(c, accs):
        blk = key_ref[c]
        idx = c * w + lax.broadcasted_iota(jnp.int32, (w, cols), 0)
        return tuple(acc + jnp.sum(jnp.where(fn(blk, idx), 1, 0).reshape(w // acc_rows, acc_rows, cols), axis=0)
                     for fn, acc in zip(pred_fns, accs))

    init = tuple(jnp.zeros((acc_rows, cols), jnp.int32) for _ in pred_fns)
    accs = lax.fori_loop(0, n_chunks, body, init)
    return [jnp.sum(a, axis=0, keepdims=True) for a in accs]


TOPK_PASSES_PER_TRIP = 4
COUNT_ACC_ROWS = 64


def _topk_threshold_cols(key_ref, n_chunks, k, index_bits):
    _, w, cols = key_ref.shape
    row = lambda v: jnp.full((1, cols), v, jnp.int32)

    def v_step(st):
        lo, hi, exact = st
        mid = (lo & hi) + ((lo ^ hi) >> 1)
        cnt, = _count_cols(key_ref, n_chunks, [lambda blk, idx: blk > mid])
        active = lo < hi
        hit = active & (cnt == k)
        less = active & (cnt < k)
        more = active & (cnt > k)
        hi = jnp.where(hit | less, mid, hi)
        lo = jnp.where(hit, mid, jnp.where(more, mid + 1, lo))
        exact = jnp.where(hit, 1, exact)
        return lo, hi, exact

    def v_body(st):
        inner = st[:3]
        for _ in range(TOPK_PASSES_PER_TRIP):
            inner = v_step(inner)
        return inner + (_any_row(inner[0] < inner[1]),)

    thr, _, exact, _ = lax.while_loop(lambda st: st[3], v_body,
                                      (row(INT_MIN), row(INT_MAX), row(0), jnp.bool_(True)))
    c_gt, c_eq = _count_cols(key_ref, n_chunks, [lambda blk, idx: blk > thr, lambda blk, idx: blk == thr])
    r = k - c_gt
    take_all_ties = (exact == 0) & (thr != INT_MIN)
    need = take_all_ties & (c_eq > r)
    cut_default = jnp.where(take_all_ties, INT_MAX, -1)

    def tie_phase():
        def body(_, st):
            lo_i, hi_i = st
            mid = (lo_i + hi_i) >> 1
            g, = _count_cols(key_ref, n_chunks, [lambda blk, idx: (blk == thr) & (idx <= mid)])
            ok = g >= r
            return jnp.where(ok, lo_i, mid + 1), jnp.where(ok, mid, hi_i)
        lo_i, _ = lax.fori_loop(0, index_bits, body, (row(0), row(0) + (n_chunks * w - 1)))
        return jnp.where(need, lo_i, cut_default)

    cut = lax.cond(_any_row(need), tie_phase, lambda: cut_default)
    return thr, cut


def _selected_cols(key, idx, thr, cut):
    return (key > thr) | ((key == thr) & (idx <= cut))


DSA_CHUNK = 1024


def _split3_lhs(x):
    hi = x.astype(jnp.bfloat16)
    lo = (x - hi.astype(jnp.float32)).astype(jnp.bfloat16)
    return jnp.concatenate([hi, hi, lo], axis=-1)


def _split3_rhs(x):
    hi = x.astype(jnp.bfloat16)
    lo = (x - hi.astype(jnp.float32)).astype(jnp.bfloat16)
    return jnp.concatenate([hi, lo, hi], axis=-1)


_NT = (((1,), (1,)), ((), ()))


def _with_ones_column(v):
    dh = v.shape[-1]
    width = (dh // V7X_LANES + 1) * V7X_LANES
    ones = jnp.ones(v.shape[:-1] + (1,), v.dtype)
    return jnp.concatenate([v, ones, jnp.zeros(v.shape[:-1] + (width - dh - 1,), v.dtype)], axis=-1)


def _dsa_prompt_kernel(qi_ref, wi_ref, kcat_ref, q_ref, k_ref, v_ref, o_ref, key_s, *, topk):
    cw = DSA_CHUNK
    rep = B_HEADS // B_KV_HEADS
    j = pl.program_id(1)
    t0 = j * Q_BLOCK
    n_chunks = (t0 + Q_BLOCK + cw - 1) // cw
    qpos = t0 + lax.broadcasted_iota(jnp.int32, (Q_BLOCK, 1), 0)

    qi = qi_ref[0]
    wi_t = wi_ref[0] * (IDX_DH ** -0.5)
    qcat = jnp.concatenate([_split3_lhs(qi[:, h * IDX_DH:(h + 1) * IDX_DH]) for h in range(IDX_HEADS)], axis=0)
    qpos_t = t0 + lax.broadcasted_iota(jnp.int32, (1, Q_BLOCK), 1)

    def score_chunk(c, carry):
        start = pl.multiple_of(c * cw, cw)
        kc = kcat_ref[0, pl.ds(start, cw), :]
        s = lax.dot_general(kc, qcat, _NT, preferred_element_type=jnp.float32)
        acc = None
        for h in range(IDX_HEADS):
            term = wi_t[h:h + 1, :] * jnp.maximum(s[:, h * Q_BLOCK:(h + 1) * Q_BLOCK], 0.0)
            acc = term if acc is None else acc + term
        kpos = start + lax.broadcasted_iota(jnp.int32, (cw, Q_BLOCK), 0)
        key_s[c] = jnp.where(kpos <= qpos_t, _f32_order_key(acc), INT_MIN)
        return carry

    lax.fori_loop(0, n_chunks, score_chunk, 0)
    thr, cut = _topk_threshold_cols(key_s, n_chunks, topk, index_bits=int(math.log2(key_s.shape[0] * cw)))

    q = q_ref[0]
    qg = [jnp.concatenate([q[:, (g * rep + r) * B_DH:(g * rep + r + 1) * B_DH] for r in range(rep)],
                          axis=0).astype(jnp.bfloat16) for g in range(B_KV_HEADS)]

    def att_chunk(c, carry):
        start = pl.multiple_of(c * cw, cw)
        kidx = start + lax.broadcasted_iota(jnp.int32, (cw, Q_BLOCK), 0)
        bias = jnp.transpose(jnp.where(_selected_cols(key_s[c], kidx, thr, cut), 0.0, NEG))[None]
        out = []
        for g in range(B_KV_HEADS):
            m, acc = carry[g]
            kg = k_ref[0, g, pl.ds(start, cw), :]
            vg = v_ref[0, g, pl.ds(start, cw), :]
            s = lax.dot_general(qg[g], kg, _NT, preferred_element_type=jnp.float32) * (B_DH ** -0.5 * LOG2E)
            s = s.reshape(rep, Q_BLOCK, cw) + bias
            m_new = jnp.maximum(m, jnp.max(s, axis=-1, keepdims=True))
            p = jnp.exp2((s - m_new).astype(jnp.bfloat16))
            pv = jnp.dot(p.reshape(rep * Q_BLOCK, cw), vg, preferred_element_type=jnp.float32)
            acc = jnp.exp2(m - m_new) * acc + pv.reshape(rep, Q_BLOCK, vg.shape[-1])
            out.append((m_new, acc))
        return tuple(out)

    vw = v_ref.shape[-1]
    init = tuple((jnp.full((rep, Q_BLOCK, 1), NEG, jnp.float32),
                  jnp.zeros((rep, Q_BLOCK, vw), jnp.float32)) for _ in range(B_KV_HEADS))
    res = lax.fori_loop(0, n_chunks, att_chunk, init)
    heads = []
    for g in range(B_KV_HEADS):
        acc = res[g][1]
        l = acc[:, :, B_DH:B_DH + 1]
        o = jnp.where(l > 0.0, acc[:, :, :B_DH] / jnp.where(l > 0.0, l, 1.0), 0.0)
        heads += [o[r] for r in range(rep)]
    o_ref[0] = jnp.concatenate(heads, axis=-1)


def dsa_prompt(q, kv, qi, ki, wi):
    bn, s, _ = q.shape
    topk = min(DSA_TOPK, s // 4)
    cw = DSA_CHUNK
    assert s % cw == 0 and s % Q_BLOCK == 0
    kcat = _split3_rhs(ki)
    k = jnp.transpose(kv[:, :, :, 0, :], (0, 2, 1, 3)).astype(jnp.bfloat16)
    v = _with_ones_column(jnp.transpose(kv[:, :, :, 1, :], (0, 2, 1, 3)).astype(jnp.bfloat16))
    qw = B_HEADS * B_DH
    return pl.pallas_call(
        functools.partial(_dsa_prompt_kernel, topk=topk),
        grid=(bn, s // Q_BLOCK),
        in_specs=[
            pl.BlockSpec((1, Q_BLOCK, IDX_HEADS * IDX_DH), lambda b, j: (b, j, 0)),
            pl.BlockSpec((1, IDX_HEADS, Q_BLOCK), lambda b, j: (b, 0, j)),
            pl.BlockSpec((1, s, 3 * IDX_DH), lambda b, j: (b, 0, 0)),
            pl.BlockSpec((1, Q_BLOCK, qw), lambda b, j: (b, j, 0)),
            pl.BlockSpec((1, B_KV_HEADS, s, B_DH), lambda b, j: (b, 0, 0, 0)),
            pl.BlockSpec((1, B_KV_HEADS, s, v.shape[-1]), lambda b, j: (b, 0, 0, 0)),
        ],
        out_specs=pl.BlockSpec((1, Q_BLOCK, qw), lambda b, j: (b, j, 0)),
        out_shape=jax.ShapeDtypeStruct((bn, s, qw), jnp.float32),
        scratch_shapes=[pltpu.VMEM((s // cw, cw, Q_BLOCK), jnp.int32)],
        compiler_params=pltpu.CompilerParams(dimension_semantics=("arbitrary", "arbitrary"),
                                             vmem_limit_bytes=VMEM_LIMIT_BYTES),
        name="dsa_prompt",
    )(qi, jnp.swapaxes(wi, 1, 2), kcat, q, k, v)


NSA_CHUNK = 1024
NSA_WIN_BLOCKS = WINDOW // Q_BLOCK + 1


def _nsa_prompt_kernel(qraw_ref, qrot_ref, gates_ref, kck_ref, kcv_ref, e_ref, ks_ref, vs_ref, *rest, n_sel):
    kw_refs = rest[0:NSA_WIN_BLOCKS]
    vw_refs = rest[NSA_WIN_BLOCKS:2 * NSA_WIN_BLOCKS]
    o_ref, key_s = rest[2 * NSA_WIN_BLOCKS:]
    cw = NSA_CHUNK
    rep = C_REP
    scale = C_DH ** -0.5
    j = pl.program_id(1)
    t0 = j * Q_BLOCK
    n_chunks = (t0 + Q_BLOCK + cw - 1) // cw
    qpos = t0 + lax.broadcasted_iota(jnp.int32, (Q_BLOCK, 1), 0)
    nbp = kck_ref.shape[2]
    blk = lax.broadcasted_iota(jnp.int32, (Q_BLOCK, nbp), 1)
    cmask = ((blk + 1) * CMP_BLOCK - 1) <= qpos
    cur = qpos // SEL_BLOCK
    forced = (blk == 0) | (blk == cur) | (blk == cur - 1)
    valid = blk * SEL_BLOCK <= qpos
    wlen = NSA_WIN_BLOCKS * Q_BLOCK
    wpos = t0 - WINDOW + lax.broadcasted_iota(jnp.int32, (Q_BLOCK, wlen), 1)
    wmask = ((wpos >= 0) & (wpos <= qpos) & (wpos > qpos - WINDOW))[None]

    qraw = qraw_ref[0]
    qrot = qrot_ref[0]
    sig = jax.nn.sigmoid(gates_ref[0])

    def stack(x, g):
        return jnp.concatenate([x[:, (g * rep + r) * C_DH:(g * rep + r + 1) * C_DH] for r in range(rep)], axis=0)

    o_cmps = []
    for g in range(C_KV):
        s = lax.dot_general(stack(qraw, g), kck_ref[0, g], _NT, precision=lax.Precision.HIGHEST,
                            preferred_element_type=jnp.float32) * scale
        s = jnp.where(cmask[None], s.reshape(rep, Q_BLOCK, nbp), NEG)
        e = jnp.exp(s - jnp.max(s, axis=-1, keepdims=True))
        p = e / jnp.sum(e, axis=-1, keepdims=True)
        p = p * jnp.where(jnp.max(jnp.where(cmask, 1.0, 0.0), axis=-1, keepdims=True) > 0.5, 1.0, 0.0)[None]
        o_cmps.append(jnp.dot(p.reshape(rep * Q_BLOCK, nbp), kcv_ref[0, g], precision=lax.Precision.HIGHEST,
                              preferred_element_type=jnp.float32).reshape(rep, Q_BLOCK, C_DH))
        imp = p[0]
        for r in range(1, rep):
            imp = imp + p[r]
        score = jnp.where(forced, FORCE, imp)
        key_s[0, :, g * Q_BLOCK:(g + 1) * Q_BLOCK] = jnp.transpose(
            jnp.where(valid, _f32_order_key(score), INT_MIN))

    thr, cut = _topk_threshold_cols(key_s, 1, n_sel, index_bits=int(math.log2(nbp)))
    blk_t = lax.broadcasted_iota(jnp.int32, (nbp, C_KV * Q_BLOCK), 0)
    sel_t = jnp.where(_selected_cols(key_s[0], blk_t, thr, cut), 1.0, 0.0)

    for g in range(C_KV):
        o_cmp = o_cmps[g]
        selblk = jnp.transpose(sel_t[:, g * Q_BLOCK:(g + 1) * Q_BLOCK]).astype(jnp.bfloat16)

        qg = stack(qrot, g).astype(jnp.bfloat16)

        def att_chunk(c, carry):
            m, acc = carry
            start = pl.multiple_of(c * cw, cw)
            kpos = start + lax.broadcasted_iota(jnp.int32, (Q_BLOCK, cw), 1)
            hit = jnp.dot(selblk, e_ref[c], preferred_element_type=jnp.float32)
            bias = jnp.where((hit > 0.5) & (kpos <= qpos), 0.0, NEG)[None]
            kg = ks_ref[0, g, pl.ds(start, cw), :]
            vg = vs_ref[0, g, pl.ds(start, cw), :]
            sc = lax.dot_general(qg, kg, _NT, preferred_element_type=jnp.float32) * (scale * LOG2E)
            sc = sc.reshape(rep, Q_BLOCK, cw) + bias
            m_new = jnp.maximum(m, jnp.max(sc, axis=-1, keepdims=True))
            pp = jnp.exp2((sc - m_new).astype(jnp.bfloat16))
            pv = jnp.dot(pp.reshape(rep * Q_BLOCK, cw), vg, preferred_element_type=jnp.float32)
            return m_new, jnp.exp2(m - m_new) * acc + pv.reshape(rep, Q_BLOCK, vg.shape[-1])

        init = (jnp.full((rep, Q_BLOCK, 1), NEG, jnp.float32),
                jnp.zeros((rep, Q_BLOCK, vs_ref.shape[-1]), jnp.float32))
        _, acc = lax.fori_loop(0, n_chunks, att_chunk, init)
        l = acc[:, :, C_DH:C_DH + 1]
        o_slc = jnp.where(l > 0.0, acc[:, :, :C_DH] / jnp.where(l > 0.0, l, 1.0), 0.0)

        kwin = jnp.concatenate([r_[0, g] for r_ in kw_refs], axis=0)
        vwin = jnp.concatenate([r_[0, g] for r_ in vw_refs], axis=0)
        sw = lax.dot_general(qg, kwin, _NT, preferred_element_type=jnp.float32) * scale
        sw = jnp.where(wmask, sw.reshape(rep, Q_BLOCK, wlen), NEG)
        ew = jnp.where(wmask, jnp.exp(sw - jnp.max(sw, axis=-1, keepdims=True)), 0.0)
        lw = jnp.sum(ew, axis=-1, keepdims=True)
        pw = ew / jnp.where(lw > 0.0, lw, 1.0)
        o_win = jnp.dot(pw.reshape(rep * Q_BLOCK, wlen).astype(jnp.bfloat16), vwin,
                        preferred_element_type=jnp.float32).reshape(rep, Q_BLOCK, C_DH)

        for r in range(rep):
            hh = g * rep + r
            o = (sig[:, 3 * hh:3 * hh + 1] * o_cmp[r] + sig[:, 3 * hh + 1:3 * hh + 2] * o_slc[r]
                 + sig[:, 3 * hh + 2:3 * hh + 3] * o_win[r])
            o_ref[0, :, hh * C_DH:(hh + 1) * C_DH] = o


def nsa_prompt(q_raw, q_rot, gates, kck, kcv, kv_s, kv_w):
    bn, s, _ = q_raw.shape
    cw = NSA_CHUNK
    assert s % cw == 0
    nb = kck.shape[2]
    nbs = -(-s // SEL_BLOCK)
    assert nb == nbs
    nbp = -(-nb // V7X_LANES) * V7X_LANES
    n_sel = min(N_SEL, nbs)
    kck = jnp.pad(kck, ((0, 0), (0, 0), (0, nbp - nb), (0, 0)))
    kcv = jnp.pad(kcv, ((0, 0), (0, 0), (0, nbp - nb), (0, 0)))
    split = lambda kv, c: jnp.transpose(kv[:, :, :, c, :], (0, 2, 1, 3)).astype(jnp.bfloat16)
    ks, vs, kw, vw = split(kv_s, 0), _with_ones_column(split(kv_s, 1)), split(kv_w, 0), split(kv_w, 1)
    expand = (jnp.arange(s, dtype=jnp.int32)[None, :] // SEL_BLOCK == jnp.arange(nbp, dtype=jnp.int32)[:, None])
    expand = jnp.transpose(expand.astype(jnp.bfloat16).reshape(nbp, s // cw, cw), (1, 0, 2))
    qd = C_HEADS * C_DH
    nq = s // Q_BLOCK
    wb = NSA_WIN_BLOCKS

    def win_spec(slot):
        return pl.BlockSpec((1, C_KV, Q_BLOCK, C_DH),
                            lambda b, j: (b, 0, jnp.maximum(j - (wb - 1) + slot, 0), 0))

    return pl.pallas_call(
        functools.partial(_nsa_prompt_kernel, n_sel=n_sel),
        grid=(bn, nq),
        in_specs=[
            pl.BlockSpec((1, Q_BLOCK, qd), lambda b, j: (b, j, 0)),
            pl.BlockSpec((1, Q_BLOCK, qd), lambda b, j: (b, j, 0)),
            pl.BlockSpec((1, Q_BLOCK, C_HEADS * 3), lambda b, j: (b, j, 0)),
            pl.BlockSpec((1, C_KV, nbp, C_DH), lambda b, j: (b, 0, 0, 0)),
            pl.BlockSpec((1, C_KV, nbp, C_DH), lambda b, j: (b, 0, 0, 0)),
            pl.BlockSpec((s // cw, nbp, cw), lambda b, j: (0, 0, 0)),
            pl.BlockSpec((1, C_KV, s, C_DH), lambda b, j: (b, 0, 0, 0)),
            pl.BlockSpec((1, C_KV, s, vs.shape[-1]), lambda b, j: (b, 0, 0, 0)),
        ] + [win_spec(i) for i in range(wb)] + [win_spec(i) for i in range(wb)],
        out_specs=pl.BlockSpec((1, Q_BLOCK, qd), lambda b, j: (b, j, 0)),
        out_shape=jax.ShapeDtypeStruct((bn, s, qd), jnp.float32),
        scratch_shapes=[pltpu.VMEM((1, nbp, C_KV * Q_BLOCK), jnp.int32)],
        compiler_params=pltpu.CompilerParams(dimension_semantics=("arbitrary", "arbitrary"),
                                             vmem_limit_bytes=VMEM_LIMIT_BYTES),
        name="nsa_prompt",
    )(q_raw, q_rot, gates, kck, kcv, expand, ks, vs, *([kw] * wb), *([vw] * wb))


S5_N = S5_GROUPS * S5_STATE
S5_TILES = 4
S5_TILE_IN = S5_WIDTH // S5_TILES
S5_TILE_N = S5_N // S5_TILES
S5_CHUNK = 256
S5_SUB = S5_CHUNK // V7X_SUBLANES


def _s5_input_map(u, bcat_ref, store):
    for i in range(S5_TILES):
        ucat = _split3_lhs(u[:, i * S5_TILE_IN:(i + 1) * S5_TILE_IN])
        r = jnp.dot(ucat, bcat_ref[i], preferred_element_type=jnp.float32)
        store(i, r[:, :S5_TILE_N], r[:, S5_TILE_N:])


def _s5_output_map(h_tile, u, ccat_ref, d_ref, wglu_ref):
    ys = []
    for i in range(S5_TILES):
        re, im = h_tile(i)
        hcat = jnp.concatenate([re, im], axis=-1).astype(jnp.bfloat16)
        ys.append(jnp.dot(hcat, ccat_ref[i], preferred_element_type=jnp.float32))
    y = jax.nn.gelu(jnp.concatenate(ys, axis=-1) + d_ref[...] * u)
    z = jnp.dot(y.astype(jnp.bfloat16), wglu_ref[...], preferred_element_type=jnp.float32)
    return y * jax.nn.sigmoid(z)


def _cmul_add(a_re, a_im, x_re, x_im, b_re, b_im):
    return a_re * x_re - a_im * x_im + b_re, a_re * x_im + a_im * x_re + b_im


def _s5_prompt_kernel(u_ref, h0_ref, bcat_ref, lam8_ref, lamm_ref, ccat_ref, d_ref, wglu_ref,
                      y_ref, hlast_ref, hs, ein_s, carry_s):
    n = S5_N
    m = S5_SUB
    c = pl.program_id(1)

    @pl.when(c == 0)
    def _():
        carry_s[...] = h0_ref[0]

    u = u_ref[0]

    def store_bu(i, re, im):
        hs[:, i * S5_TILE_N:(i + 1) * S5_TILE_N] = re
        hs[:, n + i * S5_TILE_N:n + (i + 1) * S5_TILE_N] = im

    _s5_input_map(u, bcat_ref, store_bu)

    def scan_body(k, h):
        row = pl.multiple_of(k * V7X_SUBLANES, V7X_SUBLANES)
        rows = pl.ds(row, V7X_SUBLANES)
        n_re, n_im = _cmul_add(lam8_ref[:, :n], lam8_ref[:, n:], h[0], h[1], hs[rows, :n], hs[rows, n:])
        hs[rows, :n] = n_re
        hs[rows, n:] = n_im
        return n_re, n_im

    zero = jnp.zeros((V7X_SUBLANES, n), jnp.float32)
    ends = lax.fori_loop(0, m, scan_body, (zero, zero))

    lm_re, lm_im = lamm_ref[:, :n], lamm_ref[:, n:]
    e_re, e_im = carry_s[:, :n], carry_s[:, n:]
    for s in range(V7X_SUBLANES):
        ein_s[s:s + 1, :n] = e_re
        ein_s[s:s + 1, n:] = e_im
        e_re, e_im = _cmul_add(lm_re, lm_im, e_re, e_im, ends[0][s:s + 1], ends[1][s:s + 1])
    carry_s[:, :n] = e_re
    carry_s[:, n:] = e_im
    hlast_ref[0] = carry_s[...]

    def fix_body(k, corr):
        row = pl.multiple_of(k * V7X_SUBLANES, V7X_SUBLANES)
        rows = pl.ds(row, V7X_SUBLANES)
        c_re, c_im = _cmul_add(lam8_ref[:, :n], lam8_ref[:, n:], corr[0], corr[1], 0.0, 0.0)
        hs[rows, :n] = hs[rows, :n] + c_re
        hs[rows, n:] = hs[rows, n:] + c_im
        return c_re, c_im

    lax.fori_loop(0, m, fix_body, (ein_s[:, :n], ein_s[:, n:]))

    def h_tile(i):
        return hs[:, i * S5_TILE_N:(i + 1) * S5_TILE_N], hs[:, n + i * S5_TILE_N:n + (i + 1) * S5_TILE_N]

    y_ref[0] = _s5_output_map(h_tile, u, ccat_ref, d_ref, wglu_ref)


def _s5_sample_kernel(u_ref, h0_ref, bcat_ref, lam_ref, ccat_ref, d_ref, wglu_ref, y_ref, hlast_ref, hs):
    n = S5_N
    steps, bd, _ = u_ref.shape
    h_re, h_im = h0_ref[:, :n], h0_ref[:, n:]
    lam_re, lam_im = lam_ref[:, :n], lam_ref[:, n:]
    for t in range(steps):
        u = u_ref[t]

        def store_bu(i, re, im):
            hs[:, i * S5_TILE_N:(i + 1) * S5_TILE_N] = re
            hs[:, n + i * S5_TILE_N:n + (i + 1) * S5_TILE_N] = im

        _s5_input_map(u, bcat_ref, store_bu)
        h_re, h_im = _cmul_add(lam_re, lam_im, h_re, h_im, hs[:, :n], hs[:, n:])
        hs[:, :n] = h_re
        hs[:, n:] = h_im

        def h_tile(i):
            return hs[:, i * S5_TILE_N:(i + 1) * S5_TILE_N], hs[:, n + i * S5_TILE_N:n + (i + 1) * S5_TILE_N]

        y_ref[t] = _s5_output_map(h_tile, u, ccat_ref, d_ref, wglu_ref)
    hlast_ref[:, :n] = h_re
    hlast_ref[:, n:] = h_im


def _s5_params(a_re, a_im, log_dt, b_re, b_im, c_re, c_im):
    lam = lax.complex(a_re, a_im)
    dt = jnp.exp(log_dt)[:, None]
    lam_bar = jnp.exp(lam * dt)
    b_bar = ((lam_bar - 1.0) / lam)[:, :, None] * lax.complex(b_re, b_im)
    gpt = S5_GROUPS // S5_TILES
    eye = jnp.eye(gpt, dtype=jnp.float32)

    def in_blockdiag(w):
        w = w.reshape(S5_TILES, gpt, S5_STATE, S5_GROUP)
        return jnp.einsum('tgpc,gh->tgchp', w, eye).reshape(S5_TILES, S5_TILE_IN, S5_TILE_N)

    def out_blockdiag(w):
        w = w.reshape(S5_TILES, gpt, S5_GROUP, S5_STATE)
        return jnp.einsum('tgcp,gh->tgphc', w, eye).reshape(S5_TILES, S5_TILE_N, S5_TILE_IN)

    b_full = jnp.concatenate([in_blockdiag(b_bar.real), in_blockdiag(b_bar.imag)], axis=-1)
    b_hi = b_full.astype(jnp.bfloat16)
    b_lo = (b_full - b_hi.astype(jnp.float32)).astype(jnp.bfloat16)
    bcat = jnp.concatenate([b_hi, b_lo, b_hi], axis=1)
    ccat = jnp.concatenate([out_blockdiag(c_re), out_blockdiag(-c_im)], axis=1).astype(jnp.bfloat16)
    flat = lambda z: jnp.concatenate([z.real.reshape(1, S5_N), z.imag.reshape(1, S5_N)], axis=-1)
    lam_row = flat(lam_bar)
    lamm_row = flat(jnp.exp(lam * dt * S5_SUB))
    return bcat, ccat, lam_row, lamm_row


def s5_prompt(u, a_re, a_im, log_dt, b_re, b_im, c_re, c_im, d, w_glu):
    bn, s, _ = u.shape
    tc, m = S5_CHUNK, S5_SUB
    assert s % tc == 0
    nc = s // tc
    bcat, ccat, lam_row, lamm_row = _s5_params(a_re, a_im, log_dt, b_re, b_im, c_re, c_im)
    lam8 = jnp.broadcast_to(lam_row, (V7X_SUBLANES, 2 * S5_N))
    to_ks = lambda a: a.reshape(bn, nc, V7X_SUBLANES, m, S5_WIDTH).swapaxes(2, 3).reshape(bn, s, S5_WIDTH)
    from_ks = lambda a: a.reshape(bn, nc, m, V7X_SUBLANES, S5_WIDTH).swapaxes(2, 3).reshape(bn, s, S5_WIDTH)
    h0 = jnp.zeros((bn, 1, 2 * S5_N), jnp.float32)
    const = lambda shape: pl.BlockSpec(shape, lambda b, c: (0,) * len(shape))
    y, hlast = pl.pallas_call(
        _s5_prompt_kernel,
        grid=(bn, nc),
        in_specs=[
            pl.BlockSpec((1, tc, S5_WIDTH), lambda b, c: (b, c, 0)),
            pl.BlockSpec((1, 1, 2 * S5_N), lambda b, c: (b, 0, 0)),
            const(bcat.shape), const(lam8.shape), const(lamm_row.shape), const(ccat.shape),
            const((1, S5_WIDTH)), const((S5_WIDTH, S5_WIDTH)),
        ],
        out_specs=[pl.BlockSpec((1, tc, S5_WIDTH), lambda b, c: (b, c, 0)),
                   pl.BlockSpec((1, 1, 2 * S5_N), lambda b, c: (b, 0, 0))],
        out_shape=[jax.ShapeDtypeStruct((bn, s, S5_WIDTH), jnp.float32),
                   jax.ShapeDtypeStruct((bn, 1, 2 * S5_N), jnp.float32)],
        scratch_shapes=[pltpu.VMEM((tc, 2 * S5_N), jnp.float32),
                        pltpu.VMEM((V7X_SUBLANES, 2 * S5_N), jnp.float32),
                        pltpu.VMEM((1, 2 * S5_N), jnp.float32)],
        compiler_params=pltpu.CompilerParams(dimension_semantics=("arbitrary", "arbitrary"),
                                             vmem_limit_bytes=VMEM_LIMIT_BYTES),
        name="s5_prompt",
    )(to_ks(u), h0, bcat, lam8, lamm_row, ccat, d.reshape(1, S5_WIDTH), w_glu.astype(jnp.bfloat16))
    return from_ks(y), hlast.reshape(bn, 2, S5_GROUPS, S5_STATE)


def s5_sample(u, state, a_re, a_im, log_dt, b_re, b_im, c_re, c_im, d, w_glu):
    bd, steps, _ = u.shape
    bcat, ccat, lam_row, _ = _s5_params(a_re, a_im, log_dt, b_re, b_im, c_re, c_im)
    y, hlast = pl.pallas_call(
        _s5_sample_kernel,
        out_shape=[jax.ShapeDtypeStruct((steps, bd, S5_WIDTH), jnp.float32),
                   jax.ShapeDtypeStruct((bd, 2 * S5_N), jnp.float32)],
        scratch_shapes=[pltpu.VMEM((bd, 2 * S5_N), jnp.float32)],
        compiler_params=pltpu.CompilerParams(vmem_limit_bytes=VMEM_LIMIT_BYTES),
        name="s5_sample",
    )(jnp.swapaxes(u, 0, 1), state.reshape(bd, 2 * S5_N), bcat, lam_row, ccat,
      d.reshape(1, S5_WIDTH), w_glu.astype(jnp.bfloat16))
    return jnp.swapaxes(y, 0, 1), hlast.reshape(bd, 2, S5_GROUPS, S5_STATE)


SAMPLE_PAGES_PER_STEP = 16


def _dsa_sample_kernel(pt_ref, qi_ref, wi_ref, q_ref, kinew_ref, kvnew_ref, *rest, topk, past_len):
    pg = SAMPLE_PAGES_PER_STEP
    idx_refs = rest[0:pg]
    kv_refs = rest[pg:2 * pg]
    o_ref, key_s, kv_all = rest[2 * pg:]
    cw = pg * PAGE_SIZE
    n_steps = past_len // cw
    steps_q = q_ref.shape[1]
    rows = key_s.shape[1]
    rep = B_HEADS // B_KV_HEADS
    kvw = B_KV_HEADS * 2 * B_DH
    s_id = pl.program_id(1)
    qpos = past_len + lax.broadcasted_iota(jnp.int32, (rows, 1), 0)

    qi = qi_ref[0]
    wi = wi_ref[0]
    qcat = jnp.concatenate([_split3_lhs(qi[:, h * IDX_DH:(h + 1) * IDX_DH]) for h in range(IDX_HEADS)], axis=0)
    row_ok = lax.broadcasted_iota(jnp.int32, (rows, 1), 0) < steps_q

    def scores(kidx_rows, kpos):
        s = lax.dot_general(qcat, _split3_rhs(kidx_rows), _NT, preferred_element_type=jnp.float32) * (IDX_DH ** -0.5)
        acc = None
        for h in range(IDX_HEADS):
            term = wi[:, h:h + 1] * jnp.maximum(s[h * steps_q:(h + 1) * steps_q], 0.0)
            acc = term if acc is None else acc + term
        acc = jnp.concatenate([acc, jnp.zeros((rows - steps_q, acc.shape[1]), jnp.float32)], axis=0)
        return jnp.where((kpos <= qpos) & row_ok, _f32_order_key(acc), INT_MIN)

    kpos = s_id * cw + lax.broadcasted_iota(jnp.int32, (rows, cw), 1)
    key_s[s_id] = scores(jnp.concatenate([r[0] for r in idx_refs], axis=0), kpos)
    for i in range(pg):
        start = pl.multiple_of(s_id * cw + i * PAGE_SIZE, PAGE_SIZE)
        kv_all[pl.ds(start, PAGE_SIZE), :] = kv_refs[i][0].astype(jnp.bfloat16)

    @pl.when(s_id == n_steps - 1)
    def _():
        pad = jnp.zeros((cw - steps_q, IDX_DH), jnp.float32)
        kpos_new = past_len + lax.broadcasted_iota(jnp.int32, (rows, cw), 1)
        new_key = scores(jnp.concatenate([kinew_ref[0], pad], axis=0), kpos_new)
        lane = lax.broadcasted_iota(jnp.int32, (rows, cw), 1)
        key_s[n_steps] = jnp.where(lane < steps_q, new_key, INT_MIN)
        kv_all[pl.ds(past_len, cw), :] = jnp.concatenate(
            [kvnew_ref[0], jnp.zeros((cw - steps_q, kvw), jnp.float32)], axis=0).astype(jnp.bfloat16)
        n_chunks = n_steps + 1
        thr, cut = _topk_threshold(key_s, n_chunks, topk, index_bits=int(math.ceil(math.log2(n_chunks * cw))))

        q = q_ref[0]
        qrows = []
        for h in range(B_HEADS):
            g = h // rep
            qh = q[:, h * B_DH:(h + 1) * B_DH]
            pieces = [jnp.zeros((steps_q, g * 2 * B_DH), jnp.float32)] if g else []
            pieces += [qh, jnp.zeros((steps_q, kvw - g * 2 * B_DH - B_DH), jnp.float32)]
            qrows.append(jnp.concatenate(pieces, axis=-1))
        qx = jnp.concatenate(qrows, axis=0).astype(jnp.bfloat16)
        nr = B_HEADS * steps_q

        def att_chunk(c, carry):
            m, l, acc = carry
            start = pl.multiple_of(c * cw, cw)
            kidx = start + lax.broadcasted_iota(jnp.int32, (rows, cw), 1)
            sel = _selected(key_s[c], kidx, thr, cut)[0:steps_q][None]
            kvc = kv_all[pl.ds(start, cw), :]
            s = lax.dot_general(qx, kvc, _NT, preferred_element_type=jnp.float32) * (B_DH ** -0.5)
            s = jnp.where(sel, s.reshape(B_HEADS, steps_q, cw), NEG)
            m_new = jnp.maximum(m, jnp.max(s, axis=-1, keepdims=True))
            p = jnp.where(sel, jnp.exp(s - m_new), 0.0)
            alpha = jnp.exp(m - m_new)
            l = alpha * l + jnp.sum(p, axis=-1, keepdims=True)
            pv = jnp.dot(p.reshape(nr, cw).astype(jnp.bfloat16), kvc, preferred_element_type=jnp.float32)
            return m_new, l, alpha * acc + pv.reshape(B_HEADS, steps_q, kvw)

        init = (jnp.full((B_HEADS, steps_q, 1), NEG, jnp.float32), jnp.zeros((B_HEADS, steps_q, 1), jnp.float32),
                jnp.zeros((B_HEADS, steps_q, kvw), jnp.float32))
        _, l, acc = lax.fori_loop(0, n_chunks, att_chunk, init)
        o = jnp.where(l > 0.0, acc / jnp.where(l > 0.0, l, 1.0), 0.0)
        heads = []
        for h in range(B_HEADS):
            g = h // rep
            heads.append(o[h][:, g * 2 * B_DH + B_DH:(g + 1) * 2 * B_DH])
        o_ref[0] = jnp.concatenate(heads, axis=-1)


def dsa_sample(q, kv, qi, ki, wi, cache_kv, cache_idx, page_table):
    bd, steps_q, _ = q.shape
    n_pages = page_table.shape[1]
    past_len = n_pages * PAGE_SIZE
    pg = SAMPLE_PAGES_PER_STEP
    assert n_pages % pg == 0
    n_steps = n_pages // pg
    cw = pg * PAGE_SIZE
    topk = min(DSA_TOPK, (past_len + steps_q) // 4)
    kvw = B_KV_HEADS * 2 * B_DH
    n_pool = cache_kv.shape[0]
    rows = V7X_SUBLANES
    assert steps_q <= rows

    def page_spec(i, width):
        return pl.BlockSpec((1, PAGE_SIZE, width), lambda b, s, pt: (pt[b * n_pages + s * pg + i], 0, 0))

    per_b = lambda shape: pl.BlockSpec((1,) + shape, lambda b, s, pt: (b, 0, 0))
    grid_spec = pltpu.PrefetchScalarGridSpec(
        num_scalar_prefetch=1,
        grid=(bd, n_steps),
        in_specs=[per_b((steps_q, IDX_HEADS * IDX_DH)), per_b((steps_q, IDX_HEADS)), per_b((steps_q, B_HEADS * B_DH)),
                  per_b((steps_q, IDX_DH)), per_b((steps_q, kvw))]
        + [page_spec(i, IDX_DH) for i in range(pg)] + [page_spec(i, kvw) for i in range(pg)],
        out_specs=per_b((steps_q, B_HEADS * B_DH)),
        scratch_shapes=[pltpu.VMEM((n_steps + 1, rows, cw), jnp.int32),
                        pltpu.VMEM((past_len + cw, kvw), jnp.bfloat16)],
    )
    return pl.pallas_call(
        functools.partial(_dsa_sample_kernel, topk=topk, past_len=past_len),
        grid_spec=grid_spec,
        out_shape=jax.ShapeDtypeStruct((bd, steps_q, B_HEADS * B_DH), jnp.float32),
        compiler_params=pltpu.CompilerParams(dimension_semantics=("arbitrary", "arbitrary"),
                                             vmem_limit_bytes=VMEM_LIMIT_BYTES),
        name="dsa_sample",
    )(page_table.reshape(-1), qi, wi, q, ki, kv.reshape(bd, steps_q, kvw),
      *([cache_idx] * pg), *([cache_kv.reshape(n_pool, PAGE_SIZE, kvw)] * pg))


CMP_PAGES_PER_STEP = 32


def _nsa_compress_kernel(pt_ref, cmp_pos_ref, w1_ref, w2_ref, *rest):
    pg = CMP_PAGES_PER_STEP
    page_refs = rest[:pg]
    kck_ref, kcv_ref, xs = rest[pg:]
    bpp = PAGE_SIZE // CMP_BLOCK
    nblk = pg * bpp
    slabs = C_KV * 2
    prow = PAGE_SIZE * slabs
    for i in range(pg):
        xs[i * prow:(i + 1) * prow, :] = page_refs[i][0]
    for c, out_ref in enumerate((kck_ref, kcv_ref)):
        acc = jnp.zeros((C_KV * nblk, C_DH), jnp.float32)
        for pp in range(CMP_BLOCK // 2):
            cols = []
            for pos in (2 * pp, 2 * pp + 1):
                bias = cmp_pos_ref[pos:pos + 1, c * C_DH:(c + 1) * C_DH]
                cols.append(jnp.concatenate(
                    [xs[pl.ds(pos * slabs + g * 2 + c, nblk, stride=CMP_BLOCK * slabs), :] + bias
                     for g in range(C_KV)], axis=0))
            lhs = jnp.concatenate(cols, axis=-1).astype(jnp.bfloat16)
            acc = acc + jnp.dot(lhs, w1_ref[c, pp], preferred_element_type=jnp.float32)
        kc = jnp.dot(jax.nn.gelu(acc).astype(jnp.bfloat16), w2_ref[c], preferred_element_type=jnp.float32)
        for g in range(C_KV):
            out_ref[0, g] = kc[g * nblk:(g + 1) * nblk]


def nsa_compress(pages, page_table, cmp_pos, cmp_w1, cmp_w2):
    bn, n_pages = page_table.shape
    pg = CMP_PAGES_PER_STEP
    assert n_pages % pg == 0
    n_pool = pages.shape[0]
    prow = PAGE_SIZE * C_KV * 2
    bpp = PAGE_SIZE // CMP_BLOCK
    nb = n_pages * bpp
    w1 = cmp_w1.reshape(2, CMP_BLOCK // 2, 2 * C_DH, C_DH).astype(jnp.bfloat16)
    w2 = cmp_w2.astype(jnp.bfloat16)

    def page_spec(i):
        return pl.BlockSpec((1, prow, C_DH), lambda b, s, pt: (pt[b * n_pages + s * pg + i], 0, 0))

    const = lambda shape: pl.BlockSpec(shape, lambda b, s, pt: (0,) * len(shape))
    out_spec = pl.BlockSpec((1, C_KV, pg * bpp, C_DH), lambda b, s, pt: (b, 0, s, 0))
    grid_spec = pltpu.PrefetchScalarGridSpec(
        num_scalar_prefetch=1,
        grid=(bn, n_pages // pg),
        in_specs=[const((CMP_BLOCK, 2 * C_DH)), const(w1.shape), const(w2.shape)] + [page_spec(i) for i in range(pg)],
        out_specs=[out_spec, out_spec],
        scratch_shapes=[pltpu.VMEM((pg * prow, C_DH), jnp.float32)],
    )
    shape = jax.ShapeDtypeStruct((bn, C_KV, nb, C_DH), jnp.float32)
    return pl.pallas_call(
        _nsa_compress_kernel,
        grid_spec=grid_spec,
        out_shape=[shape, shape],
        compiler_params=pltpu.CompilerParams(dimension_semantics=("arbitrary", "arbitrary"),
                                             vmem_limit_bytes=VMEM_LIMIT_BYTES),
        name="nsa_compress",
    )(page_table.reshape(-1), cmp_pos.reshape(CMP_BLOCK, 2 * C_DH), w1, w2,
      *([pages.reshape(n_pool, prow, C_DH)] * pg))


def _nsa_sample_attend_kernel(qraw_ref, qrot_ref, kck_ref, kcv_ref, win_ref, kvw_ref, tri_ref,
                              ocmp_ref, owin_ref, sel_ref, key_s, *, past_len, n_sel):
    rep = C_REP
    scale = C_DH ** -0.5
    steps_q = qraw_ref.shape[1]
    rows = key_s.shape[1]
    nbp = key_s.shape[2]
    nb = kck_ref.shape[2]
    wbuf = win_ref.shape[1] // (C_KV * 2)
    qpos = past_len + lax.broadcasted_iota(jnp.int32, (steps_q, 1), 0)
    blk_c = lax.broadcasted_iota(jnp.int32, (steps_q, nb), 1)
    cmask = ((blk_c + 1) * CMP_BLOCK - 1) <= qpos
    any_c = jnp.where(jnp.max(jnp.where(cmask, 1.0, 0.0), axis=-1, keepdims=True) > 0.5, 1.0, 0.0)
    blk = lax.broadcasted_iota(jnp.int32, (steps_q, nbp), 1)
    cur = qpos // SEL_BLOCK
    forced = (blk == 0) | (blk == cur) | (blk == cur - 1)
    nbs = -(-(past_len + steps_q) // SEL_BLOCK)
    valid = (blk * SEL_BLOCK <= qpos) & (blk < nbs)
    lane = lax.broadcasted_iota(jnp.int32, (rows, nbp), 1)
    wlen = wbuf + V7X_SUBLANES
    wcol = lax.broadcasted_iota(jnp.int32, (steps_q, wlen), 1)
    wpos = jnp.where(wcol < wbuf, past_len - wbuf + wcol, past_len + wcol - wbuf)
    wmask = ((wpos >= 0) & (wpos <= qpos) & (wpos > qpos - WINDOW) & (wcol < wbuf + steps_q))[None]
    qraw = qraw_ref[0]
    qrot = qrot_ref[0]

    def stack(x, g):
        return jnp.concatenate([x[:, (g * rep + r) * C_DH:(g * rep + r + 1) * C_DH] for r in range(rep)], axis=0)

    pad_rows = lambda x: jnp.concatenate([x, jnp.zeros((rows - steps_q,) + x.shape[1:], x.dtype)], axis=0)

    for g in range(C_KV):
        s = lax.dot_general(stack(qraw, g), kck_ref[0, g], _NT, precision=lax.Precision.HIGHEST,
                            preferred_element_type=jnp.float32) * scale
        s = jnp.where(cmask[None], s.reshape(rep, steps_q, nb), NEG)
        e = jnp.exp(s - jnp.max(s, axis=-1, keepdims=True))
        p = e / jnp.sum(e, axis=-1, keepdims=True) * any_c[None]
        o_cmp = jnp.dot(p.reshape(rep * steps_q, nb), kcv_ref[0, g], precision=lax.Precision.HIGHEST,
                        preferred_element_type=jnp.float32)
        imp = p[0]
        for r in range(1, rep):
            imp = imp + p[r]
        imp = jnp.concatenate([imp, jnp.zeros((steps_q, nbp - nb), jnp.float32)], axis=-1)
        score = jnp.where(forced, FORCE, imp)
        key_s[0] = pad_rows(jnp.where(valid, _f32_order_key(score), INT_MIN))
        thr, cut = _topk_threshold(key_s, 1, n_sel, index_bits=int(math.ceil(math.log2(nbp))))
        mask = _selected(key_s[0], lane, thr, cut)
        prefix = jnp.dot(jnp.where(mask, 1.0, 0.0).astype(jnp.bfloat16), tri_ref[...],
                         preferred_element_type=jnp.float32)
        ids = [jnp.sum(jnp.where(mask & (prefix == float(t + 1)), lane, 0), axis=-1, keepdims=True)
               for t in range(n_sel)]
        ids.append(jnp.zeros((rows, V7X_LANES - n_sel), jnp.int32))
        sel_ref[0, g] = jnp.concatenate(ids, axis=-1)

        qg = stack(qrot, g).astype(jnp.bfloat16)
        newkv = kvw_ref[0]
        padn = jnp.zeros((V7X_SUBLANES - steps_q, C_DH), jnp.float32)
        slabs = C_KV * 2
        kwin = jnp.concatenate([win_ref.at[0][pl.ds(2 * g, wbuf, stride=slabs), :],
                                newkv[:, (2 * g) * C_DH:(2 * g + 1) * C_DH], padn], axis=0).astype(jnp.bfloat16)
        vwin = jnp.concatenate([win_ref.at[0][pl.ds(2 * g + 1, wbuf, stride=slabs), :],
                                newkv[:, (2 * g + 1) * C_DH:(2 * g + 2) * C_DH], padn], axis=0).astype(jnp.bfloat16)
        sw = lax.dot_general(qg, kwin, _NT, preferred_element_type=jnp.float32) * scale
        sw = jnp.where(wmask, sw.reshape(rep, steps_q, wlen), NEG)
        ew = jnp.where(wmask, jnp.exp(sw - jnp.max(sw, axis=-1, keepdims=True)), 0.0)
        lw = jnp.sum(ew, axis=-1, keepdims=True)
        pw = ew / jnp.where(lw > 0.0, lw, 1.0)
        o_win = jnp.dot(pw.reshape(rep * steps_q, wlen).astype(jnp.bfloat16), vwin, preferred_element_type=jnp.float32)
        for r in range(rep):
            hh = g * rep + r
            ocmp_ref[0, :, hh * C_DH:(hh + 1) * C_DH] = o_cmp[r * steps_q:(r + 1) * steps_q]
            owin_ref[0, :, hh * C_DH:(hh + 1) * C_DH] = o_win[r * steps_q:(r + 1) * steps_q]


def nsa_sample_attend(q_raw, q_rot, kck, kcv, win, kv_w):
    bd, steps_q, qd = q_raw.shape
    nb = kck.shape[2]
    past_len = nb * CMP_BLOCK
    nbs = -(-(past_len + steps_q) // SEL_BLOCK)
    assert nbs >= N_SEL and steps_q <= V7X_SUBLANES
    nbp = -(-nbs // V7X_LANES) * V7X_LANES
    wbuf = win.shape[1]
    roww = C_KV * 2 * C_DH
    tri = (jnp.arange(nbp)[:, None] <= jnp.arange(nbp)[None, :]).astype(jnp.bfloat16)
    per_b = lambda shape: pl.BlockSpec((1,) + shape, lambda b: (b,) + (0,) * len(shape))
    return pl.pallas_call(
        functools.partial(_nsa_sample_attend_kernel, past_len=past_len, n_sel=N_SEL),
        grid=(bd,),
        in_specs=[per_b((steps_q, qd)), per_b((steps_q, qd)), per_b((C_KV, nb, C_DH)), per_b((C_KV, nb, C_DH)),
                  per_b((wbuf * C_KV * 2, C_DH)), per_b((steps_q, roww)), pl.BlockSpec((nbp, nbp), lambda b: (0, 0))],
        out_specs=[per_b((steps_q, qd)), per_b((steps_q, qd)), per_b((C_KV, V7X_SUBLANES, V7X_LANES))],
        out_shape=[jax.ShapeDtypeStruct((bd, steps_q, qd), jnp.float32),
                   jax.ShapeDtypeStruct((bd, steps_q, qd), jnp.float32),
                   jax.ShapeDtypeStruct((bd, C_KV, V7X_SUBLANES, V7X_LANES), jnp.int32)],
        scratch_shapes=[pltpu.VMEM((1, V7X_SUBLANES, nbp), jnp.int32)],
        compiler_params=pltpu.CompilerParams(dimension_semantics=("arbitrary",), vmem_limit_bytes=VMEM_LIMIT_BYTES),
        name="nsa_sample_attend",
    )(q_raw, q_rot, kck, kcv, win.reshape(bd, wbuf * C_KV * 2, C_DH), kv_w.reshape(bd, steps_q, roww), tri)


def _nsa_sample_selected_kernel(phys_ref, blk_ref, q_ref, gates_ref, ocmp_ref, owin_ref, new_ref, *rest,
                                past_len, steps_q):
    blk_refs = rest[:C_KV * N_SEL]
    o_ref = rest[C_KV * N_SEL]
    i = pl.program_id(0)
    qpos = past_len + i % steps_q
    scale = C_DH ** -0.5
    slabs = C_KV * 2
    row = lax.broadcasted_iota(jnp.int32, (1, SEL_BLOCK), 1)
    ncol = lax.broadcasted_iota(jnp.int32, (1, V7X_SUBLANES), 1)
    nb_past = past_len // SEL_BLOCK
    padn = jnp.zeros((V7X_SUBLANES - steps_q, C_DH), jnp.float32)
    sig = jax.nn.sigmoid(gates_ref[0])
    newkv = new_ref[0]
    for g in range(C_KV):
        q = q_ref[0, g * C_REP:(g + 1) * C_REP, :].astype(jnp.bfloat16)
        ks, vs, kpos = [], [], []
        has_new = jnp.int32(0)
        for t in range(N_SEL):
            b_id = blk_ref[(i * C_KV + g) * N_SEL + t]
            is_new = b_id >= nb_past
            has_new = jnp.maximum(has_new, is_new.astype(jnp.int32))
            ref = blk_refs[g * N_SEL + t].at[0]
            ks.append(ref[pl.ds(2 * g, SEL_BLOCK, stride=slabs), :])
            vs.append(ref[pl.ds(2 * g + 1, SEL_BLOCK, stride=slabs), :])
            kpos.append(jnp.where(is_new, jnp.int32(INT_MAX), b_id * SEL_BLOCK + row))
        ks.append(jnp.concatenate([newkv[:, (2 * g) * C_DH:(2 * g + 1) * C_DH], padn], axis=0))
        vs.append(jnp.concatenate([newkv[:, (2 * g + 1) * C_DH:(2 * g + 2) * C_DH], padn], axis=0))
        kpos.append(jnp.where((has_new > 0) & (ncol < steps_q), past_len + ncol, jnp.int32(INT_MAX)))
        k = jnp.concatenate(ks, axis=0).astype(jnp.bfloat16)
        v = jnp.concatenate(vs, axis=0).astype(jnp.bfloat16)
        mask = jnp.concatenate(kpos, axis=-1) <= qpos
        s = lax.dot_general(q, k, _NT, preferred_element_type=jnp.float32) * scale
        s = jnp.where(mask, s, NEG)
        e = jnp.where(mask, jnp.exp(s - jnp.max(s, axis=-1, keepdims=True)), 0.0)
        l = jnp.sum(e, axis=-1, keepdims=True)
        p = e / jnp.where(l > 0.0, l, 1.0)
        o_slc = jnp.dot(p.astype(jnp.bfloat16), v, preferred_element_type=jnp.float32)
        hs = slice(g * C_REP, (g + 1) * C_REP)
        o_ref[0, hs, :] = (sig[hs, 0:1] * ocmp_ref[0, hs, :] + sig[hs, 1:2] * o_slc
                           + sig[hs, 2:3] * owin_ref[0, hs, :])


def nsa_sample_selected(sel_idx, q_rot, gates, o_cmp, o_win, kv_s, cache_slc, page_table):
    bd, steps_q, qd = q_rot.shape
    n_pages = page_table.shape[1]
    past_len = n_pages * PAGE_SIZE
    n_pool = cache_slc.shape[0]
    bpp = PAGE_SIZE // SEL_BLOCK
    slabs = C_KV * 2
    blk = jnp.transpose(sel_idx[:, :, :steps_q, :N_SEL], (0, 2, 1, 3))
    page = jnp.minimum(blk // bpp, n_pages - 1)
    phys = jnp.take_along_axis(page_table, page.reshape(bd, -1), axis=1).reshape(blk.shape) * bpp + blk % bpp
    n = bd * steps_q
    heads = lambda a: a.reshape(n, C_HEADS, C_DH)

    def blk_spec(g, t):
        return pl.BlockSpec((1, SEL_BLOCK * slabs, C_DH),
                            lambda i, ph, bl: (ph[(i * C_KV + g) * N_SEL + t], 0, 0))

    row_spec = lambda shape: pl.BlockSpec((1,) + shape, lambda i, ph, bl: (i,) + (0,) * len(shape))
    grid_spec = pltpu.PrefetchScalarGridSpec(
        num_scalar_prefetch=2,
        grid=(n,),
        in_specs=[row_spec((C_HEADS, C_DH)), row_spec((C_HEADS, 3)), row_spec((C_HEADS, C_DH)),
                  row_spec((C_HEADS, C_DH)),
                  pl.BlockSpec((1, steps_q, slabs * C_DH), lambda i, ph, bl: (i // steps_q, 0, 0))]
        + [blk_spec(g, t) for g in range(C_KV) for t in range(N_SEL)],
        out_specs=row_spec((C_HEADS, C_DH)),
    )
    o = pl.pallas_call(
        functools.partial(_nsa_sample_selected_kernel, past_len=past_len, steps_q=steps_q),
        grid_spec=grid_spec,
        out_shape=jax.ShapeDtypeStruct((n, C_HEADS, C_DH), jnp.float32),
        compiler_params=pltpu.CompilerParams(dimension_semantics=("arbitrary",), vmem_limit_bytes=VMEM_LIMIT_BYTES),
        name="nsa_sample_selected",
    )(phys.reshape(-1).astype(jnp.int32), blk.reshape(-1).astype(jnp.int32),
      heads(q_rot), gates.reshape(n, C_HEADS, 3), heads(o_cmp), heads(o_win),
      kv_s.reshape(bd, steps_q, slabs * C_DH),
      *([cache_slc.reshape(n_pool * bpp, SEL_BLOCK * slabs, C_DH)] * (C_KV * N_SEL)))
    return o.reshape(bd, steps_q, qd)


LN_ROW_TILE = 512


def _layer_norm_rows(y, g, b):
    mu = jnp.mean(y, axis=-1, keepdims=True)
    var = jnp.mean(jnp.square(y - mu), axis=-1, keepdims=True)
    return (y - mu) * lax.rsqrt(var + LN_EPS) * g + b


def _outproj_ln_kernel(*refs, n_lhs):
    lhs_refs, w_refs = refs[:n_lhs], refs[n_lhs:2 * n_lhs]
    x_ref, g_ref, b_ref, o_ref, obf_ref = refs[2 * n_lhs:]
    acc = None
    for a, w in zip(lhs_refs, w_refs):
        t = jnp.dot(a[...].astype(jnp.bfloat16), w[...], preferred_element_type=jnp.float32)
        acc = t if acc is None else acc + t
    out = _layer_norm_rows(ALPHA * x_ref[...] + acc, g_ref[...], b_ref[...])
    o_ref[...] = out
    obf_ref[...] = out.astype(jnp.bfloat16)


def outproj_ln(lhs_list, w_out, x, g, b):
    t = x.shape[0]
    tm = min(LN_ROW_TILE, t)
    assert t % tm == 0
    ws, start = [], 0
    for a in lhs_list:
        ws.append(w_out[start:start + a.shape[1]].astype(jnp.bfloat16))
        start += a.shape[1]
    rows = lambda width: pl.BlockSpec((tm, width), lambda i: (i, 0))
    const = lambda shape: pl.BlockSpec(shape, lambda i: (0, 0))
    return pl.pallas_call(
        functools.partial(_outproj_ln_kernel, n_lhs=len(lhs_list)),
        grid=(t // tm,),
        in_specs=[rows(a.shape[1]) for a in lhs_list] + [const(w.shape) for w in ws]
        + [rows(D_MODEL), const((1, D_MODEL)), const((1, D_MODEL))],
        out_specs=[rows(D_MODEL), rows(D_MODEL)],
        out_shape=[jax.ShapeDtypeStruct((t, D_MODEL), jnp.float32), jax.ShapeDtypeStruct((t, D_MODEL), jnp.bfloat16)],
        compiler_params=pltpu.CompilerParams(dimension_semantics=("arbitrary",), vmem_limit_bytes=VMEM_LIMIT_BYTES),
        name="outproj_ln",
    )(*lhs_list, *ws, x, g.reshape(1, D_MODEL), b.reshape(1, D_MODEL))


def _combine_ln_kernel(x_ref, ya_ref, yb_ref, g_ref, b_ref, o_ref):
    o_ref[...] = _layer_norm_rows(ALPHA * x_ref[...] + (ya_ref[...] + yb_ref[...]), g_ref[...], b_ref[...])


def combine_ln(x, ya, yb, g, b):
    t = x.shape[0]
    tm = min(LN_ROW_TILE, t)
    assert t % tm == 0
    rows = pl.BlockSpec((tm, D_MODEL), lambda i: (i, 0))
    const = pl.BlockSpec((1, D_MODEL), lambda i: (0, 0))
    return pl.pallas_call(
        _combine_ln_kernel,
        grid=(t // tm,),
        in_specs=[rows, rows, rows, const, const],
        out_specs=rows,
        out_shape=jax.ShapeDtypeStruct((t, D_MODEL), jnp.float32),
        compiler_params=pltpu.CompilerParams(dimension_semantics=("arbitrary",), vmem_limit_bytes=VMEM_LIMIT_BYTES),
        name="combine_ln",
    )(x, ya, yb, g.reshape(1, D_MODEL), b.reshape(1, D_MODEL))


def split_cols(h, sizes):
    out, start = [], 0
    for n in sizes:
        out.append(h[..., start:start + n])
        start += n
    return out


def rope_partial(x, pos, rot_dim):
    half = rot_dim // 2
    inv = ROPE_THETA ** (-jnp.arange(half, dtype=jnp.float32) / half)
    ang = pos.astype(jnp.float32)[:, None] * inv
    cos, sin = jnp.cos(ang)[:, None, :], jnp.sin(ang)[:, None, :]
    x1 = x[..., :half]
    x2 = x[..., half:rot_dim]
    rot = jnp.concatenate([x1 * cos - x2 * sin, x2 * cos + x1 * sin], -1)
    return jnp.concatenate([rot, x[..., rot_dim:]], -1)


def rope_kv(kv, pos, rot_dim):
    k = rope_partial(kv[..., 0, :], pos, rot_dim)
    return jnp.stack([k, kv[..., 1, :]], axis=-2)


def mixer_ab(x, pos, w_in, a_re, a_im, log_dt, b_re, b_im, c_re, c_im, d, w_glu,
             state=None, cache_kv=None, cache_idx=None, page_table=None):
    bn, s, _ = x.shape
    s5p = (a_re, a_im, log_dt, b_re, b_im, c_re, c_im, d, w_glu)
    u, q, kv, qi, ki, wi = split_cols(x @ w_in, AB_SIZES)
    q = rope_partial(q.reshape(bn, s, B_HEADS, B_DH), pos, B_ROT)
    kv = rope_kv(kv.reshape(bn, s, B_KV_HEADS, 2, B_DH), pos, B_ROT)
    qi = rope_partial(qi.reshape(bn, s, IDX_HEADS, IDX_DH), pos, IDX_ROT)
    ki = rope_partial(ki.reshape(bn, s, 1, IDX_DH), pos, IDX_ROT)[:, :, 0]
    if state is None:
        o_b = dsa_prompt(q.reshape(bn, s, B_HEADS * B_DH), kv, qi.reshape(bn, s, IDX_HEADS * IDX_DH), ki, wi)
        y_s5, new_a = s5_prompt(u, *s5p)
    else:
        o_b = dsa_sample(q.reshape(bn, s, B_HEADS * B_DH), kv, qi.reshape(bn, s, IDX_HEADS * IDX_DH), ki, wi,
                         cache_kv, cache_idx, page_table)
        y_s5, new_a = s5_sample(u, state, *s5p)
    return [y_s5.reshape(bn * s, S5_WIDTH), o_b.reshape(bn * s, B_HEADS * B_DH)], new_a, kv, ki


def mixer_c(x, pos, w_in, cmp_pos, cmp_w1, cmp_w2,
            cache_cmp=None, cache_slc=None, win=None, page_table=None):
    bn, s, _ = x.shape
    q, kv_c, kv_s, kv_w, gates = split_cols(x @ w_in, C_SIZES)
    q = q.reshape(bn, s, C_HEADS, C_DH)
    q_rot = rope_partial(q, pos, C_ROT)
    kv_c = kv_c.reshape(bn, s, C_KV, 2, C_DH)
    kv_s = rope_kv(kv_s.reshape(bn, s, C_KV, 2, C_DH), pos, C_ROT)
    kv_w = rope_kv(kv_w.reshape(bn, s, C_KV, 2, C_DH), pos, C_ROT)
    qd = C_HEADS * C_DH
    q_raw2, q_rot2 = q.reshape(bn, s, qd), q_rot.reshape(bn, s, qd)
    if cache_cmp is None:
        n_pages = s // PAGE_SIZE
        ident = jnp.arange(bn * n_pages, dtype=jnp.int32).reshape(bn, n_pages)
        kck, kcv = nsa_compress(kv_c.reshape(bn * n_pages, PAGE_SIZE, C_KV, 2, C_DH), ident, cmp_pos, cmp_w1, cmp_w2)
        o = nsa_prompt(q_raw2, q_rot2, gates, kck, kcv, kv_s, kv_w)
        return [o.reshape(bn * s, qd)], kv_c, kv_s, kv_w[:, -min(WINDOW, s):]
    kck, kcv = nsa_compress(cache_cmp, page_table, cmp_pos, cmp_w1, cmp_w2)
    o_cmp, o_win, sel_idx = nsa_sample_attend(q_raw2, q_rot2, kck, kcv, win, kv_w)
    o = nsa_sample_selected(sel_idx, q_rot2, gates, o_cmp, o_win, kv_s, cache_slc, page_table)
    new_win = jnp.concatenate([win, kv_w], axis=1)[:, -win.shape[1]:]
    return [o.reshape(bn * s, qd)], kv_c, kv_s, new_win


def kernel(x_prompt, x_sample, state_a, cache_b_kv, cache_b_idx, cache_c_cmp, cache_c_slc, state_c_win, page_table,
           w_in_ab, s5_a_re, s5_a_im, s5_log_dt, s5_b_re, s5_b_im, s5_c_re, s5_c_im, s5_d, s5_w_glu, w_out_ab,
           w_in_c, cmp_pos, cmp_w1, cmp_w2, w_out_c,
           ln1_g, ln1_b, ln2_g, ln2_b, moe_w_group, moe_w_expert, moe_w_gate, moe_w_up, moe_w_down):
    past_len = page_table.shape[1] * PAGE_SIZE
    pos_p = jnp.arange(x_prompt.shape[1])
    pos_s = past_len + jnp.arange(x_sample.shape[1])
    xp, xs = x_prompt, x_sample
    pshape, sshape = xp.shape, xs.shape
    n_p = pshape[0] * pshape[1]
    outs = {k: [] for k in ('a_p', 'a_s', 'bkv_p', 'bkv_s', 'bidx_p', 'bidx_s',
                            'ccmp_p', 'ccmp_s', 'cslc_p', 'cslc_s', 'cwin_p', 'cwin_s')}
    for i in range(DEPTH):
        j = i // 2
        if i % 2 == 0:
            ab = (w_in_ab[j], s5_a_re[j], s5_a_im[j], s5_log_dt[j], s5_b_re[j], s5_b_im[j],
                  s5_c_re[j], s5_c_im[j], s5_d[j], s5_w_glu[j])
            w_out = w_out_ab[j]
            mp, st_p, kv_p, ki_p = mixer_ab(xp, pos_p, *ab)
            ms, st_s, kv_n, ki_n = mixer_ab(xs, pos_s, *ab, state=state_a[j], cache_kv=cache_b_kv[j],
                                            cache_idx=cache_b_idx[j], page_table=page_table)
            outs['a_p'].append(st_p); outs['a_s'].append(st_s)
            outs['bkv_p'].append(kv_p); outs['bkv_s'].append(kv_n)
            outs['bidx_p'].append(ki_p); outs['bidx_s'].append(ki_n)
        else:
            cp = (w_in_c[j], cmp_pos[j], cmp_w1[j], cmp_w2[j])
            w_out = w_out_c[j]
            mp, kc_p, ksl_p, win_p = mixer_c(xp, pos_p, *cp)
            ms, kc_n, ksl_n, win_n = mixer_c(xs, pos_s, *cp, cache_cmp=cache_c_cmp[j], cache_slc=cache_c_slc[j],
                                             win=state_c_win[j], page_table=page_table)
            outs['ccmp_p'].append(kc_p); outs['ccmp_s'].append(kc_n)
            outs['cslc_p'].append(ksl_p); outs['cslc_s'].append(ksl_n)
            outs['cwin_p'].append(win_p); outs['cwin_s'].append(win_n)
        xp2, xp_bf = outproj_ln(mp, w_out, xp.reshape(n_p, D_MODEL), ln1_g[i], ln1_b[i])
        xs2, xs_bf = outproj_ln(ms, w_out, xs.reshape(-1, D_MODEL), ln1_g[i], ln1_b[i])
        moe = (moe_w_group[i], moe_w_expert[i], moe_w_gate[i], moe_w_up[i], moe_w_down[i])
        (yp_a, yp_b), (ys_a, ys_b) = hier_moe([xp2, xs2], [xp_bf, xs_bf], *moe)
        xp = combine_ln(xp2, yp_a, yp_b, ln2_g[i], ln2_b[i]).reshape(pshape)
        xs = combine_ln(xs2, ys_a, ys_b, ln2_g[i], ln2_b[i]).reshape(sshape)
    st = lambda k: jnp.stack(outs[k])
    return (xp, xs, st('a_p'), st('a_s'), st('bkv_p'), st('bkv_s'), st('bidx_p'), st('bidx_s'),
            st('ccmp_p'), st('ccmp_s'), st('cslc_p'), st('cslc_s'), st('cwin_p'), st('cwin_s'))
```

```python
import functools
import math

import jax
import jax.numpy as jnp
import numpy as np
from jax import lax
from jax.experimental import pallas as pl
from jax.experimental.pallas import tpu as pltpu

D_MODEL = 1024
DEPTH = 2
PAGE_SIZE = 128

S5_WIDTH = D_MODEL // 2
S5_GROUP = 16
S5_GROUPS = S5_WIDTH // S5_GROUP
S5_STATE = 64
B_HEADS = 8
B_DH = 64
B_KV_HEADS = 2
B_ROT = B_DH // 4
IDX_HEADS = 4
IDX_DH = 64
IDX_ROT = IDX_DH // 4
DSA_TOPK = 256
C_HEADS = 8
C_DH = 128
C_KV = 2
C_REP = C_HEADS // C_KV
C_ROT = C_DH // 4
CMP_BLOCK = 64
SEL_BLOCK = 64
N_SEL = 16
WINDOW = 512
N_EGROUPS = 4
E_PER_GROUP = 8
N_EXPERTS = N_EGROUPS * E_PER_GROUP
TOP_E = 2
D_FF_E = 512

ROPE_THETA = 500000.0
ALPHA = (2 * DEPTH) ** 0.25
Q_BLOCK = 128
LN_EPS = 1e-5
NEG = -1e30
FORCE = 1e4
LOG2E = math.log2(math.e)

AB_SIZES = (S5_WIDTH, B_HEADS * B_DH, B_KV_HEADS * 2 * B_DH, IDX_HEADS * IDX_DH, IDX_DH, IDX_HEADS)
C_SIZES = (C_HEADS * C_DH, C_KV * 2 * C_DH, C_KV * 2 * C_DH, C_KV * 2 * C_DH, C_HEADS * 3)

V7X_LANES = 128
V7X_SUBLANES = 8
V7X_VMEM_BYTES = 64 * 1024 * 1024
VMEM_LIMIT_BYTES = 48 * 1024 * 1024

MOE_ROUTER_TILE = 512
MOE_EXPERT_TILE = 256


def _moe_router_kernel(x_ref, w_ref, o_ref):
    logits = jnp.dot(x_ref[...], w_ref[...], precision=lax.Precision.HIGHEST,
                     preferred_element_type=jnp.float32)
    col = lax.broadcasted_iota(jnp.int32, logits.shape, 1)
    big = jnp.int32(1 << 20)
    ninf = jnp.float32(-jnp.inf)
    gl = jnp.where(col < N_EGROUPS, logits, ninf)
    gmax = jnp.max(gl, axis=-1, keepdims=True)
    g_top = jnp.min(jnp.where(gl == gmax, col, big), axis=-1, keepdims=True)
    g_w = 1.0 / jnp.sum(jnp.exp(gl - gmax), axis=-1, keepdims=True)
    lo = N_EGROUPS + E_PER_GROUP * g_top
    el = jnp.where((col >= lo) & (col < lo + E_PER_GROUP), logits, ninf)
    v1 = jnp.max(el, axis=-1, keepdims=True)
    i1 = jnp.min(jnp.where(el == v1, col, big), axis=-1, keepdims=True)
    el2 = jnp.where(col == i1, ninf, el)
    v2 = jnp.max(el2, axis=-1, keepdims=True)
    i2 = jnp.min(jnp.where(el2 == v2, col, big), axis=-1, keepdims=True)
    e2 = jnp.exp(v2 - v1)
    den = 1.0 + e2
    w1 = g_w / den
    w2 = g_w * e2 / den
    out = jnp.where(col == 0, (i1 - N_EGROUPS).astype(jnp.float32),
                    jnp.where(col == 1, (i2 - N_EGROUPS).astype(jnp.float32),
                              jnp.where(col == 2, w1, jnp.where(col == 3, w2, 0.0))))
    o_ref[...] = out


def _moe_route(x, w_router):
    t = x.shape[0]
    tm = min(MOE_ROUTER_TILE, t)
    assert t % tm == 0
    return pl.pallas_call(
        _moe_router_kernel,
        grid=(t // tm,),
        in_specs=[pl.BlockSpec((tm, D_MODEL), lambda i: (i, 0)),
                  pl.BlockSpec((D_MODEL, V7X_LANES), lambda i: (0, 0))],
        out_specs=pl.BlockSpec((tm, V7X_LANES), lambda i: (i, 0)),
        out_shape=jax.ShapeDtypeStruct((t, V7X_LANES), jnp.float32),
        name="moe_router",
    )(x, w_router)


def _moe_expert_kernel(tile_e_ref, nvalid_ref, xs_ref, wg_ref, wu_ref, wd_ref, sw_ref, o_ref,
                       wg_s, wu_s, wd_s):
    i = pl.program_id(0)
    valid = i < nvalid_ref[0]
    prev_e = tile_e_ref[jnp.maximum(i - 1, 0)]
    new_expert = jnp.logical_or(i == 0, tile_e_ref[i] != prev_e)

    @pl.when(jnp.logical_and(valid, new_expert))
    def _():
        wg_s[...] = wg_ref[0].astype(jnp.bfloat16)
        wu_s[...] = wu_ref[0].astype(jnp.bfloat16)
        wd_s[...] = wd_ref[0].astype(jnp.bfloat16)

    @pl.when(valid)
    def _():
        x = xs_ref[...]
        g = jnp.dot(x, wg_s[...], preferred_element_type=jnp.float32)
        u = jnp.dot(x, wu_s[...], preferred_element_type=jnp.float32)
        h = (g * jax.nn.sigmoid(g)) * u
        y = jnp.dot(h.astype(jnp.bfloat16), wd_s[...], preferred_element_type=jnp.float32)
        o_ref[...] = y * sw_ref[...]

    @pl.when(jnp.logical_not(valid))
    def _():
        o_ref[...] = jnp.zeros_like(o_ref)


def _moe_experts(tile_e, nvalid, xs, slot_w, w_gate, w_up, w_down):
    np_rows = xs.shape[0]
    tm = MOE_EXPERT_TILE
    n_tiles = np_rows // tm
    grid_spec = pltpu.PrefetchScalarGridSpec(
        num_scalar_prefetch=2,
        grid=(n_tiles,),
        in_specs=[
            pl.BlockSpec((tm, D_MODEL), lambda i, te, nv: (i, 0)),
            pl.BlockSpec((1, D_MODEL, D_FF_E), lambda i, te, nv: (te[i], 0, 0)),
            pl.BlockSpec((1, D_MODEL, D_FF_E), lambda i, te, nv: (te[i], 0, 0)),
            pl.BlockSpec((1, D_FF_E, D_MODEL), lambda i, te, nv: (te[i], 0, 0)),
            pl.BlockSpec((tm, 1), lambda i, te, nv: (i, 0)),
        ],
        out_specs=pl.BlockSpec((tm, D_MODEL), lambda i, te, nv: (i, 0)),
        scratch_shapes=[pltpu.VMEM((D_MODEL, D_FF_E), jnp.bfloat16),
                        pltpu.VMEM((D_MODEL, D_FF_E), jnp.bfloat16),
                        pltpu.VMEM((D_FF_E, D_MODEL), jnp.bfloat16)],
    )
    return pl.pallas_call(
        _moe_expert_kernel,
        grid_spec=grid_spec,
        out_shape=jax.ShapeDtypeStruct((np_rows, D_MODEL), jnp.float32),
        compiler_params=pltpu.CompilerParams(dimension_semantics=("arbitrary",),
                                             vmem_limit_bytes=VMEM_LIMIT_BYTES),
        name="moe_experts",
    )(tile_e, nvalid, xs, w_gate, w_up, w_down, slot_w)


def _moe_padded_rows(t):
    tm = MOE_EXPERT_TILE
    slots = t * TOP_E
    return ((slots + N_EXPERTS * (tm - 1)) // tm + 1) * tm


def hier_moe(xs_list, xs_bf16_list, w_group, w_expert, w_gate, w_up, w_down):
    tm = MOE_EXPERT_TILE
    w_router = jnp.zeros((D_MODEL, V7X_LANES), jnp.float32)
    w_router = w_router.at[:, :N_EGROUPS].set(w_group).at[:, N_EGROUPS:N_EGROUPS + N_EXPERTS].set(w_expert)
    routed = jnp.concatenate([_moe_route(x, w_router) for x in xs_list], axis=0)
    t = routed.shape[0]
    eid = routed[:, 0:2].astype(jnp.int32).reshape(-1)
    ew = routed[:, 2:4].reshape(-1)
    n_slots = t * TOP_E
    np_rows = _moe_padded_rows(t)

    experts = jnp.arange(N_EXPERTS, dtype=jnp.int32)
    counts = jnp.sum((eid[:, None] == experts[None, :]).astype(jnp.int32), axis=0)
    order = jnp.argsort(eid, stable=True).astype(jnp.int32)
    rank = jnp.argsort(order).astype(jnp.int32)
    padded = ((counts + tm - 1) // tm) * tm
    pad_end = jnp.cumsum(padded)
    pad_off = pad_end - padded
    off = jnp.cumsum(counts) - counts
    pos = pad_off[eid] + rank - off[eid]
    tile_start = jnp.arange(np_rows // tm, dtype=jnp.int32) * tm
    tile_e = jnp.minimum(jnp.sum((pad_end[None, :] <= tile_start[:, None]).astype(jnp.int32), axis=1), N_EXPERTS - 1)
    nvalid = (pad_end[-1] // tm).astype(jnp.int32).reshape(1)
    row = jnp.arange(np_rows, dtype=jnp.int32)
    row_e = tile_e[row // tm]
    local = row - pad_off[row_e]
    row_ok = local < counts[row_e]
    slot = order[jnp.clip(off[row_e] + local, 0, n_slots - 1)]
    src_tok = jnp.where(row_ok, slot // TOP_E, 0)
    slot_w = jnp.where(row_ok, ew[slot], 0.0)

    x_all = jnp.concatenate(xs_bf16_list, axis=0)
    ys = _moe_experts(tile_e, nvalid, x_all[src_tok], slot_w[:, None], w_gate, w_up, w_down)
    pos2 = pos.reshape(t, TOP_E)
    out, start = [], 0
    for x in xs_list:
        sl = pos2[start:start + x.shape[0]]
        out.append((ys[sl[:, 0]], ys[sl[:, 1]]))
        start += x.shape[0]
    return out


INT_MIN = -(2 ** 31)
INT_MAX = 2 ** 31 - 1


def _f32_order_key(x):
    b = lax.bitcast_convert_type(x + 0.0, jnp.int32)
    return jnp.where(b >= 0, b, b ^ jnp.int32(INT_MAX))


def _lanes(x, w):
    return x if w == V7X_LANES else jnp.concatenate([x] * (w // V7X_LANES), axis=-1)


def _count_rows(key_ref, n_chunks, pred_fns):
    n_max, rows, w = key_ref.shape
    slabs = w // V7X_LANES
    assert slabs * n_max <= 256
    lane = lax.broadcasted_iota(jnp.int32, (rows, V7X_LANES), 1)

    def body(c, accs):
        out = list(accs)
        for s in range(slabs):
            blk = key_ref[c, :, s * V7X_LANES:(s + 1) * V7X_LANES]
            idx = c * w + s * V7X_LANES + lane
            for t, fn in enumerate(pred_fns):
                out[t] = out[t] + jnp.where(fn(blk, idx), 1, 0)
        return tuple(out)

    init = tuple(jnp.zeros((rows, V7X_LANES), jnp.int32) for _ in pred_fns)
    accs = lax.fori_loop(0, n_chunks, body, init)
    ones = jnp.ones((V7X_LANES, V7X_LANES), jnp.bfloat16)
    return [jnp.dot(a.astype(jnp.float32).astype(jnp.bfloat16), ones,
                    preferred_element_type=jnp.float32).astype(jnp.int32) for a in accs]


def _any_row(mask):
    return jnp.max(jnp.where(mask, 1.0, 0.0)) > 0.5


def _topk_threshold(key_ref, n_chunks, k, index_bits):
    _, rows, w = key_ref.shape
    col = lambda v: jnp.full((rows, V7X_LANES), v, jnp.int32)

    def v_step(st):
        lo, hi, exact = st
        mid = (lo & hi) + ((lo ^ hi) >> 1)
        cnt, = _count_rows(key_ref, n_chunks, [lambda blk, idx: blk > mid])
        active = lo < hi
        hit = active & (cnt == k)
        less = active & (cnt < k)
        more = active & (cnt > k)
        hi = jnp.where(hit | less, mid, hi)
        lo = jnp.where(hit, mid, jnp.where(more, mid + 1, lo))
        exact = jnp.where(hit, 1, exact)
        return lo, hi, exact

    def v_body(st):
        lo, hi, exact = v_step(v_step(st[:3]))
        return lo, hi, exact, _any_row(lo < hi)

    thr, _, exact, _ = lax.while_loop(lambda st: st[3], v_body,
                                      (col(INT_MIN), col(INT_MAX), col(0), jnp.bool_(True)))
    c_gt, c_eq = _count_rows(key_ref, n_chunks, [lambda blk, idx: blk > thr, lambda blk, idx: blk == thr])
    r = k - c_gt
    take_all_ties = (exact == 0) & (thr != INT_MIN)
    need = take_all_ties & (c_eq > r)
    cut_default = jnp.where(take_all_ties, INT_MAX, -1)

    def tie_phase():
        def body(_, st):
            lo_i, hi_i = st
            mid = (lo_i + hi_i) >> 1
            g, = _count_rows(key_ref, n_chunks, [lambda blk, idx: (blk == thr) & (idx <= mid)])
            ok = g >= r
            return jnp.where(ok, lo_i, mid + 1), jnp.where(ok, mid, hi_i)
        lo_i, _ = lax.fori_loop(0, index_bits, body, (col(0), col(0) + (n_chunks * w - 1)))
        return jnp.where(need, lo_i, cut_default)

    cut = lax.cond(_any_row(need), tie_phase, lambda: cut_default)
    return thr, cut


def _selected(key, idx, thr, cut):
    w = key.shape[-1]
    thr, cut = _lanes(thr, w), _lanes(cut, w)
    return (key > thr) | ((key == thr) & (idx <= cut))


def _count_cols(key_ref, n_chunks, pred_fns):
    _, w, cols = key_ref.shape
    acc_rows = min(w, COUNT_ACC_ROWS)

    def body(c, accs):
        blk = key_ref[c]
        idx = c * w + lax.broadcasted_iota(jnp.int32, (w, cols), 0)
        return tuple(acc + jnp.sum(jnp.where(fn(blk, idx), 1, 0).reshape(w // acc_rows, acc_rows, cols), axis=0)
                     for fn, acc in zip(pred_fns, accs))

    init = tuple(jnp.zeros((acc_rows, cols), jnp.int32) for _ in pred_fns)
    accs = lax.fori_loop(0, n_chunks, body, init)
    return [jnp.sum(a, axis=0, keepdims=True) for a in accs]


TOPK_PASSES_PER_TRIP = 4
COUNT_ACC_ROWS = 64


def _topk_threshold_cols(key_ref, n_chunks, k, index_bits):
    _, w, cols = key_ref.shape
    row = lambda v: jnp.full((1, cols), v, jnp.int32)

    def v_step(st):
        lo, hi, exact = st
        mid = (lo & hi) + ((lo ^ hi) >> 1)
        cnt, = _count_cols(key_ref, n_chunks, [lambda blk, idx: blk > mid])
        active = lo < hi
        hit = active & (cnt == k)
        less = active & (cnt < k)
        more = active & (cnt > k)
        hi = jnp.where(hit | less, mid, hi)
        lo = jnp.where(hit, mid, jnp.where(more, mid + 1, lo))
        exact = jnp.where(hit, 1, exact)
        return lo, hi, exact

    def v_body(st):
        inner = st[:3]
        for _ in range(TOPK_PASSES_PER_TRIP):
            inner = v_step(inner)
        return inner + (_any_row(inner[0] < inner[1]),)

    thr, _, exact, _ = lax.while_loop(lambda st: st[3], v_body,
                                      (row(INT_MIN), row(INT_MAX), row(0), jnp.bool_(True)))
    c_gt, c_eq = _count_cols(key_ref, n_chunks, [lambda blk, idx: blk > thr, lambda blk, idx: blk == thr])
    r = k - c_gt
    take_all_ties = (exact == 0) & (thr != INT_MIN)
    need = take_all_ties & (c_eq > r)
    cut_default = jnp.where(take_all_ties, INT_MAX, -1)

    def tie_phase():
        def body(_, st):
            lo_i, hi_i = st
            mid = (lo_i + hi_i) >> 1
            g, = _count_cols(key_ref, n_chunks, [lambda blk, idx: (blk == thr) & (idx <= mid)])
            ok = g >= r
            return jnp.where(ok, lo_i, mid + 1), jnp.where(ok, mid, hi_i)
        lo_i, _ = lax.fori_loop(0, index_bits, body, (row(0), row(0) + (n_chunks * w - 1)))
        return jnp.where(need, lo_i, cut_default)

    cut = lax.cond(_any_row(need), tie_phase, lambda: cut_default)
    return thr, cut


def _selected_cols(key, idx, thr, cut):
    return (key > thr) | ((key == thr) & (idx <= cut))


DSA_CHUNK = 1024


def _split3_lhs(x):
    hi = x.astype(jnp.bfloat16)
    lo = (x - hi.astype(jnp.float32)).astype(jnp.bfloat16)
    return jnp.concatenate([hi, hi, lo], axis=-1)


def _split3_rhs(x):
    hi = x.astype(jnp.bfloat16)
    lo = (x - hi.astype(jnp.float32)).astype(jnp.bfloat16)
    return jnp.concatenate([hi, lo, hi], axis=-1)


_NT = (((1,), (1,)), ((), ()))


def _with_ones_column(v):
    dh = v.shape[-1]
    width = (dh // V7X_LANES + 1) * V7X_LANES
    ones = jnp.ones(v.shape[:-1] + (1,), v.dtype)
    return jnp.concatenate([v, ones, jnp.zeros(v.shape[:-1] + (width - dh - 1,), v.dtype)], axis=-1)


def _dsa_prompt_kernel(qi_ref, wi_ref, kcat_ref, q_ref, k_ref, v_ref, o_ref, key_s, *, topk):
    cw = DSA_CHUNK
    rep = B_HEADS // B_KV_HEADS
    j = pl.program_id(1)
    t0 = j * Q_BLOCK
    n_chunks = (t0 + Q_BLOCK + cw - 1) // cw
    qpos = t0 + lax.broadcasted_iota(jnp.int32, (Q_BLOCK, 1), 0)

    qi = qi_ref[0]
    wi_t = wi_ref[0] * (IDX_DH ** -0.5)
    qcat = jnp.concatenate([_split3_lhs(qi[:, h * IDX_DH:(h + 1) * IDX_DH]) for h in range(IDX_HEADS)], axis=0)
    qpos_t = t0 + lax.broadcasted_iota(jnp.int32, (1, Q_BLOCK), 1)

    def score_chunk(c, carry):
        start = pl.multiple_of(c * cw, cw)
        kc = kcat_ref[0, pl.ds(start, cw), :]
        s = lax.dot_general(kc, qcat, _NT, preferred_element_type=jnp.float32)
        acc = None
        for h in range(IDX_HEADS):
            term = wi_t[h:h + 1, :] * jnp.maximum(s[:, h * Q_BLOCK:(h + 1) * Q_BLOCK], 0.0)
            acc = term if acc is None else acc + term
        kpos = start + lax.broadcasted_iota(jnp.int32, (cw, Q_BLOCK), 0)
        key_s[c] = jnp.where(kpos <= qpos_t, _f32_order_key(acc), INT_MIN)
        return carry

    lax.fori_loop(0, n_chunks, score_chunk, 0)
    thr, cut = _topk_threshold_cols(key_s, n_chunks, topk, index_bits=int(math.log2(key_s.shape[0] * cw)))

    q = q_ref[0]
    qg = [jnp.concatenate([q[:, (g * rep + r) * B_DH:(g * rep + r + 1) * B_DH] for r in range(rep)],
                          axis=0).astype(jnp.bfloat16) for g in range(B_KV_HEADS)]

    def att_chunk(c, carry):
        start = pl.multiple_of(c * cw, cw)
        kidx = start + lax.broadcasted_iota(jnp.int32, (cw, Q_BLOCK), 0)
        bias = jnp.transpose(jnp.where(_selected_cols(key_s[c], kidx, thr, cut), 0.0, NEG))[None]
        out = []
        for g in range(B_KV_HEADS):
            m, acc = carry[g]
            kg = k_ref[0, g, pl.ds(start, cw), :]
            vg = v_ref[0, g, pl.ds(start, cw), :]
            s = lax.dot_general(qg[g], kg, _NT, preferred_element_type=jnp.float32) * (B_DH ** -0.5 * LOG2E)
            s = s.reshape(rep, Q_BLOCK, cw) + bias
            m_new = jnp.maximum(m, jnp.max(s, axis=-1, keepdims=True))
            p = jnp.exp2((s - m_new).astype(jnp.bfloat16))
            pv = jnp.dot(p.reshape(rep * Q_BLOCK, cw), vg, preferred_element_type=jnp.float32)
            acc = jnp.exp2(m - m_new) * acc + pv.reshape(rep, Q_BLOCK, vg.shape[-1])
            out.append((m_new, acc))
        return tuple(out)

    vw = v_ref.shape[-1]
    init = tuple((jnp.full((rep, Q_BLOCK, 1), NEG, jnp.float32),
                  jnp.zeros((rep, Q_BLOCK, vw), jnp.float32)) for _ in range(B_KV_HEADS))
    res = lax.fori_loop(0, n_chunks, att_chunk, init)
    heads = []
    for g in range(B_KV_HEADS):
        acc = res[g][1]
        l = acc[:, :, B_DH:B_DH + 1]
        o = jnp.where(l > 0.0, acc[:, :, :B_DH] / jnp.where(l > 0.0, l, 1.0), 0.0)
        heads += [o[r] for r in range(rep)]
    o_ref[0] = jnp.concatenate(heads, axis=-1)


def dsa_prompt(q, kv, qi, ki, wi):
    bn, s, _ = q.shape
    topk = min(DSA_TOPK, s // 4)
    cw = DSA_CHUNK
    assert s % cw == 0 and s % Q_BLOCK == 0
    kcat = _split3_rhs(ki)
    k = jnp.transpose(kv[:, :, :, 0, :], (0, 2, 1, 3)).astype(jnp.bfloat16)
    v = _with_ones_column(jnp.transpose(kv[:, :, :, 1, :], (0, 2, 1, 3)).astype(jnp.bfloat16))
    qw = B_HEADS * B_DH
    return pl.pallas_call(
        functools.partial(_dsa_prompt_kernel, topk=topk),
        grid=(bn, s // Q_BLOCK),
        in_specs=[
            pl.BlockSpec((1, Q_BLOCK, IDX_HEADS * IDX_DH), lambda b, j: (b, j, 0)),
            pl.BlockSpec((1, IDX_HEADS, Q_BLOCK), lambda b, j: (b, 0, j)),
            pl.BlockSpec((1, s, 3 * IDX_DH), lambda b, j: (b, 0, 0)),
            pl.BlockSpec((1, Q_BLOCK, qw), lambda b, j: (b, j, 0)),
            pl.BlockSpec((1, B_KV_HEADS, s, B_DH), lambda b, j: (b, 0, 0, 0)),
            pl.BlockSpec((1, B_KV_HEADS, s, v.shape[-1]), lambda b, j: (b, 0, 0, 0)),
        ],
        out_specs=pl.BlockSpec((1, Q_BLOCK, qw), lambda b, j: (b, j, 0)),
        out_shape=jax.ShapeDtypeStruct((bn, s, qw), jnp.float32),
        scratch_shapes=[pltpu.VMEM((s // cw, cw, Q_BLOCK), jnp.int32)],
        compiler_params=pltpu.CompilerParams(dimension_semantics=("arbitrary", "arbitrary"),
                                             vmem_limit_bytes=VMEM_LIMIT_BYTES),
        name="dsa_prompt",
    )(qi, jnp.swapaxes(wi, 1, 2), kcat, q, k, v)


NSA_CHUNK = 1024
NSA_WIN_BLOCKS = WINDOW // Q_BLOCK + 1


def _nsa_prompt_kernel(qraw_ref, qrot_ref, gates_ref, kck_ref, kcv_ref, e_ref, ks_ref, vs_ref, *rest, n_sel):
    kw_refs = rest[0:NSA_WIN_BLOCKS]
    vw_refs = rest[NSA_WIN_BLOCKS:2 * NSA_WIN_BLOCKS]
    o_ref, key_s = rest[2 * NSA_WIN_BLOCKS:]
    cw = NSA_CHUNK
    rep = C_REP
    scale = C_DH ** -0.5
    j = pl.program_id(1)
    t0 = j * Q_BLOCK
    n_chunks = (t0 + Q_BLOCK + cw - 1) // cw
    qpos = t0 + lax.broadcasted_iota(jnp.int32, (Q_BLOCK, 1), 0)
    nbp = kck_ref.shape[2]
    blk = lax.broadcasted_iota(jnp.int32, (Q_BLOCK, nbp), 1)
    cmask = ((blk + 1) * CMP_BLOCK - 1) <= qpos
    cur = qpos // SEL_BLOCK
    forced = (blk == 0) | (blk == cur) | (blk == cur - 1)
    valid = blk * SEL_BLOCK <= qpos
    wlen = NSA_WIN_BLOCKS * Q_BLOCK
    wpos = t0 - WINDOW + lax.broadcasted_iota(jnp.int32, (Q_BLOCK, wlen), 1)
    wmask = ((wpos >= 0) & (wpos <= qpos) & (wpos > qpos - WINDOW))[None]

    qraw = qraw_ref[0]
    qrot = qrot_ref[0]
    sig = jax.nn.sigmoid(gates_ref[0])

    def stack(x, g):
        return jnp.concatenate([x[:, (g * rep + r) * C_DH:(g * rep + r + 1) * C_DH] for r in range(rep)], axis=0)

    o_cmps = []
    for g in range(C_KV):
        s = lax.dot_general(stack(qraw, g), kck_ref[0, g], _NT, precision=lax.Precision.HIGHEST,
                            preferred_element_type=jnp.float32) * scale
        s = jnp.where(cmask[None], s.reshape(rep, Q_BLOCK, nbp), NEG)
        e = jnp.exp(s - jnp.max(s, axis=-1, keepdims=True))
        p = e / jnp.sum(e, axis=-1, keepdims=True)
        p = p * jnp.where(jnp.max(jnp.where(cmask, 1.0, 0.0), axis=-1, keepdims=True) > 0.5, 1.0, 0.0)[None]
        o_cmps.append(jnp.dot(p.reshape(rep * Q_BLOCK, nbp), kcv_ref[0, g], precision=lax.Precision.HIGHEST,
                              preferred_element_type=jnp.float32).reshape(rep, Q_BLOCK, C_DH))
        imp = p[0]
        for r in range(1, rep):
            imp = imp + p[r]
        score = jnp.where(forced, FORCE, imp)
        key_s[0, :, g * Q_BLOCK:(g + 1) * Q_BLOCK] = jnp.transpose(
            jnp.where(valid, _f32_order_key(score), INT_MIN))

    thr, cut = _topk_threshold_cols(key_s, 1, n_sel, index_bits=int(math.log2(nbp)))
    blk_t = lax.broadcasted_iota(jnp.int32, (nbp, C_KV * Q_BLOCK), 0)
    sel_t = jnp.where(_selected_cols(key_s[0], blk_t, thr, cut), 1.0, 0.0)

    for g in range(C_KV):
        o_cmp = o_cmps[g]
        selblk = jnp.transpose(sel_t[:, g * Q_BLOCK:(g + 1) * Q_BLOCK]).astype(jnp.bfloat16)

        qg = stack(qrot, g).astype(jnp.bfloat16)

        def att_chunk(c, carry):
            m, acc = carry
            start = pl.multiple_of(c * cw, cw)
            kpos = start + lax.broadcasted_iota(jnp.int32, (Q_BLOCK, cw), 1)
            hit = jnp.dot(selblk, e_ref[c], preferred_element_type=jnp.float32)
            bias = jnp.where((hit > 0.5) & (kpos <= qpos), 0.0, NEG)[None]
            kg = ks_ref[0, g, pl.ds(start, cw), :]
            vg = vs_ref[0, g, pl.ds(start, cw), :]
            sc = lax.dot_general(qg, kg, _NT, preferred_element_type=jnp.float32) * (scale * LOG2E)
            sc = sc.reshape(rep, Q_BLOCK, cw) + bias
            m_new = jnp.maximum(m, jnp.max(sc, axis=-1, keepdims=True))
            pp = jnp.exp2((sc - m_new).astype(jnp.bfloat16))
            pv = jnp.dot(pp.reshape(rep * Q_BLOCK, cw), vg, preferred_element_type=jnp.float32)
            return m_new, jnp.exp2(m - m_new) * acc + pv.reshape(rep, Q_BLOCK, vg.shape[-1])

        init = (jnp.full((rep, Q_BLOCK, 1), NEG, jnp.float32),
                jnp.zeros((rep, Q_BLOCK, vs_ref.shape[-1]), jnp.float32))
        _, acc = lax.fori_loop(0, n_chunks, att_chunk, init)
        l = acc[:, :, C_DH:C_DH + 1]
        o_slc = jnp.where(l > 0.0, acc[:, :, :C_DH] / jnp.where(l > 0.0, l, 1.0), 0.0)

        kwin = jnp.concatenate([r_[0, g] for r_ in kw_refs], axis=0)
        vwin = jnp.concatenate([r_[0, g] for r_ in vw_refs], axis=0)
        sw = lax.dot_general(qg, kwin, _NT, preferred_element_type=jnp.float32) * scale
        sw = jnp.where(wmask, sw.reshape(rep, Q_BLOCK, wlen), NEG)
        ew = jnp.where(wmask, jnp.exp(sw - jnp.max(sw, axis=-1, keepdims=True)), 0.0)
        lw = jnp.sum(ew, axis=-1, keepdims=True)
        pw = ew / jnp.where(lw > 0.0, lw, 1.0)
        o_win = jnp.dot(pw.reshape(rep * Q_BLOCK, wlen).astype(jnp.bfloat16), vwin,
                        preferred_element_type=jnp.float32).reshape(rep, Q_BLOCK, C_DH)

        for r in range(rep):
            hh = g * rep + r
            o = (sig[:, 3 * hh:3 * hh + 1] * o_cmp[r] + sig[:, 3 * hh + 1:3 * hh + 2] * o_slc[r]
                 + sig[:, 3 * hh + 2:3 * hh + 3] * o_win[r])
            o_ref[0, :, hh * C_DH:(hh + 1) * C_DH] = o


def nsa_prompt(q_raw, q_rot, gates, kck, kcv, kv_s, kv_w):
    bn, s, _ = q_raw.shape
    cw = NSA_CHUNK
    assert s % cw == 0
    nb = kck.shape[2]
    nbs = -(-s // SEL_BLOCK)
    assert nb == nbs
    nbp = -(-nb // V7X_LANES) * V7X_LANES
    n_sel = min(N_SEL, nbs)
    kck = jnp.pad(kck, ((0, 0), (0, 0), (0, nbp - nb), (0, 0)))
    kcv = jnp.pad(kcv, ((0, 0), (0, 0), (0, nbp - nb), (0, 0)))
    split = lambda kv, c: jnp.transpose(kv[:, :, :, c, :], (0, 2, 1, 3)).astype(jnp.bfloat16)
    ks, vs, kw, vw = split(kv_s, 0), _with_ones_column(split(kv_s, 1)), split(kv_w, 0), split(kv_w, 1)
    expand = (jnp.arange(s, dtype=jnp.int32)[None, :] // SEL_BLOCK == jnp.arange(nbp, dtype=jnp.int32)[:, None])
    expand = jnp.transpose(expand.astype(jnp.bfloat16).reshape(nbp, s // cw, cw), (1, 0, 2))
    qd = C_HEADS * C_DH
    nq = s // Q_BLOCK
    wb = NSA_WIN_BLOCKS

    def win_spec(slot):
        return pl.BlockSpec((1, C_KV, Q_BLOCK, C_DH),
                            lambda b, j: (b, 0, jnp.maximum(j - (wb - 1) + slot, 0), 0))

    return pl.pallas_call(
        functools.partial(_nsa_prompt_kernel, n_sel=n_sel),
        grid=(bn, nq),
        in_specs=[
            pl.BlockSpec((1, Q_BLOCK, qd), lambda b, j: (b, j, 0)),
            pl.BlockSpec((1, Q_BLOCK, qd), lambda b, j: (b, j, 0)),
            pl.BlockSpec((1, Q_BLOCK, C_HEADS * 3), lambda b, j: (b, j, 0)),
            pl.BlockSpec((1, C_KV, nbp, C_DH), lambda b, j: (b, 0, 0, 0)),
            pl.BlockSpec((1, C_KV, nbp, C_DH), lambda b, j: (b, 0, 0, 0)),
            pl.BlockSpec((s // cw, nbp, cw), lambda b, j: (0, 0, 0)),
            pl.BlockSpec((1, C_KV, s, C_DH), lambda b, j: (b, 0, 0, 0)),
            pl.BlockSpec((1, C_KV, s, vs.shape[-1]), lambda b, j: (b, 0, 0, 0)),
        ] + [win_spec(i) for i in range(wb)] + [win_spec(i) for i in range(wb)],
        out_specs=pl.BlockSpec((1, Q_BLOCK, qd), lambda b, j: (b, j, 0)),
        out_shape=jax.ShapeDtypeStruct((bn, s, qd), jnp.float32),
        scratch_shapes=[pltpu.VMEM((1, nbp, C_KV * Q_BLOCK), jnp.int32)],
        compiler_params=pltpu.CompilerParams(dimension_semantics=("arbitrary", "arbitrary"),
                                             vmem_limit_bytes=VMEM_LIMIT_BYTES),
        name="nsa_prompt",
    )(q_raw, q_rot, gates, kck, kcv, expand, ks, vs, *([kw] * wb), *([vw] * wb))


S5_N = S5_GROUPS * S5_STATE
S5_TILES = 4
S5_TILE_IN = S5_WIDTH // S5_TILES
S5_TILE_N = S5_N // S5_TILES
S5_CHUNK = 256
S5_SUB = S5_CHUNK // V7X_SUBLANES


def _s5_input_map(u, bcat_ref, store):
    for i in range(S5_TILES):
        ucat = _split3_lhs(u[:, i * S5_TILE_IN:(i + 1) * S5_TILE_IN])
        r = jnp.dot(ucat, bcat_ref[i], preferred_element_type=jnp.float32)
        store(i, r[:, :S5_TILE_N], r[:, S5_TILE_N:])


def _s5_output_map(h_tile, u, ccat_ref, d_ref, wglu_ref):
    ys = []
    for i in range(S5_TILES):
        re, im = h_tile(i)
        hcat = jnp.concatenate([re, im], axis=-1).astype(jnp.bfloat16)
        ys.append(jnp.dot(hcat, ccat_ref[i], preferred_element_type=jnp.float32))
    y = jax.nn.gelu(jnp.concatenate(ys, axis=-1) + d_ref[...] * u)
    z = jnp.dot(y.astype(jnp.bfloat16), wglu_ref[...], preferred_element_type=jnp.float32)
    return y * jax.nn.sigmoid(z)


def _cmul_add(a_re, a_im, x_re, x_im, b_re, b_im):
    return a_re * x_re - a_im * x_im + b_re, a_re * x_im + a_im * x_re + b_im


def _s5_prompt_kernel(u_ref, h0_ref, bcat_ref, lam8_ref, lamm_ref, ccat_ref, d_ref, wglu_ref,
                      y_ref, hlast_ref, hs, ein_s, carry_s):
    n = S5_N
    m = S5_SUB
    c = pl.program_id(1)

    @pl.when(c == 0)
    def _():
        carry_s[...] = h0_ref[0]

    u = u_ref[0]

    def store_bu(i, re, im):
        hs[:, i * S5_TILE_N:(i + 1) * S5_TILE_N] = re
        hs[:, n + i * S5_TILE_N:n + (i + 1) * S5_TILE_N] = im

    _s5_input_map(u, bcat_ref, store_bu)

    def scan_body(k, h):
        row = pl.multiple_of(k * V7X_SUBLANES, V7X_SUBLANES)
        rows = pl.ds(row, V7X_SUBLANES)
        n_re, n_im = _cmul_add(lam8_ref[:, :n], lam8_ref[:, n:], h[0], h[1], hs[rows, :n], hs[rows, n:])
        hs[rows, :n] = n_re
        hs[rows, n:] = n_im
        return n_re, n_im

    zero = jnp.zeros((V7X_SUBLANES, n), jnp.float32)
    ends = lax.fori_loop(0, m, scan_body, (zero, zero))

    lm_re, lm_im = lamm_ref[:, :n], lamm_ref[:, n:]
    e_re, e_im = carry_s[:, :n], carry_s[:, n:]
    for s in range(V7X_SUBLANES):
        ein_s[s:s + 1, :n] = e_re
        ein_s[s:s + 1, n:] = e_im
        e_re, e_im = _cmul_add(lm_re, lm_im, e_re, e_im, ends[0][s:s + 1], ends[1][s:s + 1])
    carry_s[:, :n] = e_re
    carry_s[:, n:] = e_im
    hlast_ref[0] = carry_s[...]

    def fix_body(k, corr):
        row = pl.multiple_of(k * V7X_SUBLANES, V7X_SUBLANES)
        rows = pl.ds(row, V7X_SUBLANES)
        c_re, c_im = _cmul_add(lam8_ref[:, :n], lam8_ref[:, n:], corr[0], corr[1], 0.0, 0.0)
        hs[rows, :n] = hs[rows, :n] + c_re
        hs[rows, n:] = hs[rows, n:] + c_im
        return c_re, c_im

    lax.fori_loop(0, m, fix_body, (ein_s[:, :n], ein_s[:, n:]))

    def h_tile(i):
        return hs[:, i * S5_TILE_N:(i + 1) * S5_TILE_N], hs[:, n + i * S5_TILE_N:n + (i + 1) * S5_TILE_N]

    y_ref[0] = _s5_output_map(h_tile, u, ccat_ref, d_ref, wglu_ref)


def _s5_sample_kernel(u_ref, h0_ref, bcat_ref, lam_ref, ccat_ref, d_ref, wglu_ref, y_ref, hlast_ref, hs):
    n = S5_N
    steps, bd, _ = u_ref.shape
    h_re, h_im = h0_ref[:, :n], h0_ref[:, n:]
    lam_re, lam_im = lam_ref[:, :n], lam_ref[:, n:]
    for t in range(steps):
        u = u_ref[t]

        def store_bu(i, re, im):
            hs[:, i * S5_TILE_N:(i + 1) * S5_TILE_N] = re
            hs[:, n + i * S5_TILE_N:n + (i + 1) * S5_TILE_N] = im

        _s5_input_map(u, bcat_ref, store_bu)
        h_re, h_im = _cmul_add(lam_re, lam_im, h_re, h_im, hs[:, :n], hs[:, n:])
        hs[:, :n] = h_re
        hs[:, n:] = h_im

        def h_tile(i):
            return hs[:, i * S5_TILE_N:(i + 1) * S5_TILE_N], hs[:, n + i * S5_TILE_N:n + (i + 1) * S5_TILE_N]

        y_ref[t] = _s5_output_map(h_tile, u, ccat_ref, d_ref, wglu_ref)
    hlast_ref[:, :n] = h_re
    hlast_ref[:, n:] = h_im


def _s5_params(a_re, a_im, log_dt, b_re, b_im, c_re, c_im):
    lam = lax.complex(a_re, a_im)
    dt = jnp.exp(log_dt)[:, None]
    lam_bar = jnp.exp(lam * dt)
    b_bar = ((lam_bar - 1.0) / lam)[:, :, None] * lax.complex(b_re, b_im)
    gpt = S5_GROUPS // S5_TILES
    eye = jnp.eye(gpt, dtype=jnp.float32)

    def in_blockdiag(w):
        w = w.reshape(S5_TILES, gpt, S5_STATE, S5_GROUP)
        return jnp.einsum('tgpc,gh->tgchp', w, eye).reshape(S5_TILES, S5_TILE_IN, S5_TILE_N)

    def out_blockdiag(w):
        w = w.reshape(S5_TILES, gpt, S5_GROUP, S5_STATE)
        return jnp.einsum('tgcp,gh->tgphc', w, eye).reshape(S5_TILES, S5_TILE_N, S5_TILE_IN)

    b_full = jnp.concatenate([in_blockdiag(b_bar.real), in_blockdiag(b_bar.imag)], axis=-1)
    b_hi = b_full.astype(jnp.bfloat16)
    b_lo = (b_full - b_hi.astype(jnp.float32)).astype(jnp.bfloat16)
    bcat = jnp.concatenate([b_hi, b_lo, b_hi], axis=1)
    ccat = jnp.concatenate([out_blockdiag(c_re), out_blockdiag(-c_im)], axis=1).astype(jnp.bfloat16)
    flat = lambda z: jnp.concatenate([z.real.reshape(1, S5_N), z.imag.reshape(1, S5_N)], axis=-1)
    lam_row = flat(lam_bar)
    lamm_row = flat(jnp.exp(lam * dt * S5_SUB))
    return bcat, ccat, lam_row, lamm_row


def s5_prompt(u, a_re, a_im, log_dt, b_re, b_im, c_re, c_im, d, w_glu):
    bn, s, _ = u.shape
    tc, m = S5_CHUNK, S5_SUB
    assert s % tc == 0
    nc = s // tc
    bcat, ccat, lam_row, lamm_row = _s5_params(a_re, a_im, log_dt, b_re, b_im, c_re, c_im)
    lam8 = jnp.broadcast_to(lam_row, (V7X_SUBLANES, 2 * S5_N))
    to_ks = lambda a: a.reshape(bn, nc, V7X_SUBLANES, m, S5_WIDTH).swapaxes(2, 3).reshape(bn, s, S5_WIDTH)
    from_ks = lambda a: a.reshape(bn, nc, m, V7X_SUBLANES, S5_WIDTH).swapaxes(2, 3).reshape(bn, s, S5_WIDTH)
    h0 = jnp.zeros((bn, 1, 2 * S5_N), jnp.float32)
    const = lambda shape: pl.BlockSpec(shape, lambda b, c: (0,) * len(shape))
    y, hlast = pl.pallas_call(
        _s5_prompt_kernel,
        grid=(bn, nc),
        in_specs=[
            pl.BlockSpec((1, tc, S5_WIDTH), lambda b, c: (b, c, 0)),
            pl.BlockSpec((1, 1, 2 * S5_N), lambda b, c: (b, 0, 0)),
            const(bcat.shape), const(lam8.shape), const(lamm_row.shape), const(ccat.shape),
            const((1, S5_WIDTH)), const((S5_WIDTH, S5_WIDTH)),
        ],
        out_specs=[pl.BlockSpec((1, tc, S5_WIDTH), lambda b, c: (b, c, 0)),
                   pl.BlockSpec((1, 1, 2 * S5_N), lambda b, c: (b, 0, 0))],
        out_shape=[jax.ShapeDtypeStruct((bn, s, S5_WIDTH), jnp.float32),
                   jax.ShapeDtypeStruct((bn, 1, 2 * S5_N), jnp.float32)],
        scratch_shapes=[pltpu.VMEM((tc, 2 * S5_N), jnp.float32),
                        pltpu.VMEM((V7X_SUBLANES, 2 * S5_N), jnp.float32),
                        pltpu.VMEM((1, 2 * S5_N), jnp.float32)],
        compiler_params=pltpu.CompilerParams(dimension_semantics=("arbitrary", "arbitrary"),
                                             vmem_limit_bytes=VMEM_LIMIT_BYTES),
        name="s5_prompt",
    )(to_ks(u), h0, bcat, lam8, lamm_row, ccat, d.reshape(1, S5_WIDTH), w_glu.astype(jnp.bfloat16))
    return from_ks(y), hlast.reshape(bn, 2, S5_GROUPS, S5_STATE)


def s5_sample(u, state, a_re, a_im, log_dt, b_re, b_im, c_re, c_im, d, w_glu):
    bd, steps, _ = u.shape
    bcat, ccat, lam_row, _ = _s5_params(a_re, a_im, log_dt, b_re, b_im, c_re, c_im)
    y, hlast = pl.pallas_call(
        _s5_sample_kernel,
        out_shape=[jax.ShapeDtypeStruct((steps, bd, S5_WIDTH), jnp.float32),
                   jax.ShapeDtypeStruct((bd, 2 * S5_N), jnp.float32)],
        scratch_shapes=[pltpu.VMEM((bd, 2 * S5_N), jnp.float32)],
        compiler_params=pltpu.CompilerParams(vmem_limit_bytes=VMEM_LIMIT_BYTES),
        name="s5_sample",
    )(jnp.swapaxes(u, 0, 1), state.reshape(bd, 2 * S5_N), bcat, lam_row, ccat,
      d.reshape(1, S5_WIDTH), w_glu.astype(jnp.bfloat16))
    return jnp.swapaxes(y, 0, 1), hlast.reshape(bd, 2, S5_GROUPS, S5_STATE)


SAMPLE_PAGES_PER_STEP = 32


def _dsa_sample_kernel(pt_ref, qi_ref, wi_ref, q_ref, kinew_ref, kvnew_ref, *rest, topk, past_len):
    pg = SAMPLE_PAGES_PER_STEP
    idx_refs = rest[0:pg]
    kv_refs = rest[pg:2 * pg]
    o_ref, key_s, kv_all = rest[2 * pg:]
    cw = pg * PAGE_SIZE
    n_steps = past_len // cw
    steps_q = q_ref.shape[1]
    rows = key_s.shape[1]
    rep = B_HEADS // B_KV_HEADS
    kvw = B_KV_HEADS * 2 * B_DH
    s_id = pl.program_id(1)
    qpos = past_len + lax.broadcasted_iota(jnp.int32, (rows, 1), 0)

    qi = qi_ref[0]
    wi = wi_ref[0]
    qcat = jnp.concatenate([_split3_lhs(qi[:, h * IDX_DH:(h + 1) * IDX_DH]) for h in range(IDX_HEADS)], axis=0)
    row_ok = lax.broadcasted_iota(jnp.int32, (rows, 1), 0) < steps_q

    def scores(kidx_rows, kpos):
        s = lax.dot_general(qcat, _split3_rhs(kidx_rows), _NT, preferred_element_type=jnp.float32) * (IDX_DH ** -0.5)
        acc = None
        for h in range(IDX_HEADS):
            term = wi[:, h:h + 1] * jnp.maximum(s[h * steps_q:(h + 1) * steps_q], 0.0)
            acc = term if acc is None else acc + term
        acc = jnp.concatenate([acc, jnp.zeros((rows - steps_q, acc.shape[1]), jnp.float32)], axis=0)
        return jnp.where((kpos <= qpos) & row_ok, _f32_order_key(acc), INT_MIN)

    kpos = s_id * cw + lax.broadcasted_iota(jnp.int32, (rows, cw), 1)
    key_s[s_id] = scores(jnp.concatenate([r[0] for r in idx_refs], axis=0), kpos)
    for i in range(pg):
        start = pl.multiple_of(s_id * cw + i * PAGE_SIZE, PAGE_SIZE)
        kv_all[pl.ds(start, PAGE_SIZE), :] = kv_refs[i][0].astype(jnp.bfloat16)

    @pl.when(s_id == n_steps - 1)
    def _():
        pad = jnp.zeros((cw - steps_q, IDX_DH), jnp.float32)
        kpos_new = past_len + lax.broadcasted_iota(jnp.int32, (rows, cw), 1)
        new_key = scores(jnp.concatenate([kinew_ref[0], pad], axis=0), kpos_new)
        lane = lax.broadcasted_iota(jnp.int32, (rows, cw), 1)
        key_s[n_steps] = jnp.where(lane < steps_q, new_key, INT_MIN)
        kv_all[pl.ds(past_len, cw), :] = jnp.concatenate(
            [kvnew_ref[0], jnp.zeros((cw - steps_q, kvw), jnp.float32)], axis=0).astype(jnp.bfloat16)
        n_chunks = n_steps + 1
        thr, cut = _topk_threshold(key_s, n_chunks, topk, index_bits=int(math.ceil(math.log2(n_chunks * cw))))

        q = q_ref[0]
        qrows = []
        for h in range(B_HEADS):
            g = h // rep
            qh = q[:, h * B_DH:(h + 1) * B_DH]
            pieces = [jnp.zeros((steps_q, g * 2 * B_DH), jnp.float32)] if g else []
            pieces += [qh, jnp.zeros((steps_q, kvw - g * 2 * B_DH - B_DH), jnp.float32)]
            qrows.append(jnp.concatenate(pieces, axis=-1))
        qx = jnp.concatenate(qrows, axis=0).astype(jnp.bfloat16)
        nr = B_HEADS * steps_q

        def att_chunk(c, carry):
            m, l, acc = carry
            start = pl.multiple_of(c * cw, cw)
            kidx = start + lax.broadcasted_iota(jnp.int32, (rows, cw), 1)
            sel = _selected(key_s[c], kidx, thr, cut)[0:steps_q][None]
            kvc = kv_all[pl.ds(start, cw), :]
            s = lax.dot_general(qx, kvc, _NT, preferred_element_type=jnp.float32) * (B_DH ** -0.5)
            s = jnp.where(sel, s.reshape(B_HEADS, steps_q, cw), NEG)
            m_new = jnp.maximum(m, jnp.max(s, axis=-1, keepdims=True))
            p = jnp.where(sel, jnp.exp(s - m_new), 0.0)
            alpha = jnp.exp(m - m_new)
            l = alpha * l + jnp.sum(p, axis=-1, keepdims=True)
            pv = jnp.dot(p.reshape(nr, cw).astype(jnp.bfloat16), kvc, preferred_element_type=jnp.float32)
            return m_new, l, alpha * acc + pv.reshape(B_HEADS, steps_q, kvw)

        init = (jnp.full((B_HEADS, steps_q, 1), NEG, jnp.float32), jnp.zeros((B_HEADS, steps_q, 1), jnp.float32),
                jnp.zeros((B_HEADS, steps_q, kvw), jnp.float32))
        _, l, acc = lax.fori_loop(0, n_chunks, att_chunk, init)
        o = jnp.where(l > 0.0, acc / jnp.where(l > 0.0, l, 1.0), 0.0)
        heads = []
        for h in range(B_HEADS):
            g = h // rep
            heads.append(o[h][:, g * 2 * B_DH + B_DH:(g + 1) * 2 * B_DH])
        o_ref[0] = jnp.concatenate(heads, axis=-1)


def dsa_sample(q, kv, qi, ki, wi, cache_kv, cache_idx, page_table):
    bd, steps_q, _ = q.shape
    n_pages = page_table.shape[1]
    past_len = n_pages * PAGE_SIZE
    pg = SAMPLE_PAGES_PER_STEP
    assert n_pages % pg == 0
    n_steps = n_pages // pg
    cw = pg * PAGE_SIZE
    topk = min(DSA_TOPK, (past_len + steps_q) // 4)
    kvw = B_KV_HEADS * 2 * B_DH
    n_pool = cache_kv.shape[0]
    rows = V7X_SUBLANES
    assert steps_q <= rows

    def page_spec(i, width):
        return pl.BlockSpec((1, PAGE_SIZE, width), lambda b, s, pt: (pt[b * n_pages + s * pg + i], 0, 0))

    per_b = lambda shape: pl.BlockSpec((1,) + shape, lambda b, s, pt: (b, 0, 0))
    grid_spec = pltpu.PrefetchScalarGridSpec(
        num_scalar_prefetch=1,
        grid=(bd, n_steps),
        in_specs=[per_b((steps_q, IDX_HEADS * IDX_DH)), per_b((steps_q, IDX_HEADS)), per_b((steps_q, B_HEADS * B_DH)),
                  per_b((steps_q, IDX_DH)), per_b((steps_q, kvw))]
        + [page_spec(i, IDX_DH) for i in range(pg)] + [page_spec(i, kvw) for i in range(pg)],
        out_specs=per_b((steps_q, B_HEADS * B_DH)),
        scratch_shapes=[pltpu.VMEM((n_steps + 1, rows, cw), jnp.int32),
                        pltpu.VMEM((past_len + cw, kvw), jnp.bfloat16)],
    )
    return pl.pallas_call(
        functools.partial(_dsa_sample_kernel, topk=topk, past_len=past_len),
        grid_spec=grid_spec,
        out_shape=jax.ShapeDtypeStruct((bd, steps_q, B_HEADS * B_DH), jnp.float32),
        compiler_params=pltpu.CompilerParams(dimension_semantics=("arbitrary", "arbitrary"),
                                             vmem_limit_bytes=VMEM_LIMIT_BYTES),
        name="dsa_sample",
    )(page_table.reshape(-1), qi, wi, q, ki, kv.reshape(bd, steps_q, kvw),
      *([cache_idx] * pg), *([cache_kv.reshape(n_pool, PAGE_SIZE, kvw)] * pg))


CMP_PAGES_PER_STEP = 32


def _nsa_compress_kernel(pt_ref, cmp_pos_ref, w1_ref, w2_ref, *rest):
    pg = CMP_PAGES_PER_STEP
    page_refs = rest[:pg]
    kck_ref, kcv_ref, xs = rest[pg:]
    bpp = PAGE_SIZE // CMP_BLOCK
    nblk = pg * bpp
    slabs = C_KV * 2
    prow = PAGE_SIZE * slabs
    for i in range(pg):
        xs[i * prow:(i + 1) * prow, :] = page_refs[i][0]
    for c, out_ref in enumerate((kck_ref, kcv_ref)):
        acc = jnp.zeros((C_KV * nblk, C_DH), jnp.float32)
        for pp in range(CMP_BLOCK // 2):
            cols = []
            for pos in (2 * pp, 2 * pp + 1):
                bias = cmp_pos_ref[pos:pos + 1, c * C_DH:(c + 1) * C_DH]
                cols.append(jnp.concatenate(
                    [xs[pl.ds(pos * slabs + g * 2 + c, nblk, stride=CMP_BLOCK * slabs), :] + bias
                     for g in range(C_KV)], axis=0))
            lhs = jnp.concatenate(cols, axis=-1).astype(jnp.bfloat16)
            acc = acc + jnp.dot(lhs, w1_ref[c, pp], preferred_element_type=jnp.float32)
        kc = jnp.dot(jax.nn.gelu(acc).astype(jnp.bfloat16), w2_ref[c], preferred_element_type=jnp.float32)
        for g in range(C_KV):
            out_ref[0, g] = kc[g * nblk:(g + 1) * nblk]


def nsa_compress(pages, page_table, cmp_pos, cmp_w1, cmp_w2):
    bn, n_pages = page_table.shape
    pg = CMP_PAGES_PER_STEP
    assert n_pages % pg == 0
    n_pool = pages.shape[0]
    prow = PAGE_SIZE * C_KV * 2
    bpp = PAGE_SIZE // CMP_BLOCK
    nb = n_pages * bpp
    w1 = cmp_w1.reshape(2, CMP_BLOCK // 2, 2 * C_DH, C_DH).astype(jnp.bfloat16)
    w2 = cmp_w2.astype(jnp.bfloat16)

    def page_spec(i):
        return pl.BlockSpec((1, prow, C_DH), lambda b, s, pt: (pt[b * n_pages + s * pg + i], 0, 0))

    const = lambda shape: pl.BlockSpec(shape, lambda b, s, pt: (0,) * len(shape))
    out_spec = pl.BlockSpec((1, C_KV, pg * bpp, C_DH), lambda b, s, pt: (b, 0, s, 0))
    grid_spec = pltpu.PrefetchScalarGridSpec(
        num_scalar_prefetch=1,
        grid=(bn, n_pages // pg),
        in_specs=[const((CMP_BLOCK, 2 * C_DH)), const(w1.shape), const(w2.shape)] + [page_spec(i) for i in range(pg)],
        out_specs=[out_spec, out_spec],
        scratch_shapes=[pltpu.VMEM((pg * prow, C_DH), jnp.float32)],
    )
    shape = jax.ShapeDtypeStruct((bn, C_KV, nb, C_DH), jnp.float32)
    return pl.pallas_call(
        _nsa_compress_kernel,
        grid_spec=grid_spec,
        out_shape=[shape, shape],
        compiler_params=pltpu.CompilerParams(dimension_semantics=("arbitrary", "arbitrary"),
                                             vmem_limit_bytes=VMEM_LIMIT_BYTES),
        name="nsa_compress",
    )(page_table.reshape(-1), cmp_pos.reshape(CMP_BLOCK, 2 * C_DH), w1, w2,
      *([pages.reshape(n_pool, prow, C_DH)] * pg))


def _nsa_sample_attend_kernel(qraw_ref, qrot_ref, kck_ref, kcv_ref, win_ref, kvw_ref, tri_ref,
                              ocmp_ref, owin_ref, sel_ref, key_s, *, past_len, n_sel):
    rep = C_REP
    scale = C_DH ** -0.5
    steps_q = qraw_ref.shape[1]
    rows = V7X_SUBLANES
    nbp = key_s.shape[1]
    nb = kck_ref.shape[2]
    wbuf = win_ref.shape[1] // (C_KV * 2)
    qpos = past_len + lax.broadcasted_iota(jnp.int32, (steps_q, 1), 0)
    blk_c = lax.broadcasted_iota(jnp.int32, (steps_q, nb), 1)
    cmask = ((blk_c + 1) * CMP_BLOCK - 1) <= qpos
    any_c = jnp.where(jnp.max(jnp.where(cmask, 1.0, 0.0), axis=-1, keepdims=True) > 0.5, 1.0, 0.0)
    blk = lax.broadcasted_iota(jnp.int32, (steps_q, nbp), 1)
    cur = qpos // SEL_BLOCK
    forced = (blk == 0) | (blk == cur) | (blk == cur - 1)
    nbs = -(-(past_len + steps_q) // SEL_BLOCK)
    valid = (blk * SEL_BLOCK <= qpos) & (blk < nbs)
    lane = lax.broadcasted_iota(jnp.int32, (rows, nbp), 1)
    wlen = wbuf + V7X_SUBLANES
    wcol = lax.broadcasted_iota(jnp.int32, (steps_q, wlen), 1)
    wpos = jnp.where(wcol < wbuf, past_len - wbuf + wcol, past_len + wcol - wbuf)
    wmask = ((wpos >= 0) & (wpos <= qpos) & (wpos > qpos - WINDOW) & (wcol < wbuf + steps_q))[None]
    qraw = qraw_ref[0]
    qrot = qrot_ref[0]

    def stack(x, g):
        return jnp.concatenate([x[:, (g * rep + r) * C_DH:(g * rep + r + 1) * C_DH] for r in range(rep)], axis=0)

    pad_rows = lambda x: jnp.concatenate([x, jnp.zeros((rows - steps_q,) + x.shape[1:], x.dtype)], axis=0)

    o_cmps, key_rows = [], []
    for g in range(C_KV):
        s = lax.dot_general(stack(qraw, g), kck_ref[0, g], _NT, precision=lax.Precision.HIGHEST,
                            preferred_element_type=jnp.float32) * scale
        s = jnp.where(cmask[None], s.reshape(rep, steps_q, nb), NEG)
        e = jnp.exp(s - jnp.max(s, axis=-1, keepdims=True))
        p = e / jnp.sum(e, axis=-1, keepdims=True) * any_c[None]
        o_cmps.append(jnp.dot(p.reshape(rep * steps_q, nb), kcv_ref[0, g], precision=lax.Precision.HIGHEST,
                              preferred_element_type=jnp.float32))
        imp = p[0]
        for r in range(1, rep):
            imp = imp + p[r]
        imp = jnp.concatenate([imp, jnp.zeros((steps_q, nbp - nb), jnp.float32)], axis=-1)
        score = jnp.where(forced, FORCE, imp)
        key_rows.append(pad_rows(jnp.where(valid, _f32_order_key(score), INT_MIN)))

    n_prob = C_KV * rows
    keys = jnp.concatenate(key_rows + [jnp.full((V7X_LANES - n_prob, nbp), INT_MIN, jnp.int32)], axis=0)
    key_s[0] = jnp.transpose(keys)
    thr, cut = _topk_threshold_cols(key_s, 1, n_sel, index_bits=int(math.ceil(math.log2(nbp))))
    blk_t = lax.broadcasted_iota(jnp.int32, (nbp, V7X_LANES), 0)
    sel_rows = jnp.transpose(jnp.where(_selected_cols(key_s[0], blk_t, thr, cut), 1.0, 0.0))

    for g in range(C_KV):
        o_cmp = o_cmps[g]
        mask = sel_rows[g * rows:(g + 1) * rows] > 0.5
        prefix = jnp.dot(jnp.where(mask, 1.0, 0.0).astype(jnp.bfloat16), tri_ref[...],
                         preferred_element_type=jnp.float32)
        ids = [jnp.sum(jnp.where(mask & (prefix == float(t + 1)), lane, 0), axis=-1, keepdims=True)
               for t in range(n_sel)]
        ids.append(jnp.zeros((rows, V7X_LANES - n_sel), jnp.int32))
        sel_ref[0, g] = jnp.concatenate(ids, axis=-1)

        qg = stack(qrot, g).astype(jnp.bfloat16)
        newkv = kvw_ref[0]
        padn = jnp.zeros((V7X_SUBLANES - steps_q, C_DH), jnp.float32)
        slabs = C_KV * 2
        kwin = jnp.concatenate([win_ref.at[0][pl.ds(2 * g, wbuf, stride=slabs), :],
                                newkv[:, (2 * g) * C_DH:(2 * g + 1) * C_DH], padn], axis=0).astype(jnp.bfloat16)
        vwin = jnp.concatenate([win_ref.at[0][pl.ds(2 * g + 1, wbuf, stride=slabs), :],
                                newkv[:, (2 * g + 1) * C_DH:(2 * g + 2) * C_DH], padn], axis=0).astype(jnp.bfloat16)
        sw = lax.dot_general(qg, kwin, _NT, preferred_element_type=jnp.float32) * scale
        sw = jnp.where(wmask, sw.reshape(rep, steps_q, wlen), NEG)
        ew = jnp.where(wmask, jnp.exp(sw - jnp.max(sw, axis=-1, keepdims=True)), 0.0)
        lw = jnp.sum(ew, axis=-1, keepdims=True)
        pw = ew / jnp.where(lw > 0.0, lw, 1.0)
        o_win = jnp.dot(pw.reshape(rep * steps_q, wlen).astype(jnp.bfloat16), vwin, preferred_element_type=jnp.float32)
        for r in range(rep):
            hh = g * rep + r
            ocmp_ref[0, :, hh * C_DH:(hh + 1) * C_DH] = o_cmp[r * steps_q:(r + 1) * steps_q]
            owin_ref[0, :, hh * C_DH:(hh + 1) * C_DH] = o_win[r * steps_q:(r + 1) * steps_q]


def nsa_sample_attend(q_raw, q_rot, kck, kcv, win, kv_w):
    bd, steps_q, qd = q_raw.shape
    nb = kck.shape[2]
    past_len = nb * CMP_BLOCK
    nbs = -(-(past_len + steps_q) // SEL_BLOCK)
    assert nbs >= N_SEL and steps_q <= V7X_SUBLANES
    nbp = -(-nbs // V7X_LANES) * V7X_LANES
    wbuf = win.shape[1]
    roww = C_KV * 2 * C_DH
    tri = (jnp.arange(nbp)[:, None] <= jnp.arange(nbp)[None, :]).astype(jnp.bfloat16)
    per_b = lambda shape: pl.BlockSpec((1,) + shape, lambda b: (b,) + (0,) * len(shape))
    return pl.pallas_call(
        functools.partial(_nsa_sample_attend_kernel, past_len=past_len, n_sel=N_SEL),
        grid=(bd,),
        in_specs=[per_b((steps_q, qd)), per_b((steps_q, qd)), per_b((C_KV, nb, C_DH)), per_b((C_KV, nb, C_DH)),
                  per_b((wbuf * C_KV * 2, C_DH)), per_b((steps_q, roww)), pl.BlockSpec((nbp, nbp), lambda b: (0, 0))],
        out_specs=[per_b((steps_q, qd)), per_b((steps_q, qd)), per_b((C_KV, V7X_SUBLANES, V7X_LANES))],
        out_shape=[jax.ShapeDtypeStruct((bd, steps_q, qd), jnp.float32),
                   jax.ShapeDtypeStruct((bd, steps_q, qd), jnp.float32),
                   jax.ShapeDtypeStruct((bd, C_KV, V7X_SUBLANES, V7X_LANES), jnp.int32)],
        scratch_shapes=[pltpu.VMEM((1, nbp, V7X_LANES), jnp.int32)],
        compiler_params=pltpu.CompilerParams(dimension_semantics=("arbitrary",), vmem_limit_bytes=VMEM_LIMIT_BYTES),
        name="nsa_sample_attend",
    )(q_raw, q_rot, kck, kcv, win.reshape(bd, wbuf * C_KV * 2, C_DH), kv_w.reshape(bd, steps_q, roww), tri)


def _nsa_sample_selected_kernel(phys_ref, blk_ref, q_ref, gates_ref, ocmp_ref, owin_ref, new_ref, *rest,
                                past_len, steps_q):
    blk_refs = rest[:C_KV * N_SEL]
    o_ref = rest[C_KV * N_SEL]
    i = pl.program_id(0)
    qpos = past_len + i % steps_q
    scale = C_DH ** -0.5
    slabs = C_KV * 2
    row = lax.broadcasted_iota(jnp.int32, (1, SEL_BLOCK), 1)
    ncol = lax.broadcasted_iota(jnp.int32, (1, V7X_SUBLANES), 1)
    nb_past = past_len // SEL_BLOCK
    padn = jnp.zeros((V7X_SUBLANES - steps_q, C_DH), jnp.float32)
    sig = jax.nn.sigmoid(gates_ref[0])
    newkv = new_ref[0]
    for g in range(C_KV):
        q = q_ref[0, g * C_REP:(g + 1) * C_REP, :].astype(jnp.bfloat16)
        ks, vs, kpos = [], [], []
        has_new = jnp.int32(0)
        for t in range(N_SEL):
            b_id = blk_ref[(i * C_KV + g) * N_SEL + t]
            is_new = b_id >= nb_past
            has_new = jnp.maximum(has_new, is_new.astype(jnp.int32))
            ref = blk_refs[g * N_SEL + t].at[0]
            ks.append(ref[pl.ds(2 * g, SEL_BLOCK, stride=slabs), :])
            vs.append(ref[pl.ds(2 * g + 1, SEL_BLOCK, stride=slabs), :])
            kpos.append(jnp.where(is_new, jnp.int32(INT_MAX), b_id * SEL_BLOCK + row))
        ks.append(jnp.concatenate([newkv[:, (2 * g) * C_DH:(2 * g + 1) * C_DH], padn], axis=0))
        vs.append(jnp.concatenate([newkv[:, (2 * g + 1) * C_DH:(2 * g + 2) * C_DH], padn], axis=0))
        kpos.append(jnp.where((has_new > 0) & (ncol < steps_q), past_len + ncol, jnp.int32(INT_MAX)))
        k = jnp.concatenate(ks, axis=0).astype(jnp.bfloat16)
        v = jnp.concatenate(vs, axis=0).astype(jnp.bfloat16)
        mask = jnp.concatenate(kpos, axis=-1) <= qpos
        s = lax.dot_general(q, k, _NT, preferred_element_type=jnp.float32) * scale
        s = jnp.where(mask, s, NEG)
        e = jnp.where(mask, jnp.exp(s - jnp.max(s, axis=-1, keepdims=True)), 0.0)
        l = jnp.sum(e, axis=-1, keepdims=True)
        p = e / jnp.where(l > 0.0, l, 1.0)
        o_slc = jnp.dot(p.astype(jnp.bfloat16), v, preferred_element_type=jnp.float32)
        hs = slice(g * C_REP, (g + 1) * C_REP)
        o_ref[0, hs, :] = (sig[hs, 0:1] * ocmp_ref[0, hs, :] + sig[hs, 1:2] * o_slc
                           + sig[hs, 2:3] * owin_ref[0, hs, :])


def nsa_sample_selected(sel_idx, q_rot, gates, o_cmp, o_win, kv_s, cache_slc, page_table):
    bd, steps_q, qd = q_rot.shape
    n_pages = page_table.shape[1]
    past_len = n_pages * PAGE_SIZE
    n_pool = cache_slc.shape[0]
    bpp = PAGE_SIZE // SEL_BLOCK
    slabs = C_KV * 2
    blk = jnp.transpose(sel_idx[:, :, :steps_q, :N_SEL], (0, 2, 1, 3))
    page = jnp.minimum(blk // bpp, n_pages - 1)
    phys = jnp.take_along_axis(page_table, page.reshape(bd, -1), axis=1).reshape(blk.shape) * bpp + blk % bpp
    n = bd * steps_q
    heads = lambda a: a.reshape(n, C_HEADS, C_DH)

    def blk_spec(g, t):
        return pl.BlockSpec((1, SEL_BLOCK * slabs, C_DH),
                            lambda i, ph, bl: (ph[(i * C_KV + g) * N_SEL + t], 0, 0))

    row_spec = lambda shape: pl.BlockSpec((1,) + shape, lambda i, ph, bl: (i,) + (0,) * len(shape))
    grid_spec = pltpu.PrefetchScalarGridSpec(
        num_scalar_prefetch=2,
        grid=(n,),
        in_specs=[row_spec((C_HEADS, C_DH)), row_spec((C_HEADS, 3)), row_spec((C_HEADS, C_DH)),
                  row_spec((C_HEADS, C_DH)),
                  pl.BlockSpec((1, steps_q, slabs * C_DH), lambda i, ph, bl: (i // steps_q, 0, 0))]
        + [blk_spec(g, t) for g in range(C_KV) for t in range(N_SEL)],
        out_specs=row_spec((C_HEADS, C_DH)),
    )
    o = pl.pallas_call(
        functools.partial(_nsa_sample_selected_kernel, past_len=past_len, steps_q=steps_q),
        grid_spec=grid_spec,
        out_shape=jax.ShapeDtypeStruct((n, C_HEADS, C_DH), jnp.float32),
        compiler_params=pltpu.CompilerParams(dimension_semantics=("arbitrary",), vmem_limit_bytes=VMEM_LIMIT_BYTES),
        name="nsa_sample_selected",
    )(phys.reshape(-1).astype(jnp.int32), blk.reshape(-1).astype(jnp.int32),
      heads(q_rot), gates.reshape(n, C_HEADS, 3), heads(o_cmp), heads(o_win),
      kv_s.reshape(bd, steps_q, slabs * C_DH),
      *([cache_slc.reshape(n_pool * bpp, SEL_BLOCK * slabs, C_DH)] * (C_KV * N_SEL)))
    return o.reshape(bd, steps_q, qd)


LN_ROW_TILE = 512


def _layer_norm_rows(y, g, b):
    mu = jnp.mean(y, axis=-1, keepdims=True)
    var = jnp.mean(jnp.square(y - mu), axis=-1, keepdims=True)
    return (y - mu) * lax.rsqrt(var + LN_EPS) * g + b


def _outproj_ln_kernel(*refs, n_lhs):
    lhs_refs, w_refs = refs[:n_lhs], refs[n_lhs:2 * n_lhs]
    x_ref, g_ref, b_ref, o_ref, obf_ref = refs[2 * n_lhs:]
    acc = None
    for a, w in zip(lhs_refs, w_refs):
        t = jnp.dot(a[...].astype(jnp.bfloat16), w[...], preferred_element_type=jnp.float32)
        acc = t if acc is None else acc + t
    out = _layer_norm_rows(ALPHA * x_ref[...] + acc, g_ref[...], b_ref[...])
    o_ref[...] = out
    obf_ref[...] = out.astype(jnp.bfloat16)


def outproj_ln(lhs_list, w_out, x, g, b):
    t = x.shape[0]
    tm = min(LN_ROW_TILE, t)
    assert t % tm == 0
    ws, start = [], 0
    for a in lhs_list:
        ws.append(w_out[start:start + a.shape[1]].astype(jnp.bfloat16))
        start += a.shape[1]
    rows = lambda width: pl.BlockSpec((tm, width), lambda i: (i, 0))
    const = lambda shape: pl.BlockSpec(shape, lambda i: (0, 0))
    return pl.pallas_call(
        functools.partial(_outproj_ln_kernel, n_lhs=len(lhs_list)),
        grid=(t // tm,),
        in_specs=[rows(a.shape[1]) for a in lhs_list] + [const(w.shape) for w in ws]
        + [rows(D_MODEL), const((1, D_MODEL)), const((1, D_MODEL))],
        out_specs=[rows(D_MODEL), rows(D_MODEL)],
        out_shape=[jax.ShapeDtypeStruct((t, D_MODEL), jnp.float32), jax.ShapeDtypeStruct((t, D_MODEL), jnp.bfloat16)],
        compiler_params=pltpu.CompilerParams(dimension_semantics=("arbitrary",), vmem_limit_bytes=VMEM_LIMIT_BYTES),
        name="outproj_ln",
    )(*lhs_list, *ws, x, g.reshape(1, D_MODEL), b.reshape(1, D_MODEL))


def _combine_ln_kernel(x_ref, ya_ref, yb_ref, g_ref, b_ref, o_ref):
    o_ref[...] = _layer_norm_rows(ALPHA * x_ref[...] + (ya_ref[...] + yb_ref[...]), g_ref[...], b_ref[...])


def combine_ln(x, ya, yb, g, b):
    t = x.shape[0]
    tm = min(LN_ROW_TILE, t)
    assert t % tm == 0
    rows = pl.BlockSpec((tm, D_MODEL), lambda i: (i, 0))
    const = pl.BlockSpec((1, D_MODEL), lambda i: (0, 0))
    return pl.pallas_call(
        _combine_ln_kernel,
        grid=(t // tm,),
        in_specs=[rows, rows, rows, const, const],
        out_specs=rows,
        out_shape=jax.ShapeDtypeStruct((t, D_MODEL), jnp.float32),
        compiler_params=pltpu.CompilerParams(dimension_semantics=("arbitrary",), vmem_limit_bytes=VMEM_LIMIT_BYTES),
        name="combine_ln",
    )(x, ya, yb, g.reshape(1, D_MODEL), b.reshape(1, D_MODEL))


def split_cols(h, sizes):
    out, start = [], 0
    for n in sizes:
        out.append(h[..., start:start + n])
        start += n
    return out


def rope_partial(x, pos, rot_dim):
    half = rot_dim // 2
    inv = ROPE_THETA ** (-jnp.arange(half, dtype=jnp.float32) / half)
    ang = pos.astype(jnp.float32)[:, None] * inv
    cos, sin = jnp.cos(ang)[:, None, :], jnp.sin(ang)[:, None, :]
    x1 = x[..., :half]
    x2 = x[..., half:rot_dim]
    rot = jnp.concatenate([x1 * cos - x2 * sin, x2 * cos + x1 * sin], -1)
    return jnp.concatenate([rot, x[..., rot_dim:]], -1)


def rope_kv(kv, pos, rot_dim):
    k = rope_partial(kv[..., 0, :], pos, rot_dim)
    return jnp.stack([k, kv[..., 1, :]], axis=-2)


def mixer_ab(x, pos, w_in, a_re, a_im, log_dt, b_re, b_im, c_re, c_im, d, w_glu,
             state=None, cache_kv=None, cache_idx=None, page_table=None):
    bn, s, _ = x.shape
    s5p = (a_re, a_im, log_dt, b_re, b_im, c_re, c_im, d, w_glu)
    u, q, kv, qi, ki, wi = split_cols(x @ w_in, AB_SIZES)
    q = rope_partial(q.reshape(bn, s, B_HEADS, B_DH), pos, B_ROT)
    kv = rope_kv(kv.reshape(bn, s, B_KV_HEADS, 2, B_DH), pos, B_ROT)
    qi = rope_partial(qi.reshape(bn, s, IDX_HEADS, IDX_DH), pos, IDX_ROT)
    ki = rope_partial(ki.reshape(bn, s, 1, IDX_DH), pos, IDX_ROT)[:, :, 0]
    if state is None:
        o_b = dsa_prompt(q.reshape(bn, s, B_HEADS * B_DH), kv, qi.reshape(bn, s, IDX_HEADS * IDX_DH), ki, wi)
        y_s5, new_a = s5_prompt(u, *s5p)
    else:
        o_b = dsa_sample(q.reshape(bn, s, B_HEADS * B_DH), kv, qi.reshape(bn, s, IDX_HEADS * IDX_DH), ki, wi,
                         cache_kv, cache_idx, page_table)
        y_s5, new_a = s5_sample(u, state, *s5p)
    return [y_s5.reshape(bn * s, S5_WIDTH), o_b.reshape(bn * s, B_HEADS * B_DH)], new_a, kv, ki


def mixer_c(x, pos, w_in, cmp_pos, cmp_w1, cmp_w2,
            cache_cmp=None, cache_slc=None, win=None, page_table=None):
    bn, s, _ = x.shape
    q, kv_c, kv_s, kv_w, gates = split_cols(x @ w_in, C_SIZES)
    q = q.reshape(bn, s, C_HEADS, C_DH)
    q_rot = rope_partial(q, pos, C_ROT)
    kv_c = kv_c.reshape(bn, s, C_KV, 2, C_DH)
    kv_s = rope_kv(kv_s.reshape(bn, s, C_KV, 2, C_DH), pos, C_ROT)
    kv_w = rope_kv(kv_w.reshape(bn, s, C_KV, 2, C_DH), pos, C_ROT)
    qd = C_HEADS * C_DH
    q_raw2, q_rot2 = q.reshape(bn, s, qd), q_rot.reshape(bn, s, qd)
    if cache_cmp is None:
        n_pages = s // PAGE_SIZE
        ident = jnp.arange(bn * n_pages, dtype=jnp.int32).reshape(bn, n_pages)
        kck, kcv = nsa_compress(kv_c.reshape(bn * n_pages, PAGE_SIZE, C_KV, 2, C_DH), ident, cmp_pos, cmp_w1, cmp_w2)
        o = nsa_prompt(q_raw2, q_rot2, gates, kck, kcv, kv_s, kv_w)
        return [o.reshape(bn * s, qd)], kv_c, kv_s, kv_w[:, -min(WINDOW, s):]
    kck, kcv = nsa_compress(cache_cmp, page_table, cmp_pos, cmp_w1, cmp_w2)
    o_cmp, o_win, sel_idx = nsa_sample_attend(q_raw2, q_rot2, kck, kcv, win, kv_w)
    o = nsa_sample_selected(sel_idx, q_rot2, gates, o_cmp, o_win, kv_s, cache_slc, page_table)
    new_win = jnp.concatenate([win, kv_w], axis=1)[:, -win.shape[1]:]
    return [o.reshape(bn * s, qd)], kv_c, kv_s, new_win


def kernel(x_prompt, x_sample, state_a, cache_b_kv, cache_b_idx, cache_c_cmp, cache_c_slc, state_c_win, page_table,
           w_in_ab, s5_a_re, s5_a_im, s5_log_dt, s5_b_re, s5_b_im, s5_c_re, s5_c_im, s5_d, s5_w_glu, w_out_ab,
           w_in_c, cmp_pos, cmp_w1, cmp_w2, w_out_c,
           ln1_g, ln1_b, ln2_g, ln2_b, moe_w_group, moe_w_expert, moe_w_gate, moe_w_up, moe_w_down):
    past_len = page_table.shape[1] * PAGE_SIZE
    pos_p = jnp.arange(x_prompt.shape[1])
    pos_s = past_len + jnp.arange(x_sample.shape[1])
    xp, xs = x_prompt, x_sample
    pshape, sshape = xp.shape, xs.shape
    n_p = pshape[0] * pshape[1]
    outs = {k: [] for k in ('a_p', 'a_s', 'bkv_p', 'bkv_s', 'bidx_p', 'bidx_s',
                            'ccmp_p', 'ccmp_s', 'cslc_p', 'cslc_s', 'cwin_p', 'cwin_s')}
    for i in range(DEPTH):
        j = i // 2
        if i % 2 == 0:
            ab = (w_in_ab[j], s5_a_re[j], s5_a_im[j], s5_log_dt[j], s5_b_re[j], s5_b_im[j],
                  s5_c_re[j], s5_c_im[j], s5_d[j], s5_w_glu[j])
            w_out = w_out_ab[j]
            mp, st_p, kv_p, ki_p = mixer_ab(xp, pos_p, *ab)
            ms, st_s, kv_n, ki_n = mixer_ab(xs, pos_s, *ab, state=state_a[j], cache_kv=cache_b_kv[j],
                                            cache_idx=cache_b_idx[j], page_table=page_table)
            outs['a_p'].append(st_p); outs['a_s'].append(st_s)
            outs['bkv_p'].append(kv_p); outs['bkv_s'].append(kv_n)
            outs['bidx_p'].append(ki_p); outs['bidx_s'].append(ki_n)
        else:
            cp = (w_in_c[j], cmp_pos[j], cmp_w1[j], cmp_w2[j])
            w_out = w_out_c[j]
            mp, kc_p, ksl_p, win_p = mixer_c(xp, pos_p, *cp)
            ms, kc_n, ksl_n, win_n = mixer_c(xs, pos_s, *cp, cache_cmp=cache_c_cmp[j], cache_slc=cache_c_slc[j],
                                             win=state_c_win[j], page_table=page_table)
            outs['ccmp_p'].append(kc_p); outs['ccmp_s'].append(kc_n)
            outs['cslc_p'].append(ksl_p); outs['cslc_s'].append(ksl_n)
            outs['cwin_p'].append(win_p); outs['cwin_s'].append(win_n)
        xp2, xp_bf = outproj_ln(mp, w_out, xp.reshape(n_p, D_MODEL), ln1_g[i], ln1_b[i])
        xs2, xs_bf = outproj_ln(ms, w_out, xs.reshape(-1, D_MODEL), ln1_g[i], ln1_b[i])
        moe = (moe_w_group[i], moe_w_expert[i], moe_w_gate[i], moe_w_up[i], moe_w_down[i])
        (yp_a, yp_b), (ys_a, ys_b) = hier_moe([xp2, xs2], [xp_bf, xs_bf], *moe)
        xp = combine_ln(xp2, yp_a, yp_b, ln2_g[i], ln2_b[i]).reshape(pshape)
        xs = combine_ln(xs2, ys_a, ys_b, ln2_g[i], ln2_b[i]).reshape(sshape)
    st = lambda k: jnp.stack(outs[k])
    return (xp, xs, st('a_p'), st('a_s'), st('bkv_p'), st('bkv_s'), st('bidx_p'), st('bidx_s'),
            st('ccmp_p'), st('ccmp_s'), st('cslc_p'), st('cslc_s'), st('cwin_p'), st('cwin_s'))
```
